```python
import jax, jax.numpy as jnp
from jax import lax
import numpy as np

D_MODEL = 1024
BATCH = 8
SEQ = 16384
DEPTH = 4

CHUNK = 64
SGU_BLOCK = 128
D_A = 1024
A_GROUPS = 8
A_GROUP_DIM = D_A // A_GROUPS
D_B = 1024
POOL_WINDOWS = (2, 4, 8, 16)
B_GROUPS = len(POOL_WINDOWS)
B_GROUP_DIM = D_B // B_GROUPS
D_C = 1024
CONV_WIDTH = 3
N_BRANCH = 3
SPLIT_SIZES = (D_A, D_A, D_A, D_B, D_B, D_C, D_C, D_C, D_C, N_BRANCH * D_MODEL)
N_IN = sum(SPLIT_SIZES)
SPLIT_OFFSETS = tuple(int(o) for o in np.cumsum(SPLIT_SIZES)[:-1])
RMS_EPS = 1e-6
LN_EPS = 1e-5

kernel_name = "hybrid_gated_parallel_mixers"


def rmsnorm(x, g):
    xf = x.astype(jnp.float32)
    y = xf * lax.rsqrt(jnp.mean(xf * xf, axis=-1, keepdims=True) + RMS_EPS)
    return (y * g.astype(jnp.float32)).astype(x.dtype)


def layernorm(x, g, b):
    xf = x.astype(jnp.float32)
    mu = jnp.mean(xf, axis=-1, keepdims=True)
    var = jnp.mean(jnp.square(xf - mu), axis=-1, keepdims=True)
    y = (xf - mu) * lax.rsqrt(var + LN_EPS)
    return (y * g.astype(jnp.float32) + b.astype(jnp.float32)).astype(x.dtype)


def chunk_causal_mask():
    c = jnp.arange(SGU_BLOCK) // CHUNK
    return c[None, :] <= c[:, None]


def spatial_gating(u, v, w_s, b_s, ln_g, ln_b):
    bsz, s, _ = v.shape
    v = layernorm(v, ln_g, ln_b)
    vb = v.reshape(bsz, s // SGU_BLOCK, SGU_BLOCK, A_GROUPS, A_GROUP_DIM)
    w = jnp.where(chunk_causal_mask()[None], w_s, jnp.zeros_like(w_s))
    mixed = jnp.einsum('gij,bnjgc->bnigc', w, vb) + b_s.T[:, :, None]
    return u * mixed.reshape(bsz, s, D_A)


def multiscale_pool(p, w_g, b_g, scale):
    bsz, s, _ = p.shape
    pf = p.astype(jnp.float32)
    csum = jnp.cumsum(pf, axis=1)
    pos1 = jnp.arange(1, s + 1, dtype=jnp.int32)
    outs = []
    for k, win in enumerate(POOL_WINDOWS):
        sl = slice(k * B_GROUP_DIM, (k + 1) * B_GROUP_DIM)
        cg = csum[..., sl]
        shifted = jnp.pad(cg, ((0, 0), (win, 0), (0, 0)))[:, :s]
        count = jnp.minimum(pos1, win).astype(jnp.float32)[:, None]
        outs.append((cg - shifted) / count - pf[..., sl])
    d = jnp.concatenate(outs, axis=-1).astype(p.dtype)
    d = d.reshape(bsz, s, B_GROUPS, B_GROUP_DIM)
    y = jnp.einsum('bsgc,gcd->bsgd', d, w_g).reshape(bsz, s, D_B) + b_g
    return y * scale


def causal_depthwise_conv(h, w, b):
    s = h.shape[1]
    hp = jnp.pad(h, ((0, 0), (CONV_WIDTH - 1, 0), (0, 0)))
    y = sum(w[k] * hp[:, k:k + s] for k in range(CONV_WIDTH))
    return y + b


def _fwd_setup_inputs(seed: int = 0) -> dict:
    key = jax.random.key(seed)
    ks = jax.random.split(key, 20)

    def nrm(k, shape, scale):
        return jax.random.normal(k, shape, jnp.float32) * scale

    L = DEPTH
    return {
        "x": nrm(ks[0], (BATCH, SEQ, D_MODEL), 1.0),
        "norm_g": 1.0 + nrm(ks[1], (L, D_MODEL), 0.05),
        "w_in": nrm(ks[2], (L, D_MODEL, N_IN), D_MODEL ** -0.5),
        "sgu_ln_g": 1.0 + nrm(ks[3], (L, D_A), 0.05),
        "sgu_ln_b": nrm(ks[4], (L, D_A), 0.02),
        "sgu_w": nrm(ks[5], (L, A_GROUPS, SGU_BLOCK, SGU_BLOCK), SGU_BLOCK ** -0.5),
        "sgu_b": 1.0 + nrm(ks[6], (L, A_GROUPS, SGU_BLOCK), 0.05),
        "pool_w": nrm(ks[7], (L, B_GROUPS, B_GROUP_DIM, B_GROUP_DIM), B_GROUP_DIM ** -0.5),
        "pool_b": nrm(ks[8], (L, D_B), 0.01),
        "pool_scale": 1.0 + nrm(ks[9], (L, D_B), 0.1),
        "conv_w": nrm(ks[10], (L, CONV_WIDTH, D_C), CONV_WIDTH ** -0.5),
        "conv_b": nrm(ks[11], (L, D_C), 0.01),
        "w_branch_a": nrm(ks[12], (L, D_A, D_MODEL), D_A ** -0.5),
        "w_branch_b": nrm(ks[13], (L, D_B, D_MODEL), D_B ** -0.5),
        "w_branch_c": nrm(ks[14], (L, D_C, D_MODEL), D_C ** -0.5),
        "w_out": nrm(ks[15], (L, D_MODEL, D_MODEL), D_MODEL ** -0.5),
        "final_g": 1.0 + nrm(ks[16], (D_MODEL,), 0.05),
    }


def _fwd_reference(x, norm_g, w_in, sgu_ln_g, sgu_ln_b, sgu_w, sgu_b, pool_w, pool_b,
              pool_scale, conv_w, conv_b, w_branch_a, w_branch_b, w_branch_c,
              w_out, final_g):
    bsz, s, _ = x.shape
    for l in range(DEPTH):
        h = rmsnorm(x, norm_g[l])
        proj = jnp.einsum('bsd,dn->bsn', h, w_in[l])
        a_u, a_v, a_z, b_p, b_z, c_h, c_b, c_c, c_z, gate_logits = jnp.split(
            proj, SPLIT_OFFSETS, axis=-1)

        ya = spatial_gating(jax.nn.gelu(a_u), jax.nn.gelu(a_v), sgu_w[l], sgu_b[l],
                            sgu_ln_g[l], sgu_ln_b[l]) * jax.nn.silu(a_z)
        yb = multiscale_pool(b_p, pool_w[l], pool_b[l], pool_scale[l]) * jax.nn.silu(b_z)
        yc = c_b * causal_depthwise_conv(c_c * c_h, conv_w[l], conv_b[l]) * jax.nn.silu(c_z)

        gates = jax.nn.sigmoid(gate_logits.reshape(bsz, s, N_BRANCH, D_MODEL))
        merged = (gates[:, :, 0] * jnp.einsum('bsc,cd->bsd', ya, w_branch_a[l])
                  + gates[:, :, 1] * jnp.einsum('bsc,cd->bsd', yb, w_branch_b[l])
                  + gates[:, :, 2] * jnp.einsum('bsc,cd->bsd', yc, w_branch_c[l]))
        x = x + jnp.einsum('bsd,de->bse', merged, w_out[l])
    return rmsnorm(x, final_g)


import jax as _jax
import jax.numpy as _jnp

TWIN_FORMAT = 'train_step'
FWD_PARAMS = ['x', 'norm_g', 'w_in', 'sgu_ln_g', 'sgu_ln_b', 'sgu_w', 'sgu_b', 'pool_w', 'pool_b', 'pool_scale', 'conv_w', 'conv_b', 'w_branch_a', 'w_branch_b', 'w_branch_c', 'w_out', 'final_g']
TWIN_WEIGHTS = ['norm_g', 'w_in', 'sgu_ln_g', 'sgu_ln_b', 'sgu_w', 'sgu_b', 'pool_w', 'pool_b', 'pool_scale', 'conv_w', 'conv_b', 'w_branch_a', 'w_branch_b', 'w_branch_c', 'w_out', 'final_g']
TWIN_DIFF_INPUT = 'x'
TWIN_INPUTS = ['x', 'norm_g', 'w_in', 'sgu_ln_g', 'sgu_ln_b', 'sgu_w', 'sgu_b', 'pool_w', 'pool_b', 'pool_scale', 'conv_w', 'conv_b', 'w_branch_a', 'w_branch_b', 'w_branch_c', 'w_out', 'final_g', 'loss_target', 'm_norm_g', 'm_w_in', 'm_sgu_ln_g', 'm_sgu_ln_b', 'm_sgu_w', 'm_sgu_b', 'm_pool_w', 'm_pool_b', 'm_pool_scale', 'm_conv_w', 'm_conv_b', 'm_w_branch_a', 'm_w_branch_b', 'm_w_branch_c', 'm_w_out', 'm_final_g', 'v_norm_g', 'v_w_in', 'v_sgu_ln_g', 'v_sgu_ln_b', 'v_sgu_w', 'v_sgu_b', 'v_pool_w', 'v_pool_b', 'v_pool_scale', 'v_conv_w', 'v_conv_b', 'v_w_branch_a', 'v_w_branch_b', 'v_w_branch_c', 'v_w_out', 'v_final_g']
TWIN_OUTPUTS = ['loss', 'grad_x', 'grad_norm_g', 'grad_w_in', 'grad_sgu_ln_g', 'grad_sgu_ln_b', 'grad_sgu_w', 'grad_sgu_b', 'grad_pool_w', 'grad_pool_b', 'grad_pool_scale', 'grad_conv_w', 'grad_conv_b', 'grad_w_branch_a', 'grad_w_branch_b', 'grad_w_branch_c', 'grad_w_out', 'grad_final_g', 'delta_norm_g', 'delta_w_in', 'delta_sgu_ln_g', 'delta_sgu_ln_b', 'delta_sgu_w', 'delta_sgu_b', 'delta_pool_w', 'delta_pool_b', 'delta_pool_scale', 'delta_conv_w', 'delta_conv_b', 'delta_w_branch_a', 'delta_w_branch_b', 'delta_w_branch_c', 'delta_w_out', 'delta_final_g', 'new_m_norm_g', 'new_m_w_in', 'new_m_sgu_ln_g', 'new_m_sgu_ln_b', 'new_m_sgu_w', 'new_m_sgu_b', 'new_m_pool_w', 'new_m_pool_b', 'new_m_pool_scale', 'new_m_conv_w', 'new_m_conv_b', 'new_m_w_branch_a', 'new_m_w_branch_b', 'new_m_w_branch_c', 'new_m_w_out', 'new_m_final_g', 'new_v_norm_g', 'new_v_w_in', 'new_v_sgu_ln_g', 'new_v_sgu_ln_b', 'new_v_sgu_w', 'new_v_sgu_b', 'new_v_pool_w', 'new_v_pool_b', 'new_v_pool_scale', 'new_v_conv_w', 'new_v_conv_b', 'new_v_w_branch_a', 'new_v_w_branch_b', 'new_v_w_branch_c', 'new_v_w_out', 'new_v_final_g']
TWIN_LEAF_KINDS = {'loss': 'loss', 'grad_x': 'grad_x', 'grad_norm_g': 'grad_w', 'grad_w_in': 'grad_w', 'grad_sgu_ln_g': 'grad_w', 'grad_sgu_ln_b': 'grad_w', 'grad_sgu_w': 'grad_w', 'grad_sgu_b': 'grad_w', 'grad_pool_w': 'grad_w', 'grad_pool_b': 'grad_w', 'grad_pool_scale': 'grad_w', 'grad_conv_w': 'grad_w', 'grad_conv_b': 'grad_w', 'grad_w_branch_a': 'grad_w', 'grad_w_branch_b': 'grad_w', 'grad_w_branch_c': 'grad_w', 'grad_w_out': 'grad_w', 'grad_final_g': 'grad_w', 'delta_norm_g': 'delta_w', 'delta_w_in': 'delta_w', 'delta_sgu_ln_g': 'delta_w', 'delta_sgu_ln_b': 'delta_w', 'delta_sgu_w': 'delta_w', 'delta_sgu_b': 'delta_w', 'delta_pool_w': 'delta_w', 'delta_pool_b': 'delta_w', 'delta_pool_scale': 'delta_w', 'delta_conv_w': 'delta_w', 'delta_conv_b': 'delta_w', 'delta_w_branch_a': 'delta_w', 'delta_w_branch_b': 'delta_w', 'delta_w_branch_c': 'delta_w', 'delta_w_out': 'delta_w', 'delta_final_g': 'delta_w', 'new_m_norm_g': 'new_m', 'new_m_w_in': 'new_m', 'new_m_sgu_ln_g': 'new_m', 'new_m_sgu_ln_b': 'new_m', 'new_m_sgu_w': 'new_m', 'new_m_sgu_b': 'new_m', 'new_m_pool_w': 'new_m', 'new_m_pool_b': 'new_m', 'new_m_pool_scale': 'new_m', 'new_m_conv_w': 'new_m', 'new_m_conv_b': 'new_m', 'new_m_w_branch_a': 'new_m', 'new_m_w_branch_b': 'new_m', 'new_m_w_branch_c': 'new_m', 'new_m_w_out': 'new_m', 'new_m_final_g': 'new_m', 'new_v_norm_g': 'new_v', 'new_v_w_in': 'new_v', 'new_v_sgu_ln_g': 'new_v', 'new_v_sgu_ln_b': 'new_v', 'new_v_sgu_w': 'new_v', 'new_v_sgu_b': 'new_v', 'new_v_pool_w': 'new_v', 'new_v_pool_b': 'new_v', 'new_v_pool_scale': 'new_v', 'new_v_conv_w': 'new_v', 'new_v_conv_b': 'new_v', 'new_v_w_branch_a': 'new_v', 'new_v_w_branch_b': 'new_v', 'new_v_w_branch_c': 'new_v', 'new_v_w_out': 'new_v', 'new_v_final_g': 'new_v'}


def _forward(args):
    return _fwd_reference(*[args[k] for k in FWD_PARAMS])


def _output_shape():
    def fwd():
        inp = _fwd_setup_inputs(0)
        return _fwd_reference(*[inp[k] for k in FWD_PARAMS])
    out = _jax.eval_shape(fwd)
    return out.shape, out.dtype

N_MICROBATCH = 1
ADAM_LR = 0.001
ADAM_B1 = 0.9
ADAM_B2 = 0.999
ADAM_EPS = 1e-08
ADAM_WD = 0.01
ADAM_STEP = 10
PER_EXAMPLE_BATCH_AXIS = {'x': 0, 'loss_target': 0}
SHARED_INPUTS = []
_WEIGHT_DTYPES = {'norm_g': _jnp.float32, 'w_in': _jnp.float32, 'sgu_ln_g': _jnp.float32, 'sgu_ln_b': _jnp.float32, 'sgu_w': _jnp.float32, 'sgu_b': _jnp.float32, 'pool_w': _jnp.float32, 'pool_b': _jnp.float32, 'pool_scale': _jnp.float32, 'conv_w': _jnp.float32, 'conv_b': _jnp.float32, 'w_branch_a': _jnp.float32, 'w_branch_b': _jnp.float32, 'w_branch_c': _jnp.float32, 'w_out': _jnp.float32, 'final_g': _jnp.float32}
MOMENT_SCALE = {'norm_g': 3.391658e-01, 'w_in': 9.695521e-02, 'sgu_ln_g': 6.809796e-02, 'sgu_ln_b': 6.540155e-02, 'sgu_w': 6.652775e-02, 'sgu_b': 7.811751e-02, 'pool_w': 1.056861e-01, 'pool_b': 1.192323e-01, 'pool_scale': 1.042594e-01, 'conv_w': 1.184101e-01, 'conv_b': 1.176097e-01, 'w_branch_a': 1.023314e-01, 'w_branch_b': 1.052261e-01, 'w_branch_c': 1.177865e-01, 'w_out': 1.886476e-01, 'final_g': 1.284493e+02}


def _to_microbatches(a, axis):
    t = _jnp.moveaxis(a, axis, 0)
    t = t.reshape((N_MICROBATCH, t.shape[0] // N_MICROBATCH) + t.shape[1:])
    return _jnp.moveaxis(t, 1, axis + 1)


def setup_inputs(seed: int = 0) -> dict:
    inp = _fwd_setup_inputs(seed)
    key = _jax.random.fold_in(_jax.random.key(seed), 7919)
    shape, _ = _output_shape()
    out = dict(inp)
    out["loss_target"] = _jax.random.normal(_jax.random.fold_in(key, 0), shape, _jnp.float32)
    for i, name in enumerate(TWIN_WEIGHTS):
        w = inp[name].astype(_jnp.float32)
        if MOMENT_SCALE is None:
            s = _jnp.sqrt(_jnp.mean(_jnp.square(w)) + 1e-30)
        else:
            s = MOMENT_SCALE[name]
        km, kv = _jax.random.split(_jax.random.fold_in(key, i + 1))
        out[name] = w
        out["m_" + name] = s * _jax.random.normal(km, w.shape, _jnp.float32)
        out["v_" + name] = (s * s) * _jax.random.uniform(kv, w.shape, _jnp.float32, 0.5, 1.5)
    if N_MICROBATCH > 1:
        for name, axis in PER_EXAMPLE_BATCH_AXIS.items():
            out[name] = _to_microbatches(out[name], axis)
    return {'x': out['x'], 'norm_g': out['norm_g'], 'w_in': out['w_in'], 'sgu_ln_g': out['sgu_ln_g'], 'sgu_ln_b': out['sgu_ln_b'], 'sgu_w': out['sgu_w'], 'sgu_b': out['sgu_b'], 'pool_w': out['pool_w'], 'pool_b': out['pool_b'], 'pool_scale': out['pool_scale'], 'conv_w': out['conv_w'], 'conv_b': out['conv_b'], 'w_branch_a': out['w_branch_a'], 'w_branch_b': out['w_branch_b'], 'w_branch_c': out['w_branch_c'], 'w_out': out['w_out'], 'final_g': out['final_g'], 'loss_target': out['loss_target'], 'm_norm_g': out['m_norm_g'], 'm_w_in': out['m_w_in'], 'm_sgu_ln_g': out['m_sgu_ln_g'], 'm_sgu_ln_b': out['m_sgu_ln_b'], 'm_sgu_w': out['m_sgu_w'], 'm_sgu_b': out['m_sgu_b'], 'm_pool_w': out['m_pool_w'], 'm_pool_b': out['m_pool_b'], 'm_pool_scale': out['m_pool_scale'], 'm_conv_w': out['m_conv_w'], 'm_conv_b': out['m_conv_b'], 'm_w_branch_a': out['m_w_branch_a'], 'm_w_branch_b': out['m_w_branch_b'], 'm_w_branch_c': out['m_w_branch_c'], 'm_w_out': out['m_w_out'], 'm_final_g': out['m_final_g'], 'v_norm_g': out['v_norm_g'], 'v_w_in': out['v_w_in'], 'v_sgu_ln_g': out['v_sgu_ln_g'], 'v_sgu_ln_b': out['v_sgu_ln_b'], 'v_sgu_w': out['v_sgu_w'], 'v_sgu_b': out['v_sgu_b'], 'v_pool_w': out['v_pool_w'], 'v_pool_b': out['v_pool_b'], 'v_pool_scale': out['v_pool_scale'], 'v_conv_w': out['v_conv_w'], 'v_conv_b': out['v_conv_b'], 'v_w_branch_a': out['v_w_branch_a'], 'v_w_branch_b': out['v_w_branch_b'], 'v_w_branch_c': out['v_w_branch_c'], 'v_w_out': out['v_w_out'], 'v_final_g': out['v_final_g']}


def _loss(weights, diff, rest, loss_target):
    with _jax.named_scope("forward"):
        args = {**rest, TWIN_DIFF_INPUT: diff, **{k: w.astype(_WEIGHT_DTYPES[k]) for k, w in weights.items()}}
        y = _forward(args)
    with _jax.named_scope("loss_head"):
        err = _jnp.square(y.astype(_jnp.float32) - loss_target)
        return 0.5 * _jnp.sum(_jnp.mean(err, axis=-1)) if err.ndim else 0.5 * err


def _adamw(w, g, m, v):
    m = ADAM_B1 * m + (1.0 - ADAM_B1) * g
    v = ADAM_B2 * v + (1.0 - ADAM_B2) * _jnp.square(g)
    m_hat = m / (1.0 - ADAM_B1 ** ADAM_STEP)
    v_hat = v / (1.0 - ADAM_B2 ** ADAM_STEP)
    delta = -ADAM_LR * (m_hat / (_jnp.sqrt(v_hat) + ADAM_EPS) + ADAM_WD * w)
    return delta, m, v


def reference(x, norm_g, w_in, sgu_ln_g, sgu_ln_b, sgu_w, sgu_b, pool_w, pool_b, pool_scale, conv_w, conv_b, w_branch_a, w_branch_b, w_branch_c, w_out, final_g, loss_target, m_norm_g, m_w_in, m_sgu_ln_g, m_sgu_ln_b, m_sgu_w, m_sgu_b, m_pool_w, m_pool_b, m_pool_scale, m_conv_w, m_conv_b, m_w_branch_a, m_w_branch_b, m_w_branch_c, m_w_out, m_final_g, v_norm_g, v_w_in, v_sgu_ln_g, v_sgu_ln_b, v_sgu_w, v_sgu_b, v_pool_w, v_pool_b, v_pool_scale, v_conv_w, v_conv_b, v_w_branch_a, v_w_branch_b, v_w_branch_c, v_w_out, v_final_g):
    given = dict(x=x, norm_g=norm_g, w_in=w_in, sgu_ln_g=sgu_ln_g, sgu_ln_b=sgu_ln_b, sgu_w=sgu_w, sgu_b=sgu_b, pool_w=pool_w, pool_b=pool_b, pool_scale=pool_scale, conv_w=conv_w, conv_b=conv_b, w_branch_a=w_branch_a, w_branch_b=w_branch_b, w_branch_c=w_branch_c, w_out=w_out, final_g=final_g, loss_target=loss_target, m_norm_g=m_norm_g, m_w_in=m_w_in, m_sgu_ln_g=m_sgu_ln_g, m_sgu_ln_b=m_sgu_ln_b, m_sgu_w=m_sgu_w, m_sgu_b=m_sgu_b, m_pool_w=m_pool_w, m_pool_b=m_pool_b, m_pool_scale=m_pool_scale, m_conv_w=m_conv_w, m_conv_b=m_conv_b, m_w_branch_a=m_w_branch_a, m_w_branch_b=m_w_branch_b, m_w_branch_c=m_w_branch_c, m_w_out=m_w_out, m_final_g=m_final_g, v_norm_g=v_norm_g, v_w_in=v_w_in, v_sgu_ln_g=v_sgu_ln_g, v_sgu_ln_b=v_sgu_ln_b, v_sgu_w=v_sgu_w, v_sgu_b=v_sgu_b, v_pool_w=v_pool_w, v_pool_b=v_pool_b, v_pool_scale=v_pool_scale, v_conv_w=v_conv_w, v_conv_b=v_conv_b, v_w_branch_a=v_w_branch_a, v_w_branch_b=v_w_branch_b, v_w_branch_c=v_w_branch_c, v_w_out=v_w_out, v_final_g=v_final_g)
    weights = {n: given[n] for n in TWIN_WEIGHTS}
    shared = {n: given[n] for n in SHARED_INPUTS}
    per_example = {n: given[n] for n in ['x']}
    grad_fn = _jax.value_and_grad(_loss, argnums=(0, 1))

    def one_microbatch(ex, loss_target):
        ex = dict(ex)
        diff = ex.pop(TWIN_DIFF_INPUT)
        return grad_fn(weights, diff, {**shared, **ex}, loss_target)

    if N_MICROBATCH == 1:
        loss, (grad_w, grad_x) = one_microbatch(per_example, given["loss_target"])
    else:
        def body(carry, xs):
            loss_sum, grad_sum = carry
            l_k, (gw_k, gx_k) = one_microbatch(xs[0], xs[1])
            with _jax.named_scope("update"):
                return (loss_sum + l_k, _jax.tree.map(_jnp.add, grad_sum, gw_k)), gx_k

        init = (_jnp.zeros((), _jnp.float32), _jax.tree.map(_jnp.zeros_like, weights))
        (loss, grad_w), grad_x = _jax.lax.scan(body, init, (per_example, given["loss_target"]))
    with _jax.named_scope("update"):
        delta_w, new_m, new_v = {}, {}, {}
        for n in TWIN_WEIGHTS:
            delta_w[n], new_m[n], new_v[n] = _adamw(weights[n], grad_w[n], given["m_" + n], given["v_" + n])
    return (loss, grad_x, *[grad_w[n] for n in TWIN_WEIGHTS], *[delta_w[n] for n in TWIN_WEIGHTS],
            *[new_m[n] for n in TWIN_WEIGHTS], *[new_v[n] for n in TWIN_WEIGHTS])
```

```python
import functools

import jax
import jax.numpy as jnp
import numpy as np
from jax import lax
from jax.experimental import pallas as pl
from jax.experimental.pallas import tpu as pltpu

F32 = jnp.float32
CDT = jnp.bfloat16

D = 1024
N_IN = 12 * D
DEPTH = 4
NDEV = 8
SHARD_N = N_IN // NDEV
SGU_BLOCK = 128
SGU_GROUPS = 8
CHUNK = 64
POOL_GROUPS = 4
POOL_GDIM = D // POOL_GROUPS
POOL_WINDOWS = (2, 4, 8, 16)
RMS_EPS = 1e-6
LN_EPS = 1e-5
GELU_K0 = float(np.sqrt(2.0 / np.pi))
GELU_K1 = 0.044715

ADAM_LR = 0.001
ADAM_B1 = 0.9
ADAM_B2 = 0.999
ADAM_EPS = 1e-08
ADAM_WD = 0.01
ADAM_STEP = 10

HALO = 32
TS_MIX = 256
TS_PROJ = 1024
VMEM_LIMIT = 56 * 1024 * 1024

C_AU, C_AV, C_AZ, C_BP, C_BZ, C_CH, C_CB, C_CC, C_CZ, C_G0 = range(10)

TN_DIMS = (((0,), (0,)), ((), ()))
NT_DIMS = (((1,), (1,)), ((), ()))


def _cparams(sem):
    return pltpu.CompilerParams(dimension_semantics=sem, vmem_limit_bytes=VMEM_LIMIT)


def _dot(a, b):
    return jnp.dot(a, b, preferred_element_type=F32)


def _gelu(x):
    t = jnp.tanh(GELU_K0 * (x + GELU_K1 * (x * x * x)))
    return 0.5 * x * (1.0 + t), t


def _gelu_grad(x, t):
    return 0.5 * (1.0 + t) + 0.5 * x * (1.0 - t * t) * (GELU_K0 * (1.0 + 3.0 * GELU_K1 * (x * x)))


def _silu(x):
    s = jax.nn.sigmoid(x)
    return x * s, s


def _silu_grad(x, s):
    return s * (1.0 + x * (1.0 - s))


def _rowmean(x):
    return jnp.mean(x, axis=-1, keepdims=True)


def _colsum(x):
    return jnp.sum(x, axis=0, keepdims=True)


def _const_spec(shape):
    nd = len(shape)
    return pl.BlockSpec(shape, lambda *_: (0,) * nd)


def _chunk_spec(ts, chunk, width=1, rev_nt=None):
    if rev_nt is None:
        return pl.BlockSpec((ts, width * D), lambda i: (i, chunk))
    return pl.BlockSpec((ts, width * D), lambda i: (rev_nt - 1 - i, chunk))


def _halo_spec(ts, chunk, rev_nt=None):
    per = ts // HALO
    if rev_nt is None:
        return pl.BlockSpec((HALO, D), lambda i: (jnp.maximum(i * per - 1, 0), chunk))
    return pl.BlockSpec((HALO, D), lambda i: (jnp.maximum((rev_nt - 1 - i) * per - 1, 0), chunk))


def _exchange(arrs, scatter, name):
    n = len(arrs)

    def body(*refs):
        ins, outs = refs[:n], refs[n:2 * n]
        send_sems, recv_sems, local_sems = refs[2 * n:]
        x, y, c = lax.axis_index("x"), lax.axis_index("y"), lax.axis_index("c")
        me = 4 * x + 2 * y + c
        copies = []
        for k in range(1, NDEV):
            px = 1 - x if k & 4 else x
            py = 1 - y if k & 2 else y
            pc = 1 - c if k & 1 else c
            peer = 4 * px + 2 * py + pc
            for a in range(n):
                cp = pltpu.make_async_remote_copy(
                    src_ref=ins[a].at[peer] if scatter else ins[a],
                    dst_ref=outs[a].at[me],
                    send_sem=send_sems.at[k - 1, a],
                    recv_sem=recv_sems.at[k - 1, a],
                    device_id=(px, py, pc),
                    device_id_type=pl.DeviceIdType.MESH,
                )
                cp.start()
                copies.append(cp)
        mine = [
            pltpu.make_async_copy(ins[a].at[me] if scatter else ins[a], outs[a].at[me], local_sems.at[a])
            for a in range(n)
        ]
        for cp in mine:
            cp.start()
        for cp in copies:
            cp.wait()
        for cp in mine:
            cp.wait()

    out_shape = [
        jax.ShapeDtypeStruct(a.shape if scatter else (NDEV,) + a.shape, a.dtype) for a in arrs
    ]
    hbm = pl.BlockSpec(memory_space=pltpu.HBM)
    return pl.pallas_call(
        body,
        name=name,
        out_shape=out_shape,
        in_specs=[hbm] * n,
        out_specs=[hbm] * n,
        scratch_shapes=[
            pltpu.SemaphoreType.DMA((NDEV - 1, n)),
            pltpu.SemaphoreType.DMA((NDEV - 1, n)),
            pltpu.SemaphoreType.DMA((n,)),
        ],
    )(*arrs)


def _fwd_proj(x, g_row, w_gathered, layer):
    s = x.shape[0]
    ts = min(TS_PROJ, s)

    def body(x_ref, g_ref, w_ref, proj_ref, ht_ref, h_scr):
        @pl.when(pl.program_id(1) == 0)
        def _():
            xf = x_ref[...]
            h = xf * lax.rsqrt(_rowmean(xf * xf) + RMS_EPS) * g_ref[...]
            h_scr[...] = h.astype(CDT)
            ht_ref[...] = h.T.astype(CDT)

        proj_ref[...] = _dot(h_scr[...], w_ref[...]).astype(CDT)

    return pl.pallas_call(
        body,
        name="fwd_proj",
        grid=(s // ts, NDEV),
        in_specs=[
            pl.BlockSpec((ts, D), lambda i, j: (i, 0)),
            pl.BlockSpec((1, D), lambda i, j: (0, 0)),
            pl.BlockSpec((None, None, D, SHARD_N), lambda i, j: (j, layer, 0, 0)),
        ],
        out_specs=[
            pl.BlockSpec((ts, SHARD_N), lambda i, j: (i, j)),
            pl.BlockSpec((D, ts), lambda i, j: (0, i)),
        ],
        out_shape=[jax.ShapeDtypeStruct((s, N_IN), CDT), jax.ShapeDtypeStruct((D, s), CDT)],
        scratch_shapes=[pltpu.VMEM((ts, D), CDT)],
        compiler_params=_cparams(("arbitrary", "arbitrary")),
    )(x, g_row, w_gathered)


def _a_forward(au, av, az, lng, lnb, wm_ref, bias_ref, v_scr, mix_scr):
    ts = au.shape[0]
    u, tu = _gelu(au)
    vg, tv = _gelu(av)
    xc = vg - _rowmean(vg)
    rstd = lax.rsqrt(_rowmean(xc * xc) + LN_EPS)
    vhat = xc * rstd
    v_scr[...] = (vhat * lng + lnb).astype(CDT)
    for n in range(ts // SGU_BLOCK):
        rows = slice(n * SGU_BLOCK, (n + 1) * SGU_BLOCK)
        for g in range(SGU_GROUPS):
            cols = slice(g * SGU_BLOCK, (g + 1) * SGU_BLOCK)
            mix_scr[rows, cols] = _dot(wm_ref[g], v_scr[rows, cols]) + bias_ref[:, cols]
    mixed = mix_scr[...]
    sz, sg = _silu(az)
    return dict(u=u, tu=tu, tv=tv, rstd=rstd, vhat=vhat, mixed=mixed, sz=sz, sg=sg, ya=u * mixed * sz)


def _fwd_a(proj, lng, lnb, wm, bias_full, wa):
    s = proj.shape[0]
    ts = min(TS_MIX, s)

    def body(au_ref, av_ref, az_ref, lng_ref, lnb_ref, wm_ref, bias_ref, wa_ref, pa_ref, v_scr, mix_scr):
        f = _a_forward(au_ref[...].astype(F32), av_ref[...].astype(F32), az_ref[...].astype(F32),
                       lng_ref[...], lnb_ref[...], wm_ref, bias_ref, v_scr, mix_scr)
        pa_ref[...] = _dot(f["ya"].astype(CDT), wa_ref[...]).astype(CDT)

    return pl.pallas_call(
        body,
        name="fwd_a",
        grid=(s // ts,),
        in_specs=[
            _chunk_spec(ts, C_AU), _chunk_spec(ts, C_AV), _chunk_spec(ts, C_AZ),
            _const_spec((1, D)), _const_spec((1, D)),
            _const_spec((SGU_GROUPS, SGU_BLOCK, SGU_BLOCK)), _const_spec((SGU_BLOCK, D)),
            _const_spec((D, D)),
        ],
        out_specs=pl.BlockSpec((ts, D), lambda i: (i, 0)),
        out_shape=jax.ShapeDtypeStruct((s, D), CDT),
        scratch_shapes=[pltpu.VMEM((ts, D), CDT), pltpu.VMEM((ts, D), F32)],
        compiler_params=_cparams(("arbitrary",)),
    )(proj, proj, proj, lng, lnb, wm, bias_full, wa)


def _pool_d(pe, sa, sb, tile, ts):
    r = HALO + ts
    g = POOL_GDIM
    sa[8:r, :] = pe[8:r, :] + pe[7:r - 1, :]
    sb[16:r, g:] = sa[16:r, g:] + sa[14:r - 2, g:]
    sa[24:r, 2 * g:] = sb[24:r, 2 * g:] + sb[20:r - 4, 2 * g:]
    sb[32:r, 3 * g:] = sa[32:r, 3 * g:] + sa[24:r - 8, 3 * g:]
    pos1 = tile * ts + lax.broadcasted_iota(jnp.int32, (ts, 1), 0) + 1
    invs = [1.0 / jnp.minimum(pos1, w).astype(F32) for w in POOL_WINDOWS]
    sums = [sa[HALO:r, 0:g], sb[HALO:r, g:2 * g], sa[HALO:r, 2 * g:3 * g], sb[HALO:r, 3 * g:]]
    d = [sums[k] * invs[k] - pe[HALO:r, k * g:(k + 1) * g] for k in range(POOL_GROUPS)]
    return d, invs


def _b_forward(bp_ref, bph_ref, bz, pw_ref, pb, pscale, tile, pe, sa, sb, d_scr, lin_scr):
    ts = bz.shape[0]
    pe[0:HALO, :] = jnp.where(tile > 0, bph_ref[...].astype(F32), 0.0)
    pe[HALO:, :] = bp_ref[...].astype(F32)
    d, invs = _pool_d(pe, sa, sb, tile, ts)
    for k in range(POOL_GROUPS):
        cols = slice(k * POOL_GDIM, (k + 1) * POOL_GDIM)
        d_scr[:, cols] = d[k].astype(CDT)
        lin_scr[:, cols] = _dot(d_scr[:, cols], pw_ref[k])
    lin = lin_scr[...] + pb
    sz, sg = _silu(bz)
    return dict(lin=lin, sz=sz, sg=sg, invs=invs, yb=lin * pscale * sz)


def _fwd_b(proj, pool_w, pool_b, pool_scale, wb):
    s = proj.shape[0]
    ts = min(TS_MIX, s)

    def body(bp_ref, bph_ref, bz_ref, pw_ref, pb_ref, ps_ref, wb_ref, out_ref, pe, sa, sb, d_scr, lin_scr):
        f = _b_forward(bp_ref, bph_ref, bz_ref[...].astype(F32), pw_ref, pb_ref[...], ps_ref[...],
                       pl.program_id(0), pe, sa, sb, d_scr, lin_scr)
        out_ref[...] = _dot(f["yb"].astype(CDT), wb_ref[...]).astype(CDT)

    ext = pltpu.VMEM((HALO + ts, D), F32)
    return pl.pallas_call(
        body,
        name="fwd_b",
        grid=(s // ts,),
        in_specs=[
            _chunk_spec(ts, C_BP), _halo_spec(ts, C_BP), _chunk_spec(ts, C_BZ),
            _const_spec((POOL_GROUPS, POOL_GDIM, POOL_GDIM)), _const_spec((1, D)), _const_spec((1, D)),
            _const_spec((D, D)),
        ],
        out_specs=pl.BlockSpec((ts, D), lambda i: (i, 0)),
        out_shape=jax.ShapeDtypeStruct((s, D), CDT),
        scratch_shapes=[ext, ext, ext, pltpu.VMEM((ts, D), CDT), pltpu.VMEM((ts, D), F32)],
        compiler_params=_cparams(("arbitrary",)),
    )(proj, proj, proj, pool_w, pool_b, pool_scale, wb)


def _c_forward(ch_ref, chh_ref, cb, cc_ref, cch_ref, cz, cw_ref, cbias, tile, qe):
    ts = cb.shape[0]
    r = HALO + ts
    qe[0:HALO, :] = jnp.where(tile > 0, chh_ref[...].astype(F32) * cch_ref[...].astype(F32), 0.0)
    qe[HALO:, :] = cc_ref[...].astype(F32) * ch_ref[...].astype(F32)
    conv = (cw_ref[0:1, :] * qe[HALO - 2:r - 2, :] + cw_ref[1:2, :] * qe[HALO - 1:r - 1, :]
            + cw_ref[2:3, :] * qe[HALO:r, :]) + cbias
    sz, sg = _silu(cz)
    return dict(conv=conv, sz=sz, sg=sg, yc=cb * conv * sz)


def _fwd_c(proj, conv_w, conv_b, wc):
    s = proj.shape[0]
    ts = min(TS_MIX, s)

    def body(ch_ref, chh_ref, cb_ref, cc_ref, cch_ref, cz_ref, cw_ref, cbias_ref, wc_ref, out_ref, qe):
        f = _c_forward(ch_ref, chh_ref, cb_ref[...].astype(F32), cc_ref, cch_ref, cz_ref[...].astype(F32),
                       cw_ref, cbias_ref[...], pl.program_id(0), qe)
        out_ref[...] = _dot(f["yc"].astype(CDT), wc_ref[...]).astype(CDT)

    return pl.pallas_call(
        body,
        name="fwd_c",
        grid=(s // ts,),
        in_specs=[
            _chunk_spec(ts, C_CH), _halo_spec(ts, C_CH), _chunk_spec(ts, C_CB),
            _chunk_spec(ts, C_CC), _halo_spec(ts, C_CC), _chunk_spec(ts, C_CZ),
            _const_spec((3, D)), _const_spec((1, D)), _const_spec((D, D)),
        ],
        out_specs=pl.BlockSpec((ts, D), lambda i: (i, 0)),
        out_shape=jax.ShapeDtypeStruct((s, D), CDT),
        scratch_shapes=[pltpu.VMEM((HALO + ts, D), F32)],
        compiler_params=_cparams(("arbitrary",)),
    )(proj, proj, proj, proj, proj, proj, conv_w, conv_b, wc)


def _merge(pa, pb, pc, logits):
    gates = [jax.nn.sigmoid(logits[:, k * D:(k + 1) * D].astype(F32)) for k in range(3)]
    ps = [pa.astype(F32), pb.astype(F32), pc.astype(F32)]
    merged = gates[0] * ps[0] + gates[1] * ps[1] + gates[2] * ps[2]
    return gates, ps, merged


def _fwd_o(x, pa, pb, pc, proj, wo):
    s = x.shape[0]
    ts = min(TS_MIX, s)

    def body(x_ref, pa_ref, pb_ref, pc_ref, lg_ref, wo_ref, out_ref):
        _, _, merged = _merge(pa_ref[...], pb_ref[...], pc_ref[...], lg_ref[...])
        out_ref[...] = x_ref[...] + _dot(merged.astype(CDT), wo_ref[...])

    tile = pl.BlockSpec((ts, D), lambda i: (i, 0))
    return pl.pallas_call(
        body,
        name="fwd_o",
        grid=(s // ts,),
        in_specs=[tile, tile, tile, tile, _chunk_spec(ts, 3, width=3), _const_spec((D, D))],
        out_specs=tile,
        out_shape=jax.ShapeDtypeStruct((s, D), F32),
        compiler_params=_cparams(("arbitrary",)),
    )(x, pa, pb, pc, proj, wo)


def _loss_head(x, target, fg_row):
    s = x.shape[0]
    ts = min(TS_MIX, s)

    def body(x_ref, t_ref, g_ref, dx_ref, loss_ref, dg_ref):
        @pl.when(pl.program_id(0) == 0)
        def _():
            loss_ref[...] = jnp.zeros_like(loss_ref)
            dg_ref[...] = jnp.zeros_like(dg_ref)

        xf = x_ref[...]
        g = g_ref[...]
        r = lax.rsqrt(_rowmean(xf * xf) + RMS_EPS)
        xhat = xf * r
        err = xhat * g - t_ref[...]
        loss_ref[...] += _colsum(err * err)
        dy = err * (1.0 / D)
        dg_ref[...] += _colsum(dy * xhat)
        dxhat = dy * g
        dx_ref[...] = r * (dxhat - xhat * _rowmean(dxhat * xhat))

    tile = pl.BlockSpec((ts, D), lambda i: (i, 0))
    row = _const_spec((1, D))
    return pl.pallas_call(
        body,
        name="loss_head",
        grid=(s // ts,),
        in_specs=[tile, tile, row],
        out_specs=[tile, row, row],
        out_shape=[jax.ShapeDtypeStruct((s, D), F32), jax.ShapeDtypeStruct((1, D), F32),
                   jax.ShapeDtypeStruct((1, D), F32)],
        compiler_params=_cparams(("arbitrary",)),
    )(x, target, fg_row)


def _zero_first(refs):
    @pl.when(pl.program_id(0) == 0)
    def _():
        for r in refs:
            r[...] = jnp.zeros_like(r)


def _bwd_o(dx, pa, pb, pc, proj, wo_t):
    s = dx.shape[0]
    ts = min(TS_MIX, s)

    def body(dx_ref, pa_ref, pb_ref, pc_ref, lg_ref, wot_ref, dp_ref, dg_ref, dwo_ref):
        _zero_first([dwo_ref])
        gates, ps, merged = _merge(pa_ref[...], pb_ref[...], pc_ref[...], lg_ref[...])
        dxb = dx_ref[...].astype(CDT)
        dwo_ref[...] += lax.dot_general(merged.astype(CDT), dxb, TN_DIMS, preferred_element_type=F32)
        dmerged = _dot(dxb, wot_ref[...])
        for k in range(3):
            cols = slice(k * D, (k + 1) * D)
            dpk = dmerged * gates[k]
            dp_ref[:, cols] = dpk.astype(CDT)
            dg_ref[:, cols] = (dpk * ps[k] * (1.0 - gates[k])).astype(CDT)

    tile = pl.BlockSpec((ts, D), lambda i: (i, 0))
    wide = pl.BlockSpec((ts, 3 * D), lambda i: (i, 0))
    return pl.pallas_call(
        body,
        name="bwd_o",
        grid=(s // ts,),
        in_specs=[tile, tile, tile, tile, _chunk_spec(ts, 3, width=3), _const_spec((D, D))],
        out_specs=[wide, wide, _const_spec((D, D))],
        out_shape=[jax.ShapeDtypeStruct((s, 3 * D), CDT), jax.ShapeDtypeStruct((s, 3 * D), CDT),
                   jax.ShapeDtypeStruct((D, D), F32)],
        compiler_params=_cparams(("arbitrary",)),
    )(dx, pa, pb, pc, proj, wo_t)


def _bwd_a(proj, dp, lng, lnb, wm, wm_t, bias_full, wa_t):
    s = proj.shape[0]
    ts = min(TS_MIX, s)
    nt = s // ts

    def body(au_ref, av_ref, az_ref, dpa_ref, lng_ref, lnb_ref, wm_ref, wmt_ref, bias_ref, wat_ref,
             da_ref, dwa_ref, dwm_ref, dbs_ref, dlng_ref, dlnb_ref,
             v_scr, mix_scr, dm_scr, dmf_scr, dv_scr, bsacc):
        _zero_first([dwa_ref, dwm_ref, dlng_ref, dlnb_ref, bsacc])
        au = au_ref[...].astype(F32)
        av = av_ref[...].astype(F32)
        az = az_ref[...].astype(F32)
        lng = lng_ref[...]
        f = _a_forward(au, av, az, lng, lnb_ref[...], wm_ref, bias_ref, v_scr, mix_scr)
        dpa = dpa_ref[...]
        dwa_ref[...] += lax.dot_general(f["ya"].astype(CDT), dpa, TN_DIMS, preferred_element_type=F32)
        dya = _dot(dpa, wat_ref[...])
        t1 = dya * f["mixed"]
        da_ref[:, 0:D] = (t1 * f["sz"] * _gelu_grad(au, f["tu"])).astype(CDT)
        da_ref[:, 2 * D:3 * D] = (t1 * f["u"] * _silu_grad(az, f["sg"])).astype(CDT)
        dmix = dya * f["u"] * f["sz"]
        dmf_scr[...] = dmix
        dm_scr[...] = dmix.astype(CDT)
        for n in range(ts // SGU_BLOCK):
            rows = slice(n * SGU_BLOCK, (n + 1) * SGU_BLOCK)
            for g in range(SGU_GROUPS):
                cols = slice(g * SGU_BLOCK, (g + 1) * SGU_BLOCK)
                dmb = dm_scr[rows, cols]
                dwm_ref[g] += lax.dot_general(dmb, v_scr[rows, cols], NT_DIMS, preferred_element_type=F32)
                dv_scr[rows, cols] = _dot(wmt_ref[g], dmb)
                bsacc[g] += dmf_scr[rows, cols]
        dvln = dv_scr[...]
        vhat = f["vhat"]
        dlng_ref[...] += _colsum(dvln * vhat)
        dlnb_ref[...] += _colsum(dvln)
        dvhat = dvln * lng
        dvg = f["rstd"] * (dvhat - _rowmean(dvhat) - vhat * _rowmean(dvhat * vhat))
        da_ref[:, D:2 * D] = (dvg * _gelu_grad(av, f["tv"])).astype(CDT)

        @pl.when(pl.program_id(0) == nt - 1)
        def _():
            ones = jnp.ones((8, SGU_BLOCK), F32)
            for g in range(SGU_GROUPS):
                red = lax.dot_general(ones, bsacc[g], NT_DIMS, preferred_element_type=F32,
                                      precision=lax.Precision.HIGHEST)
                dbs_ref[g:g + 1, :] = red[0:1, :]

    gshape = (SGU_GROUPS, SGU_BLOCK, SGU_BLOCK)
    return pl.pallas_call(
        body,
        name="bwd_a",
        grid=(nt,),
        in_specs=[
            _chunk_spec(ts, C_AU), _chunk_spec(ts, C_AV), _chunk_spec(ts, C_AZ), _chunk_spec(ts, 0),
            _const_spec((1, D)), _const_spec((1, D)), _const_spec(gshape), _const_spec(gshape),
            _const_spec((SGU_BLOCK, D)), _const_spec((D, D)),
        ],
        out_specs=[
            pl.BlockSpec((ts, 3 * D), lambda i: (i, 0)), _const_spec((D, D)), _const_spec(gshape),
            _const_spec((SGU_GROUPS, SGU_BLOCK)), _const_spec((1, D)), _const_spec((1, D)),
        ],
        out_shape=[
            jax.ShapeDtypeStruct((s, 3 * D), CDT), jax.ShapeDtypeStruct((D, D), F32),
            jax.ShapeDtypeStruct(gshape, F32), jax.ShapeDtypeStruct((SGU_GROUPS, SGU_BLOCK), F32),
            jax.ShapeDtypeStruct((1, D), F32), jax.ShapeDtypeStruct((1, D), F32),
        ],
        scratch_shapes=[
            pltpu.VMEM((ts, D), CDT), pltpu.VMEM((ts, D), F32), pltpu.VMEM((ts, D), CDT),
            pltpu.VMEM((ts, D), F32), pltpu.VMEM((ts, D), F32), pltpu.VMEM(gshape, F32),
        ],
        compiler_params=_cparams(("arbitrary",)),
    )(proj, proj, proj, dp, lng, lnb, wm, wm_t, bias_full, wa_t)


def _bwd_b(proj, dp, pool_w, pool_w_t, pool_b, pool_scale, wb_t):
    s = proj.shape[0]
    ts = min(TS_MIX, s)
    nt = s // ts
    g = POOL_GDIM

    def body(bp_ref, bph_ref, bz_ref, dpb_ref, pw_ref, pwt_ref, pb_ref, ps_ref, wbt_ref,
             db_ref, dwb_ref, dpw_ref, dpb_out_ref, dps_ref,
             pe, sa, sb, d_scr, lin_scr, ee, dl_scr, carry):
        _zero_first([dwb_ref, dpw_ref, dpb_out_ref, dps_ref, carry])
        tile = nt - 1 - pl.program_id(0)
        bz = bz_ref[...].astype(F32)
        pscale = ps_ref[...]
        f = _b_forward(bp_ref, bph_ref, bz, pw_ref, pb_ref[...], pscale, tile, pe, sa, sb, d_scr, lin_scr)
        dpb = dpb_ref[...]
        dwb_ref[...] += lax.dot_general(f["yb"].astype(CDT), dpb, TN_DIMS, preferred_element_type=F32)
        dyb = _dot(dpb, wbt_ref[...])
        t1 = dyb * f["lin"]
        dps_ref[...] += _colsum(t1 * f["sz"])
        db_ref[:, D:2 * D] = (t1 * pscale * _silu_grad(bz, f["sg"])).astype(CDT)
        dlin = dyb * pscale * f["sz"]
        dpb_out_ref[...] += _colsum(dlin)
        dl_scr[...] = dlin.astype(CDT)
        r = HALO + ts
        ee[ts:r, :] = carry[...]
        for k in range(POOL_GROUPS):
            cols = slice(k * g, (k + 1) * g)
            dlk = dl_scr[:, cols]
            dpw_ref[k] += lax.dot_general(d_scr[:, cols], dlk, TN_DIMS, preferred_element_type=F32)
            dd = _dot(dlk, pwt_ref[k])
            lin_scr[:, cols] = dd
            ee[0:ts, cols] = dd * f["invs"][k]
        carry[...] = ee[0:HALO, :]
        sa[0:ts + 24, :] = ee[0:ts + 24, :] + ee[1:ts + 25, :]
        sb[0:ts + 16, g:] = sa[0:ts + 16, g:] + sa[2:ts + 18, g:]
        sa[0:ts + 8, 2 * g:] = sb[0:ts + 8, 2 * g:] + sb[4:ts + 12, 2 * g:]
        sb[0:ts, 3 * g:] = sa[0:ts, 3 * g:] + sa[8:ts + 8, 3 * g:]
        sums = [sa[0:ts, 0:g], sb[0:ts, g:2 * g], sa[0:ts, 2 * g:3 * g], sb[0:ts, 3 * g:]]
        for k in range(POOL_GROUPS):
            cols = slice(k * g, (k + 1) * g)
            db_ref[:, cols] = (sums[k] - lin_scr[:, cols]).astype(CDT)

    ext = pltpu.VMEM((HALO + ts, D), F32)
    wshape = (POOL_GROUPS, g, g)
    return pl.pallas_call(
        body,
        name="bwd_b",
        grid=(nt,),
        in_specs=[
            _chunk_spec(ts, C_BP, rev_nt=nt), _halo_spec(ts, C_BP, rev_nt=nt), _chunk_spec(ts, C_BZ, rev_nt=nt),
            _chunk_spec(ts, 1, rev_nt=nt),
            _const_spec(wshape), _const_spec(wshape), _const_spec((1, D)), _const_spec((1, D)), _const_spec((D, D)),
        ],
        out_specs=[
            pl.BlockSpec((ts, 2 * D), lambda i: (nt - 1 - i, 0)), _const_spec((D, D)), _const_spec(wshape),
            _const_spec((1, D)), _const_spec((1, D)),
        ],
        out_shape=[
            jax.ShapeDtypeStruct((s, 2 * D), CDT), jax.ShapeDtypeStruct((D, D), F32),
            jax.ShapeDtypeStruct(wshape, F32), jax.ShapeDtypeStruct((1, D), F32), jax.ShapeDtypeStruct((1, D), F32),
        ],
        scratch_shapes=[ext, ext, ext, pltpu.VMEM((ts, D), CDT), pltpu.VMEM((ts, D), F32), ext,
                        pltpu.VMEM((ts, D), CDT), pltpu.VMEM((HALO, D), F32)],
        compiler_params=_cparams(("arbitrary",)),
    )(proj, proj, proj, dp, pool_w, pool_w_t, pool_b, pool_scale, wb_t)


def _bwd_c(proj, dp, conv_w, conv_b, wc_t):
    s = proj.shape[0]
    ts = min(TS_MIX, s)
    nt = s // ts

    def body(ch_ref, chh_ref, cb_ref, cc_ref, cch_ref, cz_ref, dpc_ref, cw_ref, cbias_ref, wct_ref,
             dc_ref, dwc_ref, dcw_ref, dcb_ref, qe, de, carry):
        _zero_first([dwc_ref, dcw_ref, dcb_ref, carry])
        tile = nt - 1 - pl.program_id(0)
        cb = cb_ref[...].astype(F32)
        cz = cz_ref[...].astype(F32)
        f = _c_forward(ch_ref, chh_ref, cb, cc_ref, cch_ref, cz, cw_ref, cbias_ref[...], tile, qe)
        dpc = dpc_ref[...]
        dwc_ref[...] += lax.dot_general(f["yc"].astype(CDT), dpc, TN_DIMS, preferred_element_type=F32)
        dyc = _dot(dpc, wct_ref[...])
        t1 = dyc * f["conv"]
        dc_ref[:, D:2 * D] = (t1 * f["sz"]).astype(CDT)
        dc_ref[:, 3 * D:4 * D] = (t1 * cb * _silu_grad(cz, f["sg"])).astype(CDT)
        dconv = dyc * cb * f["sz"]
        r = HALO + ts
        dcb_ref[...] += _colsum(dconv)
        dcw_ref[0:1, :] += _colsum(dconv * qe[HALO - 2:r - 2, :])
        dcw_ref[1:2, :] += _colsum(dconv * qe[HALO - 1:r - 1, :])
        dcw_ref[2:3, :] += _colsum(dconv * qe[HALO:r, :])
        de[0:ts, :] = dconv
        de[ts:ts + 8, :] = carry[...]
        carry[...] = de[0:8, :]
        dq = cw_ref[2:3, :] * dconv + cw_ref[1:2, :] * de[1:ts + 1, :] + cw_ref[0:1, :] * de[2:ts + 2, :]
        dc_ref[:, 0:D] = (dq * cc_ref[...].astype(F32)).astype(CDT)
        dc_ref[:, 2 * D:3 * D] = (dq * ch_ref[...].astype(F32)).astype(CDT)

    return pl.pallas_call(
        body,
        name="bwd_c",
        grid=(nt,),
        in_specs=[
            _chunk_spec(ts, C_CH, rev_nt=nt), _halo_spec(ts, C_CH, rev_nt=nt), _chunk_spec(ts, C_CB, rev_nt=nt),
            _chunk_spec(ts, C_CC, rev_nt=nt), _halo_spec(ts, C_CC, rev_nt=nt), _chunk_spec(ts, C_CZ, rev_nt=nt),
            _chunk_spec(ts, 2, rev_nt=nt),
            _const_spec((3, D)), _const_spec((1, D)), _const_spec((D, D)),
        ],
        out_specs=[
            pl.BlockSpec((ts, 4 * D), lambda i: (nt - 1 - i, 0)), _const_spec((D, D)), _const_spec((8, D)),
            _const_spec((1, D)),
        ],
        out_shape=[
            jax.ShapeDtypeStruct((s, 4 * D), CDT), jax.ShapeDtypeStruct((D, D), F32),
            jax.ShapeDtypeStruct((8, D), F32), jax.ShapeDtypeStruct((1, D), F32),
        ],
        scratch_shapes=[pltpu.VMEM((HALO + ts, D), F32), pltpu.VMEM((ts + 8, D), F32), pltpu.VMEM((8, D), F32)],
        compiler_params=_cparams(("arbitrary",)),
    )(proj, proj, proj, proj, proj, proj, dp, conv_w, conv_b, wc_t)


DPROJ_STARTS = (0, 3, 5, 9)
DPROJ_WIDTHS = (3, 2, 4, 3)


def _bwd_proj_dx(d_arrs, w_in_t, layer, x, g_row, dx_out):
    s = x.shape[0]
    ts = min(TS_PROJ, s)
    nk = N_IN // D

    def body(da_ref, db_ref, dc_ref, dg_ref, wt_ref, x_ref, g_ref, dxo_ref, dxi_ref, dng_ref, acc):
        i, k = pl.program_id(0), pl.program_id(1)

        @pl.when((i == 0) & (k == 0))
        def _():
            dng_ref[...] = jnp.zeros_like(dng_ref)

        @pl.when(k == 0)
        def _():
            acc[...] = jnp.zeros_like(acc)

        for ref, start, width in zip((da_ref, db_ref, dc_ref, dg_ref), DPROJ_STARTS, DPROJ_WIDTHS):
            @pl.when((k >= start) & (k < start + width))
            def _(ref=ref):
                acc[...] += _dot(ref[...], wt_ref[...])

        @pl.when(k == nk - 1)
        def _():
            xf = x_ref[...]
            g = g_ref[...]
            dh = acc[...]
            r = lax.rsqrt(_rowmean(xf * xf) + RMS_EPS)
            xhat = xf * r
            dng_ref[...] += _colsum(dh * xhat)
            dxhat = dh * g
            dxi_ref[...] = dxo_ref[...] + r * (dxhat - xhat * _rowmean(dxhat * xhat))

    def dspec(start, width):
        return pl.BlockSpec((ts, D), lambda i, k: (i, jnp.clip(k - start, 0, width - 1)))

    tile = pl.BlockSpec((ts, D), lambda i, k: (i, 0))
    row = pl.BlockSpec((1, D), lambda i, k: (0, 0))
    return pl.pallas_call(
        body,
        name="bwd_proj_dx",
        grid=(s // ts, nk),
        in_specs=[dspec(st, w) for st, w in zip(DPROJ_STARTS, DPROJ_WIDTHS)] + [
            pl.BlockSpec((None, D, D), lambda i, k: (layer, k, 0)), tile, row, tile,
        ],
        out_specs=[tile, row],
        out_shape=[jax.ShapeDtypeStruct((s, D), F32), jax.ShapeDtypeStruct((1, D), F32)],
        scratch_shapes=[pltpu.VMEM((ts, D), F32)],
        compiler_params=_cparams(("arbitrary", "arbitrary")),
    )(*d_arrs, w_in_t, x, g_row, dx_out)


def _bwd_proj_dw(h_t, d_arrs):
    s = h_t.shape[1]
    tk = min(2048, s)
    tn = 512
    per_shard = SHARD_N // tn
    starts = [st * D // tn for st in DPROJ_STARTS]
    widths = [w * D // tn for w in DPROJ_WIDTHS]
    nk = s // tk

    def body(ht_ref, da_ref, db_ref, dc_ref, dg_ref, out_ref, acc):
        n, k = pl.program_id(0), pl.program_id(1)

        @pl.when(k == 0)
        def _():
            acc[...] = jnp.zeros_like(acc)

        for ref, start, width in zip((da_ref, db_ref, dc_ref, dg_ref), starts, widths):
            @pl.when((n >= start) & (n < start + width))
            def _(ref=ref):
                acc[...] += _dot(ht_ref[...], ref[...])

        @pl.when(k == nk - 1)
        def _():
            out_ref[...] = acc[...].astype(CDT)

    def dspec(start, width):
        def index(n, k):
            active = (n >= start) & (n < start + width)
            return (jnp.where(active, k, 0), jnp.clip(n - start, 0, width - 1))
        return pl.BlockSpec((tk, tn), index)

    return pl.pallas_call(
        body,
        name="bwd_proj_dw",
        grid=(N_IN // tn, nk),
        in_specs=[pl.BlockSpec((D, tk), lambda n, k: (0, k))] + [dspec(st, w) for st, w in zip(starts, widths)],
        out_specs=pl.BlockSpec((None, D, tn), lambda n, k: (n // per_shard, 0, n % per_shard)),
        out_shape=jax.ShapeDtypeStruct((NDEV, D, SHARD_N), CDT),
        scratch_shapes=[pltpu.VMEM((D, tn), F32)],
        compiler_params=_cparams(("arbitrary", "arbitrary")),
    )(h_t, *d_arrs)


def _adamw(recvs, w, m, v, tr, name):
    nq = len(recvs)
    _, rows, c = recvs[0].shape
    assert rows % tr == 0 and w.shape == (nq * rows, c)
    steps = rows // tr

    def body(*refs):
        rrefs = refs[:nq]
        w_ref, m_ref, v_ref, g_out, d_out, m_out, v_out = refs[nq:]
        q = pl.program_id(0)
        for qi in range(nq):
            @pl.when(q == qi)
            def _(r=rrefs[qi]):
                g = r[0].astype(F32)
                for j in range(1, NDEV):
                    g = g + r[j].astype(F32)
                wv = w_ref[...]
                m2 = ADAM_B1 * m_ref[...] + (1.0 - ADAM_B1) * g
                v2 = ADAM_B2 * v_ref[...] + (1.0 - ADAM_B2) * (g * g)
                m_hat = m2 / (1.0 - ADAM_B1 ** ADAM_STEP)
                v_hat = v2 / (1.0 - ADAM_B2 ** ADAM_STEP)
                g_out[...] = g
                d_out[...] = -ADAM_LR * (m_hat / (jnp.sqrt(v_hat) + ADAM_EPS) + ADAM_WD * wv)
                m_out[...] = m2
                v_out[...] = v2

    def rspec(qi):
        return pl.BlockSpec((NDEV, tr, c), lambda q, t: (0, jnp.where(q == qi, t, 0), 0))

    tile = pl.BlockSpec((tr, c), lambda q, t: (q * steps + t, 0))
    shp = jax.ShapeDtypeStruct(w.shape, F32)
    return pl.pallas_call(
        body,
        name=name,
        grid=(nq, steps),
        in_specs=[rspec(qi) for qi in range(nq)] + [tile, tile, tile],
        out_specs=[tile] * 4,
        out_shape=[shp] * 4,
        compiler_params=_cparams(("arbitrary", "arbitrary")),
    )(*recvs, w, m, v)


SMALL_ROWS = 264
REP_NAMES = ("norm_g", "sgu_ln_g", "sgu_ln_b", "sgu_b", "pool_b", "pool_scale", "conv_b")


def _pack_small(pool_w_l, conv_w_l):
    flat = jnp.concatenate([pool_w_l.reshape(-1), conv_w_l.reshape(-1)])
    return jnp.pad(flat, (0, SMALL_ROWS * 128 - flat.shape[0])).reshape(SMALL_ROWS, 128)


def _unpack_small(packed):
    flat = packed.reshape(DEPTH, SMALL_ROWS * 128)
    n_pool = POOL_GROUPS * (POOL_GDIM // NDEV) * POOL_GDIM
    pool = flat[:, :n_pool].reshape(DEPTH, POOL_GROUPS, POOL_GDIM // NDEV, POOL_GDIM)
    conv = flat[:, n_pool:n_pool + 3 * (D // NDEV)].reshape(DEPTH, 3, D // NDEV)
    return pool, conv


REP_LAYER_ROWS = 7 * 8 + SGU_GROUPS * SGU_BLOCK
REP_ROWS = 4352
REP_TILE = 256


def _pack_rep(per_layer, sgu_w, final_g):
    parts = []
    for l in range(DEPTH):
        for nme in REP_NAMES:
            parts.append(per_layer[nme][l].reshape(8, 128))
        parts.append(sgu_w[l].reshape(SGU_GROUPS * SGU_BLOCK, SGU_BLOCK))
    parts.append(final_g.reshape(8, 128))
    packed = jnp.concatenate(parts, axis=0)
    return jnp.pad(packed, ((0, REP_ROWS - packed.shape[0]), (0, 0)))


def _unpack_rep(packed):
    per_layer = {nme: [] for nme in REP_NAMES}
    sgu_w = []
    for l in range(DEPTH):
        base = l * REP_LAYER_ROWS
        for q, nme in enumerate(REP_NAMES):
            per_layer[nme].append(packed[base + 8 * q:base + 8 * q + 8])
        sgu_w.append(packed[base + 56:base + REP_LAYER_ROWS].reshape(SGU_GROUPS, SGU_BLOCK, SGU_BLOCK))
    final_g = packed[DEPTH * REP_LAYER_ROWS:DEPTH * REP_LAYER_ROWS + 8].reshape(D)
    out = {nme: jnp.stack(vs).reshape(DEPTH, D) for nme, vs in per_layer.items()}
    out["sgu_b"] = out["sgu_b"].reshape(DEPTH, SGU_GROUPS, SGU_BLOCK)
    return out, jnp.stack(sgu_w), final_g


def kernel(x, norm_g, w_in, sgu_ln_g, sgu_ln_b, sgu_w, sgu_b, pool_w, pool_b, pool_scale, conv_w, conv_b, w_branch_a, w_branch_b, w_branch_c, w_out, final_g, loss_target, m_norm_g, m_w_in, m_sgu_ln_g, m_sgu_ln_b, m_sgu_w, m_sgu_b, m_pool_w, m_pool_b, m_pool_scale, m_conv_w, m_conv_b, m_w_branch_a, m_w_branch_b, m_w_branch_c, m_w_out, m_final_g, v_norm_g, v_w_in, v_sgu_ln_g, v_sgu_ln_b, v_sgu_w, v_sgu_b, v_pool_w, v_pool_b, v_pool_scale, v_conv_w, v_conv_b, v_w_branch_a, v_w_branch_b, v_w_branch_c, v_w_out, v_final_g):
    assert x.shape[0] == 1 and x.shape[2] == D
    s = x.shape[1]
    x0 = x.reshape(s, D)
    target = loss_target.reshape(s, D)

    w_br = jnp.stack([w_branch_a, w_branch_b, w_branch_c, w_out], axis=1)
    g_in, g_br, g_pool, g_conv = _exchange(
        [w_in.astype(CDT), w_br.astype(CDT), pool_w.astype(CDT), conv_w], False, "gather_weights")
    w_in_t = jnp.transpose(g_in, (1, 0, 3, 2)).reshape(DEPTH, N_IN, D)
    w_mats = jnp.transpose(g_br, (1, 2, 0, 3, 4)).reshape(DEPTH, 4, D, D)
    w_mats_t = jnp.transpose(w_mats, (0, 1, 3, 2))
    pool_full = jnp.transpose(g_pool, (1, 2, 0, 3, 4)).reshape(DEPTH, POOL_GROUPS, POOL_GDIM, POOL_GDIM)
    pool_full_t = jnp.transpose(pool_full, (0, 1, 3, 2))
    conv_full = jnp.transpose(g_conv, (1, 2, 0, 3)).reshape(DEPTH, 3, D)
    pos = np.arange(SGU_BLOCK) // CHUNK
    mask = jnp.asarray(pos[None, :] <= pos[:, None])
    wm_all = jnp.where(mask[None, None], sgu_w, 0.0).astype(CDT)
    wm_all_t = jnp.transpose(wm_all, (0, 1, 3, 2))
    bias_all = jnp.repeat(jnp.transpose(sgu_b, (0, 2, 1)), SGU_BLOCK, axis=2)

    def row(a, l):
        return a[l].reshape(1, D)

    saved = []
    xl = x0
    for l in range(DEPTH):
        proj, h_t = _fwd_proj(xl, row(norm_g, l), g_in, l)
        pa = _fwd_a(proj, row(sgu_ln_g, l), row(sgu_ln_b, l), wm_all[l], bias_all[l], w_mats[l, 0])
        pb = _fwd_b(proj, pool_full[l], row(pool_b, l), row(pool_scale, l), w_mats[l, 1])
        pc = _fwd_c(proj, conv_full[l], row(conv_b, l), w_mats[l, 2])
        saved.append((xl, proj, h_t, pa, pb, pc))
        xl = _fwd_o(xl, pa, pb, pc, proj, w_mats[l, 3])

    dx, loss_cols, dfinal_g = _loss_head(xl, target, final_g.reshape(1, D))
    loss = lax.psum(0.5 / D * jnp.sum(loss_cols), ("x", "y", "c"))

    rep_grads = {nme: [None] * DEPTH for nme in REP_NAMES}
    sgu_w_grads = [None] * DEPTH
    recv_in, recv_br, recv_small = [None] * DEPTH, [None] * DEPTH, [None] * DEPTH
    for l in reversed(range(DEPTH)):
        xl, proj, h_t, pa, pb, pc = saved[l]
        dp, d_g, dwo = _bwd_o(dx, pa, pb, pc, proj, w_mats_t[l, 3])
        d_a, dwa, dwm, dbs, dlng, dlnb = _bwd_a(
            proj, dp, row(sgu_ln_g, l), row(sgu_ln_b, l), wm_all[l], wm_all_t[l], bias_all[l], w_mats_t[l, 0])
        d_b, dwb, dpw, dpb, dps = _bwd_b(
            proj, dp, pool_full[l], pool_full_t[l], row(pool_b, l), row(pool_scale, l), w_mats_t[l, 1])
        d_c, dwc, dcw, dcb = _bwd_c(proj, dp, conv_full[l], row(conv_b, l), w_mats_t[l, 2])
        d_arrs = (d_a, d_b, d_c, d_g)
        dx, dng = _bwd_proj_dx(d_arrs, w_in_t, l, xl, row(norm_g, l), dx)
        dw_in = _bwd_proj_dw(h_t, d_arrs)

        dw_br = jnp.stack([dwa, dwb, dwc, dwo]).astype(CDT).reshape(4, NDEV, D // NDEV, D).transpose(1, 0, 2, 3)
        dpw_j = dpw.reshape(POOL_GROUPS, NDEV, POOL_GDIM // NDEV, POOL_GDIM).transpose(1, 0, 2, 3)
        dcw_j = dcw[0:3].reshape(3, NDEV, D // NDEV).transpose(1, 0, 2)
        small = jnp.stack([_pack_small(dpw_j[j], dcw_j[j]) for j in range(NDEV)])
        recv_in[l], recv_br[l], recv_small[l] = _exchange([dw_in, dw_br, small], True, "scatter_grads")

        rep_grads["norm_g"][l] = dng
        rep_grads["sgu_ln_g"][l] = dlng
        rep_grads["sgu_ln_b"][l] = dlnb
        rep_grads["sgu_b"][l] = dbs
        rep_grads["pool_b"][l] = dpb
        rep_grads["pool_scale"][l] = dps
        rep_grads["conv_b"][l] = dcb
        sgu_w_grads[l] = jnp.where(mask[None], dwm, 0.0)

    rep_packed = _pack_rep(rep_grads, sgu_w_grads, dfinal_g)
    (rep_recv,) = _exchange([rep_packed], False, "gather_small_grads")
    rep_w = dict(norm_g=norm_g, sgu_ln_g=sgu_ln_g, sgu_ln_b=sgu_ln_b, sgu_b=sgu_b, pool_b=pool_b,
                 pool_scale=pool_scale, conv_b=conv_b)
    rep_m = dict(norm_g=m_norm_g, sgu_ln_g=m_sgu_ln_g, sgu_ln_b=m_sgu_ln_b, sgu_b=m_sgu_b, pool_b=m_pool_b,
                 pool_scale=m_pool_scale, conv_b=m_conv_b)
    rep_v = dict(norm_g=v_norm_g, sgu_ln_g=v_sgu_ln_g, sgu_ln_b=v_sgu_ln_b, sgu_b=v_sgu_b, pool_b=v_pool_b,
                 pool_scale=v_pool_scale, conv_b=v_conv_b)
    rep_out = _adamw(
        [rep_recv], _pack_rep(rep_w, sgu_w, final_g), _pack_rep(rep_m, m_sgu_w, m_final_g),
        _pack_rep(rep_v, v_sgu_w, v_final_g), REP_TILE, "adamw_replicated")
    rep_res = [_unpack_rep(o) for o in rep_out]

    in_out = _adamw(recv_in, w_in.reshape(DEPTH * D, SHARD_N), m_w_in.reshape(DEPTH * D, SHARD_N),
                    v_w_in.reshape(DEPTH * D, SHARD_N), 64, "adamw_w_in")
    in_res = [o.reshape(DEPTH, D, SHARD_N) for o in in_out]

    def stack_br(a, b, c, o):
        return jnp.stack([a, b, c, o], axis=1).reshape(DEPTH * 4 * (D // NDEV), D)

    br_out = _adamw(
        [r.reshape(NDEV, 4 * (D // NDEV), D) for r in recv_br],
        stack_br(w_branch_a, w_branch_b, w_branch_c, w_out),
        stack_br(m_w_branch_a, m_w_branch_b, m_w_branch_c, m_w_out),
        stack_br(v_w_branch_a, v_w_branch_b, v_w_branch_c, v_w_out), 128, "adamw_w_branch")
    br_res = [o.reshape(DEPTH, 4, D // NDEV, D) for o in br_out]

    def stack_small(pw, cw):
        return jnp.concatenate([_pack_small(pw[l], cw[l]) for l in range(DEPTH)], axis=0)

    small_out = _adamw(recv_small, stack_small(pool_w, conv_w), stack_small(m_pool_w, m_conv_w),
                       stack_small(v_pool_w, v_conv_w), SMALL_ROWS, "adamw_small")
    small_res = [_unpack_small(o) for o in small_out]

    def leaves(kind):
        rep, sgu_w_k, final_k = rep_res[kind]
        br = br_res[kind]
        pool_k, conv_k = small_res[kind]
        return [rep["norm_g"], in_res[kind], rep["sgu_ln_g"], rep["sgu_ln_b"], sgu_w_k, rep["sgu_b"], pool_k,
                rep["pool_b"], rep["pool_scale"], conv_k, rep["conv_b"], br[:, 0], br[:, 1], br[:, 2], br[:, 3],
                final_k]

    return (loss, dx.reshape(1, s, D), *leaves(0), *leaves(1), *leaves(2), *leaves(3))
```

```python
import functools

import jax
import jax.numpy as jnp
import numpy as np
from jax import lax
from jax.experimental import pallas as pl
from jax.experimental.pallas import tpu as pltpu

F32 = jnp.float32
CDT = jnp.bfloat16

D = 1024
N_IN = 12 * D
DEPTH = 4
NDEV = 8
SHARD_N = N_IN // NDEV
SGU_BLOCK = 128
SGU_GROUPS = 8
CHUNK = 64
POOL_GROUPS = 4
POOL_GDIM = D // POOL_GROUPS
POOL_WINDOWS = (2, 4, 8, 16)
RMS_EPS = 1e-6
LN_EPS = 1e-5
GELU_K0 = float(np.sqrt(2.0 / np.pi))
GELU_K1 = 0.044715

ADAM_LR = 0.001
ADAM_B1 = 0.9
ADAM_B2 = 0.999
ADAM_EPS = 1e-08
ADAM_WD = 0.01
ADAM_STEP = 10

HALO = 32
TS_MIX = 256
TS_PROJ = 1024
VMEM_LIMIT = 56 * 1024 * 1024

C_AU, C_AV, C_AZ, C_BP, C_BZ, C_CH, C_CB, C_CC, C_CZ, C_G0 = range(10)

TN_DIMS = (((0,), (0,)), ((), ()))
NT_DIMS = (((1,), (1,)), ((), ()))


def _cparams(sem):
    return pltpu.CompilerParams(dimension_semantics=sem, vmem_limit_bytes=VMEM_LIMIT)


def _dot(a, b):
    return jnp.dot(a, b, preferred_element_type=F32)


def _gelu(x):
    t = jnp.tanh(GELU_K0 * (x + GELU_K1 * (x * x * x)))
    return 0.5 * x * (1.0 + t), t


def _gelu_grad(x, t):
    return 0.5 * (1.0 + t) + 0.5 * x * (1.0 - t * t) * (GELU_K0 * (1.0 + 3.0 * GELU_K1 * (x * x)))


def _silu(x):
    s = jax.nn.sigmoid(x)
    return x * s, s


def _silu_grad(x, s):
    return s * (1.0 + x * (1.0 - s))


def _rowmean(x):
    return jnp.mean(x, axis=-1, keepdims=True)


def _colsum(x):
    return jnp.sum(x, axis=0, keepdims=True)


def _const_spec(shape):
    nd = len(shape)
    return pl.BlockSpec(shape, lambda *_: (0,) * nd)


def _chunk_spec(ts, chunk, width=1, rev_nt=None):
    if rev_nt is None:
        return pl.BlockSpec((ts, width * D), lambda i: (i, chunk))
    return pl.BlockSpec((ts, width * D), lambda i: (rev_nt - 1 - i, chunk))


def _halo_spec(ts, chunk, rev_nt=None):
    per = ts // HALO
    if rev_nt is None:
        return pl.BlockSpec((HALO, D), lambda i: (jnp.maximum(i * per - 1, 0), chunk))
    return pl.BlockSpec((HALO, D), lambda i: (jnp.maximum((rev_nt - 1 - i) * per - 1, 0), chunk))


def _exchange_copies(scat_ins, scat_outs, gath_ins, gath_outs, send_sems, recv_sems, local_sems):
    x, y, c = lax.axis_index("x"), lax.axis_index("y"), lax.axis_index("c")
    me = 4 * x + 2 * y + c
    arrs = [(i, o, True) for i, o in zip(scat_ins, scat_outs)] + [(i, o, False) for i, o in zip(gath_ins, gath_outs)]
    copies = []
    for k in range(1, NDEV):
        px = 1 - x if k & 4 else x
        py = 1 - y if k & 2 else y
        pc = 1 - c if k & 1 else c
        peer = 4 * px + 2 * py + pc
        for a, (src, dst, scatter) in enumerate(arrs):
            copies.append(pltpu.make_async_remote_copy(
                src_ref=src.at[peer] if scatter else src,
                dst_ref=dst.at[me],
                send_sem=send_sems.at[k - 1, a],
                recv_sem=recv_sems.at[k - 1, a],
                device_id=(px, py, pc),
                device_id_type=pl.DeviceIdType.MESH,
            ))
    for a, (src, dst, scatter) in enumerate(arrs):
        copies.append(pltpu.make_async_copy(src.at[me] if scatter else src, dst.at[me], local_sems.at[a]))
    return copies


def _exchange_shapes(scat, gath):
    return ([jax.ShapeDtypeStruct(a.shape, a.dtype) for a in scat]
            + [jax.ShapeDtypeStruct((NDEV,) + a.shape, a.dtype) for a in gath])


def _exchange_sems(n):
    return [pltpu.SemaphoreType.DMA((NDEV - 1, n)), pltpu.SemaphoreType.DMA((NDEV - 1, n)),
            pltpu.SemaphoreType.DMA((n,))]


HBM_SPEC = pl.BlockSpec(memory_space=pltpu.HBM)


def _exchange(scat, gath, name):
    ns, n = len(scat), len(scat) + len(gath)

    def body(*refs):
        ins, outs, sems = refs[:n], refs[n:2 * n], refs[2 * n:]
        copies = _exchange_copies(ins[:ns], outs[:ns], ins[ns:], outs[ns:], *sems)
        for cp in copies:
            cp.start()
        for cp in copies:
            cp.wait()

    return pl.pallas_call(
        body,
        name=name,
        out_shape=_exchange_shapes(scat, gath),
        in_specs=[HBM_SPEC] * n,
        out_specs=[HBM_SPEC] * n,
        scratch_shapes=_exchange_sems(n),
    )(*scat, *gath)


def _fwd_proj(x, g_row, w_gathered, next_shards=()):
    s = x.shape[0]
    ts = min(TS_PROJ, s)
    ng = len(next_shards)
    nt = s // ts

    def body(*refs):
        x_ref, g_ref, w_ref = refs[:3]
        gath_ins = refs[3:3 + ng]
        proj_ref, ht_ref = refs[3 + ng:5 + ng]
        gath_outs = refs[5 + ng:5 + 2 * ng]
        h_scr = refs[5 + 2 * ng]
        sems = refs[6 + 2 * ng:]
        i, j = pl.program_id(0), pl.program_id(1)

        if ng:
            @pl.when((i == 0) & (j == 0))
            def _():
                for cp in _exchange_copies((), (), gath_ins, gath_outs, *sems):
                    cp.start()

        @pl.when(j == 0)
        def _():
            xf = x_ref[...]
            h = xf * lax.rsqrt(_rowmean(xf * xf) + RMS_EPS) * g_ref[...]
            h_scr[...] = h.astype(CDT)
            ht_ref[...] = h.T.astype(CDT)

        proj_ref[...] = _dot(h_scr[...], w_ref[...]).astype(CDT)

        if ng:
            @pl.when((i == nt - 1) & (j == NDEV - 1))
            def _():
                for cp in _exchange_copies((), (), gath_ins, gath_outs, *sems):
                    cp.wait()

    return pl.pallas_call(
        body,
        name="fwd_proj_gather" if ng else "fwd_proj",
        grid=(nt, NDEV),
        in_specs=[
            pl.BlockSpec((ts, D), lambda i, j: (i, 0)),
            pl.BlockSpec((1, D), lambda i, j: (0, 0)),
            pl.BlockSpec((None, D, SHARD_N), lambda i, j: (j, 0, 0)),
        ] + [HBM_SPEC] * ng,
        out_specs=[
            pl.BlockSpec((ts, SHARD_N), lambda i, j: (i, j)),
            pl.BlockSpec((D, ts), lambda i, j: (0, i)),
        ] + [HBM_SPEC] * ng,
        out_shape=[jax.ShapeDtypeStruct((s, N_IN), CDT), jax.ShapeDtypeStruct((D, s), CDT)]
        + _exchange_shapes((), next_shards),
        scratch_shapes=[pltpu.VMEM((ts, D), CDT)] + (_exchange_sems(ng) if ng else []),
        compiler_params=_cparams(("arbitrary", "arbitrary")),
    )(x, g_row, w_gathered, *next_shards)


def _a_forward(au, av, az, lng, lnb, wm_ref, bias_ref, v_scr, mix_scr):
    ts = au.shape[0]
    u, tu = _gelu(au)
    vg, tv = _gelu(av)
    xc = vg - _rowmean(vg)
    rstd = lax.rsqrt(_rowmean(xc * xc) + LN_EPS)
    vhat = xc * rstd
    v_scr[...] = (vhat * lng + lnb).astype(CDT)
    for n in range(ts // SGU_BLOCK):
        rows = slice(n * SGU_BLOCK, (n + 1) * SGU_BLOCK)
        for g in range(SGU_GROUPS):
            cols = slice(g * SGU_BLOCK, (g + 1) * SGU_BLOCK)
            mix_scr[rows, cols] = _dot(wm_ref[g], v_scr[rows, cols]) + bias_ref[:, cols]
    mixed = mix_scr[...]
    sz, sg = _silu(az)
    return dict(u=u, tu=tu, tv=tv, rstd=rstd, vhat=vhat, mixed=mixed, sz=sz, sg=sg, ya=u * mixed * sz)


def _fwd_a(proj, lng, lnb, wm, bias_full, wa):
    s = proj.shape[0]
    ts = min(TS_MIX, s)

    def body(au_ref, av_ref, az_ref, lng_ref, lnb_ref, wm_ref, bias_ref, wa_ref, pa_ref, v_scr, mix_scr):
        f = _a_forward(au_ref[...].astype(F32), av_ref[...].astype(F32), az_ref[...].astype(F32),
                       lng_ref[...], lnb_ref[...], wm_ref, bias_ref, v_scr, mix_scr)
        pa_ref[...] = _dot(f["ya"].astype(CDT), wa_ref[...]).astype(CDT)

    return pl.pallas_call(
        body,
        name="fwd_a",
        grid=(s // ts,),
        in_specs=[
            _chunk_spec(ts, C_AU), _chunk_spec(ts, C_AV), _chunk_spec(ts, C_AZ),
            _const_spec((1, D)), _const_spec((1, D)),
            _const_spec((SGU_GROUPS, SGU_BLOCK, SGU_BLOCK)), _const_spec((SGU_BLOCK, D)),
            _const_spec((D, D)),
        ],
        out_specs=pl.BlockSpec((ts, D), lambda i: (i, 0)),
        out_shape=jax.ShapeDtypeStruct((s, D), CDT),
        scratch_shapes=[pltpu.VMEM((ts, D), CDT), pltpu.VMEM((ts, D), F32)],
        compiler_params=_cparams(("arbitrary",)),
    )(proj, proj, proj, lng, lnb, wm, bias_full, wa)


def _pool_d(pe, sa, sb, tile, ts):
    r = HALO + ts
    g = POOL_GDIM
    sa[8:r, :] = pe[8:r, :] + pe[7:r - 1, :]
    sb[16:r, g:] = sa[16:r, g:] + sa[14:r - 2, g:]
    sa[24:r, 2 * g:] = sb[24:r, 2 * g:] + sb[20:r - 4, 2 * g:]
    sb[32:r, 3 * g:] = sa[32:r, 3 * g:] + sa[24:r - 8, 3 * g:]
    pos1 = tile * ts + lax.broadcasted_iota(jnp.int32, (ts, 1), 0) + 1
    invs = [1.0 / jnp.minimum(pos1, w).astype(F32) for w in POOL_WINDOWS]
    sums = [sa[HALO:r, 0:g], sb[HALO:r, g:2 * g], sa[HALO:r, 2 * g:3 * g], sb[HALO:r, 3 * g:]]
    d = [sums[k] * invs[k] - pe[HALO:r, k * g:(k + 1) * g] for k in range(POOL_GROUPS)]
    return d, invs


def _b_forward(bp_ref, bph_ref, bz, pw_ref, pb, pscale, tile, pe, sa, sb, d_scr, lin_scr):
    ts = bz.shape[0]
    pe[0:HALO, :] = jnp.where(tile > 0, bph_ref[...].astype(F32), 0.0)
    pe[HALO:, :] = bp_ref[...].astype(F32)
    d, invs = _pool_d(pe, sa, sb, tile, ts)
    for k in range(POOL_GROUPS):
        cols = slice(k * POOL_GDIM, (k + 1) * POOL_GDIM)
        d_scr[:, cols] = d[k].astype(CDT)
        lin_scr[:, cols] = _dot(d_scr[:, cols], pw_ref[k])
    lin = lin_scr[...] + pb
    sz, sg = _silu(bz)
    return dict(lin=lin, sz=sz, sg=sg, invs=invs, yb=lin * pscale * sz)


def _fwd_b(proj, pool_w, pool_b, pool_scale, wb):
    s = proj.shape[0]
    ts = min(TS_MIX, s)

    def body(bp_ref, bph_ref, bz_ref, pw_ref, pb_ref, ps_ref, wb_ref, out_ref, pe, sa, sb, d_scr, lin_scr):
        f = _b_forward(bp_ref, bph_ref, bz_ref[...].astype(F32), pw_ref, pb_ref[...], ps_ref[...],
                       pl.program_id(0), pe, sa, sb, d_scr, lin_scr)
        out_ref[...] = _dot(f["yb"].astype(CDT), wb_ref[...]).astype(CDT)

    ext = pltpu.VMEM((HALO + ts, D), F32)
    return pl.pallas_call(
        body,
        name="fwd_b",
        grid=(s // ts,),
        in_specs=[
            _chunk_spec(ts, C_BP), _halo_spec(ts, C_BP), _chunk_spec(ts, C_BZ),
            _const_spec((POOL_GROUPS, POOL_GDIM, POOL_GDIM)), _const_spec((1, D)), _const_spec((1, D)),
            _const_spec((D, D)),
        ],
        out_specs=pl.BlockSpec((ts, D), lambda i: (i, 0)),
        out_shape=jax.ShapeDtypeStruct((s, D), CDT),
        scratch_shapes=[ext, ext, ext, pltpu.VMEM((ts, D), CDT), pltpu.VMEM((ts, D), F32)],
        compiler_params=_cparams(("arbitrary",)),
    )(proj, proj, proj, pool_w, pool_b, pool_scale, wb)


def _c_forward(ch_ref, chh_ref, cb, cc_ref, cch_ref, cz, cw_ref, cbias, tile, qe):
    ts = cb.shape[0]
    r = HALO + ts
    qe[0:HALO, :] = jnp.where(tile > 0, chh_ref[...].astype(F32) * cch_ref[...].astype(F32), 0.0)
    qe[HALO:, :] = cc_ref[...].astype(F32) * ch_ref[...].astype(F32)
    conv = (cw_ref[0:1, :] * qe[HALO - 2:r - 2, :] + cw_ref[1:2, :] * qe[HALO - 1:r - 1, :]
            + cw_ref[2:3, :] * qe[HALO:r, :]) + cbias
    sz, sg = _silu(cz)
    return dict(conv=conv, sz=sz, sg=sg, yc=cb * conv * sz)


def _fwd_c(proj, conv_w, conv_b, wc):
    s = proj.shape[0]
    ts = min(TS_MIX, s)

    def body(ch_ref, chh_ref, cb_ref, cc_ref, cch_ref, cz_ref, cw_ref, cbias_ref, wc_ref, out_ref, qe):
        f = _c_forward(ch_ref, chh_ref, cb_ref[...].astype(F32), cc_ref, cch_ref, cz_ref[...].astype(F32),
                       cw_ref, cbias_ref[...], pl.program_id(0), qe)
        out_ref[...] = _dot(f["yc"].astype(CDT), wc_ref[...]).astype(CDT)

    return pl.pallas_call(
        body,
        name="fwd_c",
        grid=(s // ts,),
        in_specs=[
            _chunk_spec(ts, C_CH), _halo_spec(ts, C_CH), _chunk_spec(ts, C_CB),
            _chunk_spec(ts, C_CC), _halo_spec(ts, C_CC), _chunk_spec(ts, C_CZ),
            _const_spec((3, D)), _const_spec((1, D)), _const_spec((D, D)),
        ],
        out_specs=pl.BlockSpec((ts, D), lambda i: (i, 0)),
        out_shape=jax.ShapeDtypeStruct((s, D), CDT),
        scratch_shapes=[pltpu.VMEM((HALO + ts, D), F32)],
        compiler_params=_cparams(("arbitrary",)),
    )(proj, proj, proj, proj, proj, proj, conv_w, conv_b, wc)


def _merge(pa, pb, pc, logits):
    gates = [jax.nn.sigmoid(logits[:, k * D:(k + 1) * D].astype(F32)) for k in range(3)]
    ps = [pa.astype(F32), pb.astype(F32), pc.astype(F32)]
    merged = gates[0] * ps[0] + gates[1] * ps[1] + gates[2] * ps[2]
    return gates, ps, merged


def _fwd_o(x, pa, pb, pc, proj, wo):
    s = x.shape[0]
    ts = min(TS_MIX, s)

    def body(x_ref, pa_ref, pb_ref, pc_ref, lg_ref, wo_ref, out_ref):
        _, _, merged = _merge(pa_ref[...], pb_ref[...], pc_ref[...], lg_ref[...])
        out_ref[...] = x_ref[...] + _dot(merged.astype(CDT), wo_ref[...])

    tile = pl.BlockSpec((ts, D), lambda i: (i, 0))
    return pl.pallas_call(
        body,
        name="fwd_o",
        grid=(s // ts,),
        in_specs=[tile, tile, tile, tile, _chunk_spec(ts, 3, width=3), _const_spec((D, D))],
        out_specs=tile,
        out_shape=jax.ShapeDtypeStruct((s, D), F32),
        compiler_params=_cparams(("arbitrary",)),
    )(x, pa, pb, pc, proj, wo)


def _loss_head(x, target, fg_row):
    s = x.shape[0]
    ts = min(TS_MIX, s)

    def body(x_ref, t_ref, g_ref, dx_ref, loss_ref, dg_ref):
        @pl.when(pl.program_id(0) == 0)
        def _():
            loss_ref[...] = jnp.zeros_like(loss_ref)
            dg_ref[...] = jnp.zeros_like(dg_ref)

        xf = x_ref[...]
        g = g_ref[...]
        r = lax.rsqrt(_rowmean(xf * xf) + RMS_EPS)
        xhat = xf * r
        err = xhat * g - t_ref[...]
        loss_ref[...] += _colsum(err * err)
        dy = err * (1.0 / D)
        dg_ref[...] += _colsum(dy * xhat)
        dxhat = dy * g
        dx_ref[...] = r * (dxhat - xhat * _rowmean(dxhat * xhat))

    tile = pl.BlockSpec((ts, D), lambda i: (i, 0))
    row = _const_spec((1, D))
    return pl.pallas_call(
        body,
        name="loss_head",
        grid=(s // ts,),
        in_specs=[tile, tile, row],
        out_specs=[tile, row, row],
        out_shape=[jax.ShapeDtypeStruct((s, D), F32), jax.ShapeDtypeStruct((1, D), F32),
                   jax.ShapeDtypeStruct((1, D), F32)],
        compiler_params=_cparams(("arbitrary",)),
    )(x, target, fg_row)


def _zero_first(refs):
    @pl.when(pl.program_id(0) == 0)
    def _():
        for r in refs:
            r[...] = jnp.zeros_like(r)


def _bwd_o(dx, pa, pb, pc, proj, wo_t):
    s = dx.shape[0]
    ts = min(TS_MIX, s)

    def body(dx_ref, pa_ref, pb_ref, pc_ref, lg_ref, wot_ref, dp_ref, dg_ref, dwo_ref):
        _zero_first([dwo_ref])
        gates, ps, merged = _merge(pa_ref[...], pb_ref[...], pc_ref[...], lg_ref[...])
        dxb = dx_ref[...].astype(CDT)
        dwo_ref[...] += lax.dot_general(merged.astype(CDT), dxb, TN_DIMS, preferred_element_type=F32)
        dmerged = _dot(dxb, wot_ref[...])
        for k in range(3):
            cols = slice(k * D, (k + 1) * D)
            dpk = dmerged * gates[k]
            dp_ref[:, cols] = dpk.astype(CDT)
            dg_ref[:, cols] = (dpk * ps[k] * (1.0 - gates[k])).astype(CDT)

    tile = pl.BlockSpec((ts, D), lambda i: (i, 0))
    wide = pl.BlockSpec((ts, 3 * D), lambda i: (i, 0))
    return pl.pallas_call(
        body,
        name="bwd_o",
        grid=(s // ts,),
        in_specs=[tile, tile, tile, tile, _chunk_spec(ts, 3, width=3), _const_spec((D, D))],
        out_specs=[wide, wide, _const_spec((D, D))],
        out_shape=[jax.ShapeDtypeStruct((s, 3 * D), CDT), jax.ShapeDtypeStruct((s, 3 * D), CDT),
                   jax.ShapeDtypeStruct((D, D), F32)],
        compiler_params=_cparams(("arbitrary",)),
    )(dx, pa, pb, pc, proj, wo_t)


def _bwd_a(proj, dp, lng, lnb, wm, wm_t, bias_full, wa_t):
    s = proj.shape[0]
    ts = min(TS_MIX, s)
    nt = s // ts

    def body(au_ref, av_ref, az_ref, dpa_ref, lng_ref, lnb_ref, wm_ref, wmt_ref, bias_ref, wat_ref,
             da_ref, dwa_ref, dwm_ref, dbs_ref, dlng_ref, dlnb_ref,
             v_scr, mix_scr, dm_scr, dmf_scr, dv_scr, bsacc):
        _zero_first([dwa_ref, dwm_ref, dlng_ref, dlnb_ref, bsacc])
        au = au_ref[...].astype(F32)
        av = av_ref[...].astype(F32)
        az = az_ref[...].astype(F32)
        lng = lng_ref[...]
        f = _a_forward(au, av, az, lng, lnb_ref[...], wm_ref, bias_ref, v_scr, mix_scr)
        dpa = dpa_ref[...]
        dwa_ref[...] += lax.dot_general(f["ya"].astype(CDT), dpa, TN_DIMS, preferred_element_type=F32)
        dya = _dot(dpa, wat_ref[...])
        t1 = dya * f["mixed"]
        da_ref[:, 0:D] = (t1 * f["sz"] * _gelu_grad(au, f["tu"])).astype(CDT)
        da_ref[:, 2 * D:3 * D] = (t1 * f["u"] * _silu_grad(az, f["sg"])).astype(CDT)
        dmix = dya * f["u"] * f["sz"]
        dmf_scr[...] = dmix
        dm_scr[...] = dmix.astype(CDT)
        for n in range(ts // SGU_BLOCK):
            rows = slice(n * SGU_BLOCK, (n + 1) * SGU_BLOCK)
            for g in range(SGU_GROUPS):
                cols = slice(g * SGU_BLOCK, (g + 1) * SGU_BLOCK)
                dmb = dm_scr[rows, cols]
                dwm_ref[g] += lax.dot_general(dmb, v_scr[rows, cols], NT_DIMS, preferred_element_type=F32)
                dv_scr[rows, cols] = _dot(wmt_ref[g], dmb)
                bsacc[g] += dmf_scr[rows, cols]
        dvln = dv_scr[...]
        vhat = f["vhat"]
        dlng_ref[...] += _colsum(dvln * vhat)
        dlnb_ref[...] += _colsum(dvln)
        dvhat = dvln * lng
        dvg = f["rstd"] * (dvhat - _rowmean(dvhat) - vhat * _rowmean(dvhat * vhat))
        da_ref[:, D:2 * D] = (dvg * _gelu_grad(av, f["tv"])).astype(CDT)

        @pl.when(pl.program_id(0) == nt - 1)
        def _():
            ones = jnp.ones((8, SGU_BLOCK), F32)
            for g in range(SGU_GROUPS):
                red = lax.dot_general(ones, bsacc[g], NT_DIMS, preferred_element_type=F32,
                                      precision=lax.Precision.HIGHEST)
                dbs_ref[g:g + 1, :] = red[0:1, :]

    gshape = (SGU_GROUPS, SGU_BLOCK, SGU_BLOCK)
    return pl.pallas_call(
        body,
        name="bwd_a",
        grid=(nt,),
        in_specs=[
            _chunk_spec(ts, C_AU), _chunk_spec(ts, C_AV), _chunk_spec(ts, C_AZ), _chunk_spec(ts, 0),
            _const_spec((1, D)), _const_spec((1, D)), _const_spec(gshape), _const_spec(gshape),
            _const_spec((SGU_BLOCK, D)), _const_spec((D, D)),
        ],
        out_specs=[
            pl.BlockSpec((ts, 3 * D), lambda i: (i, 0)), _const_spec((D, D)), _const_spec(gshape),
            _const_spec((SGU_GROUPS, SGU_BLOCK)), _const_spec((1, D)), _const_spec((1, D)),
        ],
        out_shape=[
            jax.ShapeDtypeStruct((s, 3 * D), CDT), jax.ShapeDtypeStruct((D, D), F32),
            jax.ShapeDtypeStruct(gshape, F32), jax.ShapeDtypeStruct((SGU_GROUPS, SGU_BLOCK), F32),
            jax.ShapeDtypeStruct((1, D), F32), jax.ShapeDtypeStruct((1, D), F32),
        ],
        scratch_shapes=[
            pltpu.VMEM((ts, D), CDT), pltpu.VMEM((ts, D), F32), pltpu.VMEM((ts, D), CDT),
            pltpu.VMEM((ts, D), F32), pltpu.VMEM((ts, D), F32), pltpu.VMEM(gshape, F32),
        ],
        compiler_params=_cparams(("arbitrary",)),
    )(proj, proj, proj, dp, lng, lnb, wm, wm_t, bias_full, wa_t)


def _bwd_b(proj, dp, pool_w, pool_w_t, pool_b, pool_scale, wb_t):
    s = proj.shape[0]
    ts = min(TS_MIX, s)
    nt = s // ts
    g = POOL_GDIM

    def body(bp_ref, bph_ref, bz_ref, dpb_ref, pw_ref, pwt_ref, pb_ref, ps_ref, wbt_ref,
             db_ref, dwb_ref, dpw_ref, dpb_out_ref, dps_ref,
             pe, sa, sb, d_scr, lin_scr, ee, dl_scr, carry):
        _zero_first([dwb_ref, dpw_ref, dpb_out_ref, dps_ref, carry])
        tile = nt - 1 - pl.program_id(0)
        bz = bz_ref[...].astype(F32)
        pscale = ps_ref[...]
        f = _b_forward(bp_ref, bph_ref, bz, pw_ref, pb_ref[...], pscale, tile, pe, sa, sb, d_scr, lin_scr)
        dpb = dpb_ref[...]
        dwb_ref[...] += lax.dot_general(f["yb"].astype(CDT), dpb, TN_DIMS, preferred_element_type=F32)
        dyb = _dot(dpb, wbt_ref[...])
        t1 = dyb * f["lin"]
        dps_ref[...] += _colsum(t1 * f["sz"])
        db_ref[:, D:2 * D] = (t1 * pscale * _silu_grad(bz, f["sg"])).astype(CDT)
        dlin = dyb * pscale * f["sz"]
        dpb_out_ref[...] += _colsum(dlin)
        dl_scr[...] = dlin.astype(CDT)
        r = HALO + ts
        ee[ts:r, :] = carry[...]
        for k in range(POOL_GROUPS):
            cols = slice(k * g, (k + 1) * g)
            dlk = dl_scr[:, cols]
            dpw_ref[k] += lax.dot_general(d_scr[:, cols], dlk, TN_DIMS, preferred_element_type=F32)
            dd = _dot(dlk, pwt_ref[k])
            lin_scr[:, cols] = dd
            ee[0:ts, cols] = dd * f["invs"][k]
        carry[...] = ee[0:HALO, :]
        sa[0:ts + 24, :] = ee[0:ts + 24, :] + ee[1:ts + 25, :]
        sb[0:ts + 16, g:] = sa[0:ts + 16, g:] + sa[2:ts + 18, g:]
        sa[0:ts + 8, 2 * g:] = sb[0:ts + 8, 2 * g:] + sb[4:ts + 12, 2 * g:]
        sb[0:ts, 3 * g:] = sa[0:ts, 3 * g:] + sa[8:ts + 8, 3 * g:]
        sums = [sa[0:ts, 0:g], sb[0:ts, g:2 * g], sa[0:ts, 2 * g:3 * g], sb[0:ts, 3 * g:]]
        for k in range(POOL_GROUPS):
            cols = slice(k * g, (k + 1) * g)
            db_ref[:, cols] = (sums[k] - lin_scr[:, cols]).astype(CDT)

    ext = pltpu.VMEM((HALO + ts, D), F32)
    wshape = (POOL_GROUPS, g, g)
    return pl.pallas_call(
        body,
        name="bwd_b",
        grid=(nt,),
        in_specs=[
            _chunk_spec(ts, C_BP, rev_nt=nt), _halo_spec(ts, C_BP, rev_nt=nt), _chunk_spec(ts, C_BZ, rev_nt=nt),
            _chunk_spec(ts, 1, rev_nt=nt),
            _const_spec(wshape), _const_spec(wshape), _const_spec((1, D)), _const_spec((1, D)), _const_spec((D, D)),
        ],
        out_specs=[
            pl.BlockSpec((ts, 2 * D), lambda i: (nt - 1 - i, 0)), _const_spec((D, D)), _const_spec(wshape),
            _const_spec((1, D)), _const_spec((1, D)),
        ],
        out_shape=[
            jax.ShapeDtypeStruct((s, 2 * D), CDT), jax.ShapeDtypeStruct((D, D), F32),
            jax.ShapeDtypeStruct(wshape, F32), jax.ShapeDtypeStruct((1, D), F32), jax.ShapeDtypeStruct((1, D), F32),
        ],
        scratch_shapes=[ext, ext, ext, pltpu.VMEM((ts, D), CDT), pltpu.VMEM((ts, D), F32), ext,
                        pltpu.VMEM((ts, D), CDT), pltpu.VMEM((HALO, D), F32)],
        compiler_params=_cparams(("arbitrary",)),
    )(proj, proj, proj, dp, pool_w, pool_w_t, pool_b, pool_scale, wb_t)


def _bwd_c(proj, dp, conv_w, conv_b, wc_t):
    s = proj.shape[0]
    ts = min(TS_MIX, s)
    nt = s // ts

    def body(ch_ref, chh_ref, cb_ref, cc_ref, cch_ref, cz_ref, dpc_ref, cw_ref, cbias_ref, wct_ref,
             dc_ref, dwc_ref, dcw_ref, dcb_ref, qe, de, carry):
        _zero_first([dwc_ref, dcw_ref, dcb_ref, carry])
        tile = nt - 1 - pl.program_id(0)
        cb = cb_ref[...].astype(F32)
        cz = cz_ref[...].astype(F32)
        f = _c_forward(ch_ref, chh_ref, cb, cc_ref, cch_ref, cz, cw_ref, cbias_ref[...], tile, qe)
        dpc = dpc_ref[...]
        dwc_ref[...] += lax.dot_general(f["yc"].astype(CDT), dpc, TN_DIMS, preferred_element_type=F32)
        dyc = _dot(dpc, wct_ref[...])
        t1 = dyc * f["conv"]
        dc_ref[:, D:2 * D] = (t1 * f["sz"]).astype(CDT)
        dc_ref[:, 3 * D:4 * D] = (t1 * cb * _silu_grad(cz, f["sg"])).astype(CDT)
        dconv = dyc * cb * f["sz"]
        r = HALO + ts
        dcb_ref[...] += _colsum(dconv)
        dcw_ref[0:1, :] += _colsum(dconv * qe[HALO - 2:r - 2, :])
        dcw_ref[1:2, :] += _colsum(dconv * qe[HALO - 1:r - 1, :])
        dcw_ref[2:3, :] += _colsum(dconv * qe[HALO:r, :])
        de[0:ts, :] = dconv
        de[ts:ts + 8, :] = carry[...]
        carry[...] = de[0:8, :]
        dq = cw_ref[2:3, :] * dconv + cw_ref[1:2, :] * de[1:ts + 1, :] + cw_ref[0:1, :] * de[2:ts + 2, :]
        dc_ref[:, 0:D] = (dq * cc_ref[...].astype(F32)).astype(CDT)
        dc_ref[:, 2 * D:3 * D] = (dq * ch_ref[...].astype(F32)).astype(CDT)

    return pl.pallas_call(
        body,
        name="bwd_c",
        grid=(nt,),
        in_specs=[
            _chunk_spec(ts, C_CH, rev_nt=nt), _halo_spec(ts, C_CH, rev_nt=nt), _chunk_spec(ts, C_CB, rev_nt=nt),
            _chunk_spec(ts, C_CC, rev_nt=nt), _halo_spec(ts, C_CC, rev_nt=nt), _chunk_spec(ts, C_CZ, rev_nt=nt),
            _chunk_spec(ts, 2, rev_nt=nt),
            _const_spec((3, D)), _const_spec((1, D)), _const_spec((D, D)),
        ],
        out_specs=[
            pl.BlockSpec((ts, 4 * D), lambda i: (nt - 1 - i, 0)), _const_spec((D, D)), _const_spec((8, D)),
            _const_spec((1, D)),
        ],
        out_shape=[
            jax.ShapeDtypeStruct((s, 4 * D), CDT), jax.ShapeDtypeStruct((D, D), F32),
            jax.ShapeDtypeStruct((8, D), F32), jax.ShapeDtypeStruct((1, D), F32),
        ],
        scratch_shapes=[pltpu.VMEM((HALO + ts, D), F32), pltpu.VMEM((ts + 8, D), F32), pltpu.VMEM((8, D), F32)],
        compiler_params=_cparams(("arbitrary",)),
    )(proj, proj, proj, proj, proj, proj, dp, conv_w, conv_b, wc_t)


DPROJ_STARTS = (0, 3, 5, 9)
DPROJ_WIDTHS = (3, 2, 4, 3)


def _bwd_proj_dx(d_arrs, w_in_t, x, g_row, dx_out, scat, gath):
    s = x.shape[0]
    ts = min(TS_PROJ, s)
    nk = N_IN // D
    nt = s // ts
    ns, nx = len(scat), len(scat) + len(gath)

    def body(*refs):
        da_ref, db_ref, dc_ref, dg_ref, wt_ref, x_ref, g_ref, dxo_ref = refs[:8]
        ex_ins = refs[8:8 + nx]
        dxi_ref, dng_ref = refs[8 + nx:10 + nx]
        ex_outs = refs[10 + nx:10 + 2 * nx]
        acc = refs[10 + 2 * nx]
        sems = refs[11 + 2 * nx:]
        i, k = pl.program_id(0), pl.program_id(1)

        def exchange():
            return _exchange_copies(ex_ins[:ns], ex_outs[:ns], ex_ins[ns:], ex_outs[ns:], *sems)

        @pl.when((i == 0) & (k == 0))
        def _():
            dng_ref[...] = jnp.zeros_like(dng_ref)
            for cp in exchange():
                cp.start()

        @pl.when(k == 0)
        def _():
            acc[...] = jnp.zeros_like(acc)

        for ref, start, width in zip((da_ref, db_ref, dc_ref, dg_ref), DPROJ_STARTS, DPROJ_WIDTHS):
            @pl.when((k >= start) & (k < start + width))
            def _(ref=ref):
                acc[...] += _dot(ref[...], wt_ref[...])

        @pl.when(k == nk - 1)
        def _():
            xf = x_ref[...]
            g = g_ref[...]
            dh = acc[...]
            r = lax.rsqrt(_rowmean(xf * xf) + RMS_EPS)
            xhat = xf * r
            dng_ref[...] += _colsum(dh * xhat)
            dxhat = dh * g
            dxi_ref[...] = dxo_ref[...] + r * (dxhat - xhat * _rowmean(dxhat * xhat))

        @pl.when((i == nt - 1) & (k == nk - 1))
        def _():
            for cp in exchange():
                cp.wait()

    def dspec(start, width):
        return pl.BlockSpec((ts, D), lambda i, k: (i, jnp.clip(k - start, 0, width - 1)))

    tile = pl.BlockSpec((ts, D), lambda i, k: (i, 0))
    row = pl.BlockSpec((1, D), lambda i, k: (0, 0))
    return pl.pallas_call(
        body,
        name="bwd_proj_dx_exchange",
        grid=(nt, nk),
        in_specs=[dspec(st, w) for st, w in zip(DPROJ_STARTS, DPROJ_WIDTHS)] + [
            pl.BlockSpec((D, D), lambda i, k: (k, 0)), tile, row, tile,
        ] + [HBM_SPEC] * nx,
        out_specs=[tile, row] + [HBM_SPEC] * nx,
        out_shape=[jax.ShapeDtypeStruct((s, D), F32), jax.ShapeDtypeStruct((1, D), F32)]
        + _exchange_shapes(scat, gath),
        scratch_shapes=[pltpu.VMEM((ts, D), F32)] + _exchange_sems(nx),
        compiler_params=_cparams(("arbitrary", "arbitrary")),
    )(*d_arrs, w_in_t, x, g_row, dx_out, *scat, *gath)


def _bwd_proj_dw(h_t, d_arrs):
    s = h_t.shape[1]
    tk = min(2048, s)
    tn = 512
    per_shard = SHARD_N // tn
    starts = [st * D // tn for st in DPROJ_STARTS]
    widths = [w * D // tn for w in DPROJ_WIDTHS]
    nk = s // tk

    def body(ht_ref, da_ref, db_ref, dc_ref, dg_ref, out_ref, acc):
        n, k = pl.program_id(0), pl.program_id(1)

        @pl.when(k == 0)
        def _():
            acc[...] = jnp.zeros_like(acc)

        for ref, start, width in zip((da_ref, db_ref, dc_ref, dg_ref), starts, widths):
            @pl.when((n >= start) & (n < start + width))
            def _(ref=ref):
                acc[...] += _dot(ht_ref[...], ref[...])

        @pl.when(k == nk - 1)
        def _():
            out_ref[...] = acc[...].astype(CDT)

    def dspec(start, width):
        def index(n, k):
            active = (n >= start) & (n < start + width)
            return (jnp.where(active, k, 0), jnp.clip(n - start, 0, width - 1))
        return pl.BlockSpec((tk, tn), index)

    return pl.pallas_call(
        body,
        name="bwd_proj_dw",
        grid=(N_IN // tn, nk),
        in_specs=[pl.BlockSpec((D, tk), lambda n, k: (0, k))] + [dspec(st, w) for st, w in zip(starts, widths)],
        out_specs=pl.BlockSpec((None, D, tn), lambda n, k: (n // per_shard, 0, n % per_shard)),
        out_shape=jax.ShapeDtypeStruct((NDEV, D, SHARD_N), CDT),
        scratch_shapes=[pltpu.VMEM((D, tn), F32)],
        compiler_params=_cparams(("arbitrary", "arbitrary")),
    )(h_t, *d_arrs)


def _adamw(recvs, w, m, v, tr, name):
    nq = len(recvs)
    _, rows, c = recvs[0].shape
    assert rows % tr == 0 and w.shape == (nq * rows, c)
    steps = rows // tr

    def body(*refs):
        rrefs = refs[:nq]
        w_ref, m_ref, v_ref, g_out, d_out, m_out, v_out = refs[nq:]
        q = pl.program_id(0)
        for qi in range(nq):
            @pl.when(q == qi)
            def _(r=rrefs[qi]):
                g = r[0].astype(F32)
                for j in range(1, NDEV):
                    g = g + r[j].astype(F32)
                wv = w_ref[...]
                m2 = ADAM_B1 * m_ref[...] + (1.0 - ADAM_B1) * g
                v2 = ADAM_B2 * v_ref[...] + (1.0 - ADAM_B2) * (g * g)
                m_hat = m2 / (1.0 - ADAM_B1 ** ADAM_STEP)
                v_hat = v2 / (1.0 - ADAM_B2 ** ADAM_STEP)
                g_out[...] = g
                d_out[...] = -ADAM_LR * (m_hat / (jnp.sqrt(v_hat) + ADAM_EPS) + ADAM_WD * wv)
                m_out[...] = m2
                v_out[...] = v2

    def rspec(qi):
        return pl.BlockSpec((NDEV, tr, c), lambda q, t: (0, jnp.where(q == qi, t, 0), 0))

    tile = pl.BlockSpec((tr, c), lambda q, t: (q * steps + t, 0))
    shp = jax.ShapeDtypeStruct(w.shape, F32)
    return pl.pallas_call(
        body,
        name=name,
        grid=(nq, steps),
        in_specs=[rspec(qi) for qi in range(nq)] + [tile, tile, tile],
        out_specs=[tile] * 4,
        out_shape=[shp] * 4,
        compiler_params=_cparams(("arbitrary", "arbitrary")),
    )(*recvs, w, m, v)


SMALL_ROWS = 264
REP_NAMES = ("sgu_ln_g", "sgu_ln_b", "sgu_b", "pool_b", "pool_scale", "conv_b")


def _pack_small(pool_w_l, conv_w_l):
    flat = jnp.concatenate([pool_w_l.reshape(-1), conv_w_l.reshape(-1)])
    return jnp.pad(flat, (0, SMALL_ROWS * 128 - flat.shape[0])).reshape(SMALL_ROWS, 128)


def _unpack_small(packed):
    flat = packed.reshape(DEPTH, SMALL_ROWS * 128)
    n_pool = POOL_GROUPS * (POOL_GDIM // NDEV) * POOL_GDIM
    pool = flat[:, :n_pool].reshape(DEPTH, POOL_GROUPS, POOL_GDIM // NDEV, POOL_GDIM)
    conv = flat[:, n_pool:n_pool + 3 * (D // NDEV)].reshape(DEPTH, 3, D // NDEV)
    return pool, conv


REP_LAYER_ROWS = len(REP_NAMES) * 8 + SGU_GROUPS * SGU_BLOCK
NORM_ROWS = 8 * (DEPTH + 1)


def _pack_rep_layer(vals, sgu_w_l):
    parts = [vals[nme].reshape(8, 128) for nme in REP_NAMES]
    parts.append(sgu_w_l.reshape(SGU_GROUPS * SGU_BLOCK, SGU_BLOCK))
    return jnp.concatenate(parts, axis=0)


def _pack_rep(per_layer, sgu_w):
    return jnp.concatenate(
        [_pack_rep_layer({nme: per_layer[nme][l] for nme in REP_NAMES}, sgu_w[l]) for l in range(DEPTH)], axis=0)


def _unpack_rep(packed):
    per_layer = {nme: [] for nme in REP_NAMES}
    sgu_w = []
    for l in range(DEPTH):
        base = l * REP_LAYER_ROWS
        for q, nme in enumerate(REP_NAMES):
            per_layer[nme].append(packed[base + 8 * q:base + 8 * q + 8])
        sgu_w.append(packed[base + 8 * len(REP_NAMES):base + REP_LAYER_ROWS].reshape(SGU_GROUPS, SGU_BLOCK, SGU_BLOCK))
    out = {nme: jnp.stack(vs).reshape(DEPTH, D) for nme, vs in per_layer.items()}
    out["sgu_b"] = out["sgu_b"].reshape(DEPTH, SGU_GROUPS, SGU_BLOCK)
    return out, jnp.stack(sgu_w)


def _pack_norm(norm_g_like, final_g_like):
    return jnp.concatenate([norm_g_like.reshape(DEPTH * 8, 128), final_g_like.reshape(8, 128)], axis=0)


def _unpack_norm(packed):
    return packed[:DEPTH * 8].reshape(DEPTH, D), packed[DEPTH * 8:].reshape(D)


def kernel(x, norm_g, w_in, sgu_ln_g, sgu_ln_b, sgu_w, sgu_b, pool_w, pool_b, pool_scale, conv_w, conv_b, w_branch_a, w_branch_b, w_branch_c, w_out, final_g, loss_target, m_norm_g, m_w_in, m_sgu_ln_g, m_sgu_ln_b, m_sgu_w, m_sgu_b, m_pool_w, m_pool_b, m_pool_scale, m_conv_w, m_conv_b, m_w_branch_a, m_w_branch_b, m_w_branch_c, m_w_out, m_final_g, v_norm_g, v_w_in, v_sgu_ln_g, v_sgu_ln_b, v_sgu_w, v_sgu_b, v_pool_w, v_pool_b, v_pool_scale, v_conv_w, v_conv_b, v_w_branch_a, v_w_branch_b, v_w_branch_c, v_w_out, v_final_g):
    assert x.shape[0] == 1 and x.shape[2] == D
    s = x.shape[1]
    x0 = x.reshape(s, D)
    target = loss_target.reshape(s, D)

    w_br = jnp.stack([w_branch_a, w_branch_b, w_branch_c, w_out], axis=1)
    shards = [[w_in[l].astype(CDT), w_br[l].astype(CDT), pool_w[l].astype(CDT), conv_w[l]] for l in range(DEPTH)]

    def layout(gathered):
        g_in, g_br, g_pool, g_conv = gathered
        mats = jnp.transpose(g_br, (1, 0, 2, 3)).reshape(4, D, D)
        pool_full = jnp.transpose(g_pool, (1, 0, 2, 3)).reshape(POOL_GROUPS, POOL_GDIM, POOL_GDIM)
        return dict(
            g_in=g_in,
            w_in_t=jnp.transpose(g_in, (0, 2, 1)).reshape(N_IN, D),
            mats=mats, mats_t=jnp.transpose(mats, (0, 2, 1)),
            pool=pool_full, pool_t=jnp.transpose(pool_full, (0, 2, 1)),
            conv=jnp.transpose(g_conv, (1, 0, 2)).reshape(3, D),
        )

    pos = np.arange(SGU_BLOCK) // CHUNK
    mask = jnp.asarray(pos[None, :] <= pos[:, None])
    wm_all = jnp.where(mask[None, None], sgu_w, 0.0).astype(CDT)
    wm_all_t = jnp.transpose(wm_all, (0, 1, 3, 2))
    bias_all = jnp.repeat(jnp.transpose(sgu_b, (0, 2, 1)), SGU_BLOCK, axis=2)

    def row(a, l):
        return a[l].reshape(1, D)

    weights = [None] * DEPTH
    weights[0] = layout(_exchange([], shards[0], "gather_weights"))
    saved = []
    xl = x0
    for l in range(DEPTH):
        w = weights[l]
        if l + 1 < DEPTH:
            proj, h_t, *nxt = _fwd_proj(xl, row(norm_g, l), w["g_in"], shards[l + 1])
            weights[l + 1] = layout(nxt)
        else:
            proj, h_t = _fwd_proj(xl, row(norm_g, l), w["g_in"])
        pa = _fwd_a(proj, row(sgu_ln_g, l), row(sgu_ln_b, l), wm_all[l], bias_all[l], w["mats"][0])
        pb = _fwd_b(proj, w["pool"], row(pool_b, l), row(pool_scale, l), w["mats"][1])
        pc = _fwd_c(proj, w["conv"], row(conv_b, l), w["mats"][2])
        saved.append((xl, proj, h_t, pa, pb, pc))
        xl = _fwd_o(xl, pa, pb, pc, proj, w["mats"][3])

    dx, loss_cols, dfinal_g = _loss_head(xl, target, final_g.reshape(1, D))
    loss = lax.psum(0.5 / D * jnp.sum(loss_cols), ("x", "y", "c"))

    norm_grads = [None] * DEPTH
    recv_in, recv_br, recv_small, recv_rep = ([None] * DEPTH for _ in range(4))
    for l in reversed(range(DEPTH)):
        w = weights[l]
        xl, proj, h_t, pa, pb, pc = saved[l]
        dp, d_g, dwo = _bwd_o(dx, pa, pb, pc, proj, w["mats_t"][3])
        d_a, dwa, dwm, dbs, dlng, dlnb = _bwd_a(
            proj, dp, row(sgu_ln_g, l), row(sgu_ln_b, l), wm_all[l], wm_all_t[l], bias_all[l], w["mats_t"][0])
        d_b, dwb, dpw, dpb, dps = _bwd_b(
            proj, dp, w["pool"], w["pool_t"], row(pool_b, l), row(pool_scale, l), w["mats_t"][1])
        d_c, dwc, dcw, dcb = _bwd_c(proj, dp, w["conv"], row(conv_b, l), w["mats_t"][2])
        d_arrs = (d_a, d_b, d_c, d_g)
        dw_in = _bwd_proj_dw(h_t, d_arrs)

        dw_br = jnp.stack([dwa, dwb, dwc, dwo]).astype(CDT).reshape(4, NDEV, D // NDEV, D).transpose(1, 0, 2, 3)
        dpw_j = dpw.reshape(POOL_GROUPS, NDEV, POOL_GDIM // NDEV, POOL_GDIM).transpose(1, 0, 2, 3)
        dcw_j = dcw[0:3].reshape(3, NDEV, D // NDEV).transpose(1, 0, 2)
        small = jnp.stack([_pack_small(dpw_j[j], dcw_j[j]) for j in range(NDEV)])
        rep = _pack_rep_layer(
            dict(sgu_ln_g=dlng, sgu_ln_b=dlnb, sgu_b=dbs, pool_b=dpb, pool_scale=dps, conv_b=dcb),
            jnp.where(mask[None], dwm, 0.0))
        dx, norm_grads[l], recv_in[l], recv_br[l], recv_small[l], recv_rep[l] = _bwd_proj_dx(
            d_arrs, w["w_in_t"], xl, row(norm_g, l), dx, [dw_in, dw_br, small], [rep])

    (recv_norm,) = _exchange([], [_pack_norm(jnp.stack(norm_grads), dfinal_g)], "gather_norm_grads")

    rep_w = dict(sgu_ln_g=sgu_ln_g, sgu_ln_b=sgu_ln_b, sgu_b=sgu_b, pool_b=pool_b, pool_scale=pool_scale, conv_b=conv_b)
    rep_m = dict(sgu_ln_g=m_sgu_ln_g, sgu_ln_b=m_sgu_ln_b, sgu_b=m_sgu_b, pool_b=m_pool_b, pool_scale=m_pool_scale,
                 conv_b=m_conv_b)
    rep_v = dict(sgu_ln_g=v_sgu_ln_g, sgu_ln_b=v_sgu_ln_b, sgu_b=v_sgu_b, pool_b=v_pool_b, pool_scale=v_pool_scale,
                 conv_b=v_conv_b)
    rep_out = _adamw(recv_rep, _pack_rep(rep_w, sgu_w), _pack_rep(rep_m, m_sgu_w), _pack_rep(rep_v, v_sgu_w),
                     REP_LAYER_ROWS // 2, "adamw_replicated")
    rep_res = [_unpack_rep(o) for o in rep_out]
    norm_out = _adamw([recv_norm], _pack_norm(norm_g, final_g), _pack_norm(m_norm_g, m_final_g),
                      _pack_norm(v_norm_g, v_final_g), NORM_ROWS, "adamw_norm")
    norm_res = [_unpack_norm(o) for o in norm_out]

    in_out = _adamw(recv_in, w_in.reshape(DEPTH * D, SHARD_N), m_w_in.reshape(DEPTH * D, SHARD_N),
                    v_w_in.reshape(DEPTH * D, SHARD_N), 64, "adamw_w_in")
    in_res = [o.reshape(DEPTH, D, SHARD_N) for o in in_out]

    def stack_br(a, b, c, o):
        return jnp.stack([a, b, c, o], axis=1).reshape(DEPTH * 4 * (D // NDEV), D)

    br_out = _adamw(
        [r.reshape(NDEV, 4 * (D // NDEV), D) for r in recv_br],
        stack_br(w_branch_a, w_branch_b, w_branch_c, w_out),
        stack_br(m_w_branch_a, m_w_branch_b, m_w_branch_c, m_w_out),
        stack_br(v_w_branch_a, v_w_branch_b, v_w_branch_c, v_w_out), 128, "adamw_w_branch")
    br_res = [o.reshape(DEPTH, 4, D // NDEV, D) for o in br_out]

    def stack_small(pw, cw):
        return jnp.concatenate([_pack_small(pw[l], cw[l]) for l in range(DEPTH)], axis=0)

    small_out = _adamw(recv_small, stack_small(pool_w, conv_w), stack_small(m_pool_w, m_conv_w),
                       stack_small(v_pool_w, v_conv_w), SMALL_ROWS, "adamw_small")
    small_res = [_unpack_small(o) for o in small_out]

    def leaves(kind):
        rep, sgu_w_k = rep_res[kind]
        norm_k, final_k = norm_res[kind]
        br = br_res[kind]
        pool_k, conv_k = small_res[kind]
        return [norm_k, in_res[kind], rep["sgu_ln_g"], rep["sgu_ln_b"], sgu_w_k, rep["sgu_b"], pool_k,
                rep["pool_b"], rep["pool_scale"], conv_k, rep["conv_b"], br[:, 0], br[:, 1], br[:, 2], br[:, 3],
                final_k]

    return (loss, dx.reshape(1, s, D), *leaves(0), *leaves(1), *leaves(2), *leaves(3))
```

```python
import functools

import jax
import jax.numpy as jnp
import numpy as np
from jax import lax
from jax.experimental import pallas as pl
from jax.experimental.pallas import tpu as pltpu

F32 = jnp.float32
CDT = jnp.bfloat16

D = 1024
N_IN = 12 * D
DEPTH = 4
NDEV = 8
SHARD_N = N_IN // NDEV
SGU_BLOCK = 128
SGU_GROUPS = 8
CHUNK = 64
POOL_GROUPS = 4
POOL_GDIM = D // POOL_GROUPS
POOL_WINDOWS = (2, 4, 8, 16)
RMS_EPS = 1e-6
LN_EPS = 1e-5
GELU_K0 = float(np.sqrt(2.0 / np.pi))
GELU_K1 = 0.044715

ADAM_LR = 0.001
ADAM_B1 = 0.9
ADAM_B2 = 0.999
ADAM_EPS = 1e-08
ADAM_WD = 0.01
ADAM_STEP = 10

HALO = 32
TS_MIX = 512
TS_PROJ = 1024
VMEM_LIMIT = 56 * 1024 * 1024

C_AU, C_AV, C_AZ, C_BP, C_BZ, C_CH, C_CB, C_CC, C_CZ, C_G0 = range(10)

TN_DIMS = (((0,), (0,)), ((), ()))
NT_DIMS = (((1,), (1,)), ((), ()))


def _cparams(sem):
    return pltpu.CompilerParams(dimension_semantics=sem, vmem_limit_bytes=VMEM_LIMIT)


def _dot(a, b):
    return jnp.dot(a, b, preferred_element_type=F32)


def _gelu(x):
    t = jnp.tanh(GELU_K0 * (x + GELU_K1 * (x * x * x)))
    return 0.5 * x * (1.0 + t), t


def _gelu_grad(x, t):
    return 0.5 * (1.0 + t) + 0.5 * x * (1.0 - t * t) * (GELU_K0 * (1.0 + 3.0 * GELU_K1 * (x * x)))


def _silu(x):
    s = jax.nn.sigmoid(x)
    return x * s, s


def _silu_grad(x, s):
    return s * (1.0 + x * (1.0 - s))


def _rowmean(x):
    return jnp.mean(x, axis=-1, keepdims=True)


def _colsum(x):
    return jnp.sum(x, axis=0, keepdims=True)


def _const_spec(shape):
    nd = len(shape)
    return pl.BlockSpec(shape, lambda *_: (0,) * nd)


def _chunk_spec(ts, chunk, width=1, rev_nt=None):
    if rev_nt is None:
        return pl.BlockSpec((ts, width * D), lambda i: (i, chunk))
    return pl.BlockSpec((ts, width * D), lambda i: (rev_nt - 1 - i, chunk))


def _halo_spec(ts, chunk, rev_nt=None):
    per = ts // HALO
    if rev_nt is None:
        return pl.BlockSpec((HALO, D), lambda i: (jnp.maximum(i * per - 1, 0), chunk))
    return pl.BlockSpec((HALO, D), lambda i: (jnp.maximum((rev_nt - 1 - i) * per - 1, 0), chunk))


def _exchange_copies(scat_ins, scat_outs, gath_ins, gath_outs, send_sems, recv_sems, local_sems):
    x, y, c = lax.axis_index("x"), lax.axis_index("y"), lax.axis_index("c")
    me = 4 * x + 2 * y + c
    arrs = [(i, o, True) for i, o in zip(scat_ins, scat_outs)] + [(i, o, False) for i, o in zip(gath_ins, gath_outs)]
    copies = []
    for k in range(1, NDEV):
        px = 1 - x if k & 4 else x
        py = 1 - y if k & 2 else y
        pc = 1 - c if k & 1 else c
        peer = 4 * px + 2 * py + pc
        for a, (src, dst, scatter) in enumerate(arrs):
            copies.append(pltpu.make_async_remote_copy(
                src_ref=src.at[peer] if scatter else src,
                dst_ref=dst.at[me],
                send_sem=send_sems.at[k - 1, a],
                recv_sem=recv_sems.at[k - 1, a],
                device_id=(px, py, pc),
                device_id_type=pl.DeviceIdType.MESH,
            ))
    for a, (src, dst, scatter) in enumerate(arrs):
        copies.append(pltpu.make_async_copy(src.at[me] if scatter else src, dst.at[me], local_sems.at[a]))
    return copies


def _exchange_shapes(scat, gath):
    return ([jax.ShapeDtypeStruct(a.shape, a.dtype) for a in scat]
            + [jax.ShapeDtypeStruct((NDEV,) + a.shape, a.dtype) for a in gath])


def _exchange_sems(n):
    return [pltpu.SemaphoreType.DMA((NDEV - 1, n)), pltpu.SemaphoreType.DMA((NDEV - 1, n)),
            pltpu.SemaphoreType.DMA((n,))]


HBM_SPEC = pl.BlockSpec(memory_space=pltpu.HBM)


def _exchange(scat, gath, name):
    ns, n = len(scat), len(scat) + len(gath)

    def body(*refs):
        ins, outs, sems = refs[:n], refs[n:2 * n], refs[2 * n:]
        copies = _exchange_copies(ins[:ns], outs[:ns], ins[ns:], outs[ns:], *sems)
        for cp in copies:
            cp.start()
        for cp in copies:
            cp.wait()

    return pl.pallas_call(
        body,
        name=name,
        out_shape=_exchange_shapes(scat, gath),
        in_specs=[HBM_SPEC] * n,
        out_specs=[HBM_SPEC] * n,
        scratch_shapes=_exchange_sems(n),
    )(*scat, *gath)


def _gather_two_level(shard, name):
    def body(x_ref, out_ref, send_sems, recv_sems, local_sem):
        x, y, c = lax.axis_index("x"), lax.axis_index("y"), lax.axis_index("c")
        me, sibling = (x, y, c), (x, y, 1 - c)
        chips = [(1 - x, y), (x, 1 - y), (1 - x, 1 - y)]

        def slot(px, py, pc):
            return out_ref.at[4 * px + 2 * py + pc]

        def copy(k, block, to, src=None):
            return pltpu.make_async_remote_copy(
                src_ref=slot(*block) if src is None else src,
                dst_ref=slot(*block),
                send_sem=send_sems.at[k],
                recv_sem=recv_sems.at[k],
                device_id=to,
                device_id_type=pl.DeviceIdType.MESH,
            )

        mine = pltpu.make_async_copy(x_ref, slot(*me), local_sem)
        mine.start()
        first = [copy(0, me, sibling, src=x_ref)]
        first += [copy(1 + j, me, (*chip, c), src=x_ref) for j, chip in enumerate(chips)]
        for cp in first:
            cp.start()
        passed = [copy(4 + j, (*chip, c), sibling) for j, chip in enumerate(chips)]
        for j, chip in enumerate(chips):
            copy(1 + j, (*chip, c), me).wait_recv()
            passed[j].start()
        copy(0, sibling, me).wait_recv()
        for j, chip in enumerate(chips):
            copy(4 + j, (*chip, 1 - c), me).wait_recv()
        for cp in first + passed:
            cp.wait_send()
        mine.wait()

    return pl.pallas_call(
        body,
        name=name,
        out_shape=jax.ShapeDtypeStruct((NDEV,) + shard.shape, shard.dtype),
        in_specs=[HBM_SPEC],
        out_specs=HBM_SPEC,
        scratch_shapes=[pltpu.SemaphoreType.DMA((NDEV - 1,)), pltpu.SemaphoreType.DMA((NDEV - 1,)),
                        pltpu.SemaphoreType.DMA],
    )(shard)


def _fwd_proj(x, g_row, w_gathered, next_shards=()):
    s = x.shape[0]
    ts = min(TS_PROJ, s)
    ng = len(next_shards)
    nt = s // ts

    def body(*refs):
        x_ref, g_ref, w_ref = refs[:3]
        gath_ins = refs[3:3 + ng]
        proj_ref, ht_ref = refs[3 + ng:5 + ng]
        gath_outs = refs[5 + ng:5 + 2 * ng]
        h_scr = refs[5 + 2 * ng]
        sems = refs[6 + 2 * ng:]
        i, j = pl.program_id(0), pl.program_id(1)

        if ng:
            @pl.when((i == 0) & (j == 0))
            def _():
                for cp in _exchange_copies((), (), gath_ins, gath_outs, *sems):
                    cp.start()

        @pl.when(j == 0)
        def _():
            xf = x_ref[...]
            h = xf * lax.rsqrt(_rowmean(xf * xf) + RMS_EPS) * g_ref[...]
            h_scr[...] = h.astype(CDT)
            ht_ref[...] = h.T.astype(CDT)

        proj_ref[...] = _dot(h_scr[...], w_ref[...]).astype(CDT)

        if ng:
            @pl.when((i == nt - 1) & (j == NDEV - 1))
            def _():
                for cp in _exchange_copies((), (), gath_ins, gath_outs, *sems):
                    cp.wait()

    return pl.pallas_call(
        body,
        name="fwd_proj_gather" if ng else "fwd_proj",
        grid=(nt, NDEV),
        in_specs=[
            pl.BlockSpec((ts, D), lambda i, j: (i, 0)),
            pl.BlockSpec((1, D), lambda i, j: (0, 0)),
            pl.BlockSpec((None, D, SHARD_N), lambda i, j: (j, 0, 0)),
        ] + [HBM_SPEC] * ng,
        out_specs=[
            pl.BlockSpec((ts, SHARD_N), lambda i, j: (i, j)),
            pl.BlockSpec((D, ts), lambda i, j: (0, i)),
        ] + [HBM_SPEC] * ng,
        out_shape=[jax.ShapeDtypeStruct((s, N_IN), CDT), jax.ShapeDtypeStruct((D, s), CDT)]
        + _exchange_shapes((), next_shards),
        scratch_shapes=[pltpu.VMEM((ts, D), CDT)] + (_exchange_sems(ng) if ng else []),
        compiler_params=_cparams(("arbitrary", "arbitrary")),
    )(x, g_row, w_gathered, *next_shards)


def _a_forward(au, av, az, lng, lnb, wm_ref, bias_ref, v_scr, mix_scr):
    ts = au.shape[0]
    u, tu = _gelu(au)
    vg, tv = _gelu(av)
    xc = vg - _rowmean(vg)
    rstd = lax.rsqrt(_rowmean(xc * xc) + LN_EPS)
    vhat = xc * rstd
    v_scr[...] = (vhat * lng + lnb).astype(CDT)
    for n in range(ts // SGU_BLOCK):
        rows = slice(n * SGU_BLOCK, (n + 1) * SGU_BLOCK)
        for g in range(SGU_GROUPS):
            cols = slice(g * SGU_BLOCK, (g + 1) * SGU_BLOCK)
            mix_scr[rows, cols] = _dot(wm_ref[g], v_scr[rows, cols]) + bias_ref[:, cols]
    mixed = mix_scr[...]
    sz, sg = _silu(az)
    return dict(u=u, tu=tu, tv=tv, rstd=rstd, vhat=vhat, mixed=mixed, sz=sz, sg=sg, ya=u * mixed * sz)


def _fwd_a(proj, lng, lnb, wm, bias_full, wa):
    s = proj.shape[0]
    ts = min(TS_MIX, s)

    def body(au_ref, av_ref, az_ref, lng_ref, lnb_ref, wm_ref, bias_ref, wa_ref, pa_ref, v_scr, mix_scr):
        f = _a_forward(au_ref[...].astype(F32), av_ref[...].astype(F32), az_ref[...].astype(F32),
                       lng_ref[...], lnb_ref[...], wm_ref, bias_ref, v_scr, mix_scr)
        pa_ref[...] = _dot(f["ya"].astype(CDT), wa_ref[...]).astype(CDT)

    return pl.pallas_call(
        body,
        name="fwd_a",
        grid=(s // ts,),
        in_specs=[
            _chunk_spec(ts, C_AU), _chunk_spec(ts, C_AV), _chunk_spec(ts, C_AZ),
            _const_spec((1, D)), _const_spec((1, D)),
            _const_spec((SGU_GROUPS, SGU_BLOCK, SGU_BLOCK)), _const_spec((SGU_BLOCK, D)),
            _const_spec((D, D)),
        ],
        out_specs=pl.BlockSpec((ts, D), lambda i: (i, 0)),
        out_shape=jax.ShapeDtypeStruct((s, D), CDT),
        scratch_shapes=[pltpu.VMEM((ts, D), CDT), pltpu.VMEM((ts, D), F32)],
        compiler_params=_cparams(("arbitrary",)),
    )(proj, proj, proj, lng, lnb, wm, bias_full, wa)


def _pool_d(pe, sa, sb, tile, ts):
    r = HALO + ts
    g = POOL_GDIM
    sa[8:r, :] = pe[8:r, :] + pe[7:r - 1, :]
    sb[16:r, g:] = sa[16:r, g:] + sa[14:r - 2, g:]
    sa[24:r, 2 * g:] = sb[24:r, 2 * g:] + sb[20:r - 4, 2 * g:]
    sb[32:r, 3 * g:] = sa[32:r, 3 * g:] + sa[24:r - 8, 3 * g:]
    pos1 = tile * ts + lax.broadcasted_iota(jnp.int32, (ts, 1), 0) + 1
    invs = [1.0 / jnp.minimum(pos1, w).astype(F32) for w in POOL_WINDOWS]
    sums = [sa[HALO:r, 0:g], sb[HALO:r, g:2 * g], sa[HALO:r, 2 * g:3 * g], sb[HALO:r, 3 * g:]]
    d = [sums[k] * invs[k] - pe[HALO:r, k * g:(k + 1) * g] for k in range(POOL_GROUPS)]
    return d, invs


def _b_forward(bp_ref, bph_ref, bz, pw_ref, pb, pscale, tile, pe, sa, sb, d_scr, lin_scr):
    ts = bz.shape[0]
    pe[0:HALO, :] = jnp.where(tile > 0, bph_ref[...].astype(F32), 0.0)
    pe[HALO:, :] = bp_ref[...].astype(F32)
    d, invs = _pool_d(pe, sa, sb, tile, ts)
    for k in range(POOL_GROUPS):
        cols = slice(k * POOL_GDIM, (k + 1) * POOL_GDIM)
        d_scr[:, cols] = d[k].astype(CDT)
        lin_scr[:, cols] = _dot(d_scr[:, cols], pw_ref[k])
    lin = lin_scr[...] + pb
    sz, sg = _silu(bz)
    return dict(lin=lin, sz=sz, sg=sg, invs=invs, yb=lin * pscale * sz)


def _fwd_b(proj, pool_w, pool_b, pool_scale, wb):
    s = proj.shape[0]
    ts = min(TS_MIX, s)

    def body(bp_ref, bph_ref, bz_ref, pw_ref, pb_ref, ps_ref, wb_ref, out_ref, pe, sa, sb, d_scr, lin_scr):
        f = _b_forward(bp_ref, bph_ref, bz_ref[...].astype(F32), pw_ref, pb_ref[...], ps_ref[...],
                       pl.program_id(0), pe, sa, sb, d_scr, lin_scr)
        out_ref[...] = _dot(f["yb"].astype(CDT), wb_ref[...]).astype(CDT)

    ext = pltpu.VMEM((HALO + ts, D), F32)
    return pl.pallas_call(
        body,
        name="fwd_b",
        grid=(s // ts,),
        in_specs=[
            _chunk_spec(ts, C_BP), _halo_spec(ts, C_BP), _chunk_spec(ts, C_BZ),
            _const_spec((POOL_GROUPS, POOL_GDIM, POOL_GDIM)), _const_spec((1, D)), _const_spec((1, D)),
            _const_spec((D, D)),
        ],
        out_specs=pl.BlockSpec((ts, D), lambda i: (i, 0)),
        out_shape=jax.ShapeDtypeStruct((s, D), CDT),
        scratch_shapes=[ext, ext, ext, pltpu.VMEM((ts, D), CDT), pltpu.VMEM((ts, D), F32)],
        compiler_params=_cparams(("arbitrary",)),
    )(proj, proj, proj, pool_w, pool_b, pool_scale, wb)


def _c_forward(ch_ref, chh_ref, cb, cc_ref, cch_ref, cz, cw_ref, cbias, tile, qe):
    ts = cb.shape[0]
    r = HALO + ts
    qe[0:HALO, :] = jnp.where(tile > 0, chh_ref[...].astype(F32) * cch_ref[...].astype(F32), 0.0)
    qe[HALO:, :] = cc_ref[...].astype(F32) * ch_ref[...].astype(F32)
    conv = (cw_ref[0:1, :] * qe[HALO - 2:r - 2, :] + cw_ref[1:2, :] * qe[HALO - 1:r - 1, :]
            + cw_ref[2:3, :] * qe[HALO:r, :]) + cbias
    sz, sg = _silu(cz)
    return dict(conv=conv, sz=sz, sg=sg, yc=cb * conv * sz)


def _fwd_c(proj, conv_w, conv_b, wc):
    s = proj.shape[0]
    ts = min(TS_MIX, s)

    def body(ch_ref, chh_ref, cb_ref, cc_ref, cch_ref, cz_ref, cw_ref, cbias_ref, wc_ref, out_ref, qe):
        f = _c_forward(ch_ref, chh_ref, cb_ref[...].astype(F32), cc_ref, cch_ref, cz_ref[...].astype(F32),
                       cw_ref, cbias_ref[...], pl.program_id(0), qe)
        out_ref[...] = _dot(f["yc"].astype(CDT), wc_ref[...]).astype(CDT)

    return pl.pallas_call(
        body,
        name="fwd_c",
        grid=(s // ts,),
        in_specs=[
            _chunk_spec(ts, C_CH), _halo_spec(ts, C_CH), _chunk_spec(ts, C_CB),
            _chunk_spec(ts, C_CC), _halo_spec(ts, C_CC), _chunk_spec(ts, C_CZ),
            _const_spec((3, D)), _const_spec((1, D)), _const_spec((D, D)),
        ],
        out_specs=pl.BlockSpec((ts, D), lambda i: (i, 0)),
        out_shape=jax.ShapeDtypeStruct((s, D), CDT),
        scratch_shapes=[pltpu.VMEM((HALO + ts, D), F32)],
        compiler_params=_cparams(("arbitrary",)),
    )(proj, proj, proj, proj, proj, proj, conv_w, conv_b, wc)


def _merge(pa, pb, pc, logits):
    gates = [jax.nn.sigmoid(logits[:, k * D:(k + 1) * D].astype(F32)) for k in range(3)]
    ps = [pa.astype(F32), pb.astype(F32), pc.astype(F32)]
    merged = gates[0] * ps[0] + gates[1] * ps[1] + gates[2] * ps[2]
    return gates, ps, merged


def _fwd_o(x, pa, pb, pc, proj, wo):
    s = x.shape[0]
    ts = min(TS_MIX, s)

    def body(x_ref, pa_ref, pb_ref, pc_ref, lg_ref, wo_ref, out_ref):
        _, _, merged = _merge(pa_ref[...], pb_ref[...], pc_ref[...], lg_ref[...])
        out_ref[...] = x_ref[...] + _dot(merged.astype(CDT), wo_ref[...])

    tile = pl.BlockSpec((ts, D), lambda i: (i, 0))
    return pl.pallas_call(
        body,
        name="fwd_o",
        grid=(s // ts,),
        in_specs=[tile, tile, tile, tile, _chunk_spec(ts, 3, width=3), _const_spec((D, D))],
        out_specs=tile,
        out_shape=jax.ShapeDtypeStruct((s, D), F32),
        compiler_params=_cparams(("arbitrary",)),
    )(x, pa, pb, pc, proj, wo)


def _loss_head(x, target, fg_row):
    s = x.shape[0]
    ts = min(TS_MIX, s)

    def body(x_ref, t_ref, g_ref, dx_ref, loss_ref, dg_ref):
        @pl.when(pl.program_id(0) == 0)
        def _():
            loss_ref[...] = jnp.zeros_like(loss_ref)
            dg_ref[...] = jnp.zeros_like(dg_ref)

        xf = x_ref[...]
        g = g_ref[...]
        r = lax.rsqrt(_rowmean(xf * xf) + RMS_EPS)
        xhat = xf * r
        err = xhat * g - t_ref[...]
        loss_ref[...] += _colsum(err * err)
        dy = err * (1.0 / D)
        dg_ref[...] += _colsum(dy * xhat)
        dxhat = dy * g
        dx_ref[...] = r * (dxhat - xhat * _rowmean(dxhat * xhat))

    tile = pl.BlockSpec((ts, D), lambda i: (i, 0))
    row = _const_spec((1, D))
    return pl.pallas_call(
        body,
        name="loss_head",
        grid=(s // ts,),
        in_specs=[tile, tile, row],
        out_specs=[tile, row, row],
        out_shape=[jax.ShapeDtypeStruct((s, D), F32), jax.ShapeDtypeStruct((1, D), F32),
                   jax.ShapeDtypeStruct((1, D), F32)],
        compiler_params=_cparams(("arbitrary",)),
    )(x, target, fg_row)


def _zero_first(refs):
    @pl.when(pl.program_id(0) == 0)
    def _():
        for r in refs:
            r[...] = jnp.zeros_like(r)


def _bwd_o(dx, pa, pb, pc, proj, wo_t):
    s = dx.shape[0]
    ts = min(TS_MIX, s)

    def body(dx_ref, pa_ref, pb_ref, pc_ref, lg_ref, wot_ref, dp_ref, dg_ref, dwo_ref):
        _zero_first([dwo_ref])
        gates, ps, merged = _merge(pa_ref[...], pb_ref[...], pc_ref[...], lg_ref[...])
        dxb = dx_ref[...].astype(CDT)
        dwo_ref[...] += lax.dot_general(merged.astype(CDT), dxb, TN_DIMS, preferred_element_type=F32)
        dmerged = _dot(dxb, wot_ref[...])
        for k in range(3):
            cols = slice(k * D, (k + 1) * D)
            dpk = dmerged * gates[k]
            dp_ref[:, cols] = dpk.astype(CDT)
            dg_ref[:, cols] = (dpk * ps[k] * (1.0 - gates[k])).astype(CDT)

    tile = pl.BlockSpec((ts, D), lambda i: (i, 0))
    wide = pl.BlockSpec((ts, 3 * D), lambda i: (i, 0))
    return pl.pallas_call(
        body,
        name="bwd_o",
        grid=(s // ts,),
        in_specs=[tile, tile, tile, tile, _chunk_spec(ts, 3, width=3), _const_spec((D, D))],
        out_specs=[wide, wide, _const_spec((D, D))],
        out_shape=[jax.ShapeDtypeStruct((s, 3 * D), CDT), jax.ShapeDtypeStruct((s, 3 * D), CDT),
                   jax.ShapeDtypeStruct((D, D), F32)],
        compiler_params=_cparams(("arbitrary",)),
    )(dx, pa, pb, pc, proj, wo_t)


def _bwd_a(proj, dp, lng, lnb, wm, wm_t, bias_full, wa_t):
    s = proj.shape[0]
    ts = min(TS_MIX, s)
    nt = s // ts

    def body(au_ref, av_ref, az_ref, dpa_ref, lng_ref, lnb_ref, wm_ref, wmt_ref, bias_ref, wat_ref,
             da_ref, dwa_ref, dwm_ref, dbs_ref, dlng_ref, dlnb_ref,
             v_scr, mix_scr, dm_scr, dmf_scr, dv_scr, bsacc):
        _zero_first([dwa_ref, dwm_ref, dlng_ref, dlnb_ref, bsacc])
        au = au_ref[...].astype(F32)
        av = av_ref[...].astype(F32)
        az = az_ref[...].astype(F32)
        lng = lng_ref[...]
        f = _a_forward(au, av, az, lng, lnb_ref[...], wm_ref, bias_ref, v_scr, mix_scr)
        dpa = dpa_ref[...]
        dwa_ref[...] += lax.dot_general(f["ya"].astype(CDT), dpa, TN_DIMS, preferred_element_type=F32)
        dya = _dot(dpa, wat_ref[...])
        t1 = dya * f["mixed"]
        da_ref[:, 0:D] = (t1 * f["sz"] * _gelu_grad(au, f["tu"])).astype(CDT)
        da_ref[:, 2 * D:3 * D] = (t1 * f["u"] * _silu_grad(az, f["sg"])).astype(CDT)
        dmix = dya * f["u"] * f["sz"]
        dmf_scr[...] = dmix
        dm_scr[...] = dmix.astype(CDT)
        for n in range(ts // SGU_BLOCK):
            rows = slice(n * SGU_BLOCK, (n + 1) * SGU_BLOCK)
            for g in range(SGU_GROUPS):
                cols = slice(g * SGU_BLOCK, (g + 1) * SGU_BLOCK)
                dmb = dm_scr[rows, cols]
                dwm_ref[g] += lax.dot_general(dmb, v_scr[rows, cols], NT_DIMS, preferred_element_type=F32)
                dv_scr[rows, cols] = _dot(wmt_ref[g], dmb)
                bsacc[g] += dmf_scr[rows, cols]
        dvln = dv_scr[...]
        vhat = f["vhat"]
        dlng_ref[...] += _colsum(dvln * vhat)
        dlnb_ref[...] += _colsum(dvln)
        dvhat = dvln * lng
        dvg = f["rstd"] * (dvhat - _rowmean(dvhat) - vhat * _rowmean(dvhat * vhat))
        da_ref[:, D:2 * D] = (dvg * _gelu_grad(av, f["tv"])).astype(CDT)

        @pl.when(pl.program_id(0) == nt - 1)
        def _():
            ones = jnp.ones((8, SGU_BLOCK), F32)
            for g in range(SGU_GROUPS):
                red = lax.dot_general(ones, bsacc[g], NT_DIMS, preferred_element_type=F32,
                                      precision=lax.Precision.HIGHEST)
                dbs_ref[g:g + 1, :] = red[0:1, :]

    gshape = (SGU_GROUPS, SGU_BLOCK, SGU_BLOCK)
    return pl.pallas_call(
        body,
        name="bwd_a",
        grid=(nt,),
        in_specs=[
            _chunk_spec(ts, C_AU), _chunk_spec(ts, C_AV), _chunk_spec(ts, C_AZ), _chunk_spec(ts, 0),
            _const_spec((1, D)), _const_spec((1, D)), _const_spec(gshape), _const_spec(gshape),
            _const_spec((SGU_BLOCK, D)), _const_spec((D, D)),
        ],
        out_specs=[
            pl.BlockSpec((ts, 3 * D), lambda i: (i, 0)), _const_spec((D, D)), _const_spec(gshape),
            _const_spec((SGU_GROUPS, SGU_BLOCK)), _const_spec((1, D)), _const_spec((1, D)),
        ],
        out_shape=[
            jax.ShapeDtypeStruct((s, 3 * D), CDT), jax.ShapeDtypeStruct((D, D), F32),
            jax.ShapeDtypeStruct(gshape, F32), jax.ShapeDtypeStruct((SGU_GROUPS, SGU_BLOCK), F32),
            jax.ShapeDtypeStruct((1, D), F32), jax.ShapeDtypeStruct((1, D), F32),
        ],
        scratch_shapes=[
            pltpu.VMEM((ts, D), CDT), pltpu.VMEM((ts, D), F32), pltpu.VMEM((ts, D), CDT),
            pltpu.VMEM((ts, D), F32), pltpu.VMEM((ts, D), F32), pltpu.VMEM(gshape, F32),
        ],
        compiler_params=_cparams(("arbitrary",)),
    )(proj, proj, proj, dp, lng, lnb, wm, wm_t, bias_full, wa_t)


def _bwd_b(proj, dp, pool_w, pool_w_t, pool_b, pool_scale, wb_t):
    s = proj.shape[0]
    ts = min(TS_MIX, s)
    nt = s // ts
    g = POOL_GDIM

    def body(bp_ref, bph_ref, bz_ref, dpb_ref, pw_ref, pwt_ref, pb_ref, ps_ref, wbt_ref,
             db_ref, dwb_ref, dpw_ref, dpb_out_ref, dps_ref,
             pe, sa, sb, d_scr, lin_scr, ee, dl_scr, carry):
        _zero_first([dwb_ref, dpw_ref, dpb_out_ref, dps_ref, carry])
        tile = nt - 1 - pl.program_id(0)
        bz = bz_ref[...].astype(F32)
        pscale = ps_ref[...]
        f = _b_forward(bp_ref, bph_ref, bz, pw_ref, pb_ref[...], pscale, tile, pe, sa, sb, d_scr, lin_scr)
        dpb = dpb_ref[...]
        dwb_ref[...] += lax.dot_general(f["yb"].astype(CDT), dpb, TN_DIMS, preferred_element_type=F32)
        dyb = _dot(dpb, wbt_ref[...])
        t1 = dyb * f["lin"]
        dps_ref[...] += _colsum(t1 * f["sz"])
        db_ref[:, D:2 * D] = (t1 * pscale * _silu_grad(bz, f["sg"])).astype(CDT)
        dlin = dyb * pscale * f["sz"]
        dpb_out_ref[...] += _colsum(dlin)
        dl_scr[...] = dlin.astype(CDT)
        r = HALO + ts
        ee[ts:r, :] = carry[...]
        for k in range(POOL_GROUPS):
            cols = slice(k * g, (k + 1) * g)
            dlk = dl_scr[:, cols]
            dpw_ref[k] += lax.dot_general(d_scr[:, cols], dlk, TN_DIMS, preferred_element_type=F32)
            dd = _dot(dlk, pwt_ref[k])
            lin_scr[:, cols] = dd
            ee[0:ts, cols] = dd * f["invs"][k]
        carry[...] = ee[0:HALO, :]
        sa[0:ts + 24, :] = ee[0:ts + 24, :] + ee[1:ts + 25, :]
        sb[0:ts + 16, g:] = sa[0:ts + 16, g:] + sa[2:ts + 18, g:]
        sa[0:ts + 8, 2 * g:] = sb[0:ts + 8, 2 * g:] + sb[4:ts + 12, 2 * g:]
        sb[0:ts, 3 * g:] = sa[0:ts, 3 * g:] + sa[8:ts + 8, 3 * g:]
        sums = [sa[0:ts, 0:g], sb[0:ts, g:2 * g], sa[0:ts, 2 * g:3 * g], sb[0:ts, 3 * g:]]
        for k in range(POOL_GROUPS):
            cols = slice(k * g, (k + 1) * g)
            db_ref[:, cols] = (sums[k] - lin_scr[:, cols]).astype(CDT)

    ext = pltpu.VMEM((HALO + ts, D), F32)
    wshape = (POOL_GROUPS, g, g)
    return pl.pallas_call(
        body,
        name="bwd_b",
        grid=(nt,),
        in_specs=[
            _chunk_spec(ts, C_BP, rev_nt=nt), _halo_spec(ts, C_BP, rev_nt=nt), _chunk_spec(ts, C_BZ, rev_nt=nt),
            _chunk_spec(ts, 1, rev_nt=nt),
            _const_spec(wshape), _const_spec(wshape), _const_spec((1, D)), _const_spec((1, D)), _const_spec((D, D)),
        ],
        out_specs=[
            pl.BlockSpec((ts, 2 * D), lambda i: (nt - 1 - i, 0)), _const_spec((D, D)), _const_spec(wshape),
            _const_spec((1, D)), _const_spec((1, D)),
        ],
        out_shape=[
            jax.ShapeDtypeStruct((s, 2 * D), CDT), jax.ShapeDtypeStruct((D, D), F32),
            jax.ShapeDtypeStruct(wshape, F32), jax.ShapeDtypeStruct((1, D), F32), jax.ShapeDtypeStruct((1, D), F32),
        ],
        scratch_shapes=[ext, ext, ext, pltpu.VMEM((ts, D), CDT), pltpu.VMEM((ts, D), F32), ext,
                        pltpu.VMEM((ts, D), CDT), pltpu.VMEM((HALO, D), F32)],
        compiler_params=_cparams(("arbitrary",)),
    )(proj, proj, proj, dp, pool_w, pool_w_t, pool_b, pool_scale, wb_t)


def _bwd_c(proj, dp, conv_w, conv_b, wc_t):
    s = proj.shape[0]
    ts = min(TS_MIX, s)
    nt = s // ts

    def body(ch_ref, chh_ref, cb_ref, cc_ref, cch_ref, cz_ref, dpc_ref, cw_ref, cbias_ref, wct_ref,
             dc_ref, dwc_ref, dcw_ref, dcb_ref, qe, de, carry):
        _zero_first([dwc_ref, dcw_ref, dcb_ref, carry])
        tile = nt - 1 - pl.program_id(0)
        cb = cb_ref[...].astype(F32)
        cz = cz_ref[...].astype(F32)
        f = _c_forward(ch_ref, chh_ref, cb, cc_ref, cch_ref, cz, cw_ref, cbias_ref[...], tile, qe)
        dpc = dpc_ref[...]
        dwc_ref[...] += lax.dot_general(f["yc"].astype(CDT), dpc, TN_DIMS, preferred_element_type=F32)
        dyc = _dot(dpc, wct_ref[...])
        t1 = dyc * f["conv"]
        dc_ref[:, D:2 * D] = (t1 * f["sz"]).astype(CDT)
        dc_ref[:, 3 * D:4 * D] = (t1 * cb * _silu_grad(cz, f["sg"])).astype(CDT)
        dconv = dyc * cb * f["sz"]
        r = HALO + ts
        dcb_ref[...] += _colsum(dconv)
        dcw_ref[0:1, :] += _colsum(dconv * qe[HALO - 2:r - 2, :])
        dcw_ref[1:2, :] += _colsum(dconv * qe[HALO - 1:r - 1, :])
        dcw_ref[2:3, :] += _colsum(dconv * qe[HALO:r, :])
        de[0:ts, :] = dconv
        de[ts:ts + 8, :] = carry[...]
        carry[...] = de[0:8, :]
        dq = cw_ref[2:3, :] * dconv + cw_ref[1:2, :] * de[1:ts + 1, :] + cw_ref[0:1, :] * de[2:ts + 2, :]
        dc_ref[:, 0:D] = (dq * cc_ref[...].astype(F32)).astype(CDT)
        dc_ref[:, 2 * D:3 * D] = (dq * ch_ref[...].astype(F32)).astype(CDT)

    return pl.pallas_call(
        body,
        name="bwd_c",
        grid=(nt,),
        in_specs=[
            _chunk_spec(ts, C_CH, rev_nt=nt), _halo_spec(ts, C_CH, rev_nt=nt), _chunk_spec(ts, C_CB, rev_nt=nt),
            _chunk_spec(ts, C_CC, rev_nt=nt), _halo_spec(ts, C_CC, rev_nt=nt), _chunk_spec(ts, C_CZ, rev_nt=nt),
            _chunk_spec(ts, 2, rev_nt=nt),
            _const_spec((3, D)), _const_spec((1, D)), _const_spec((D, D)),
        ],
        out_specs=[
            pl.BlockSpec((ts, 4 * D), lambda i: (nt - 1 - i, 0)), _const_spec((D, D)), _const_spec((8, D)),
            _const_spec((1, D)),
        ],
        out_shape=[
            jax.ShapeDtypeStruct((s, 4 * D), CDT), jax.ShapeDtypeStruct((D, D), F32),
            jax.ShapeDtypeStruct((8, D), F32), jax.ShapeDtypeStruct((1, D), F32),
        ],
        scratch_shapes=[pltpu.VMEM((HALO + ts, D), F32), pltpu.VMEM((ts + 8, D), F32), pltpu.VMEM((8, D), F32)],
        compiler_params=_cparams(("arbitrary",)),
    )(proj, proj, proj, proj, proj, proj, dp, conv_w, conv_b, wc_t)


DPROJ_STARTS = (0, 3, 5, 9)
DPROJ_WIDTHS = (3, 2, 4, 3)


def _bwd_proj_dx(d_arrs, w_in_t, x, g_row, dx_out, scat, gath):
    s = x.shape[0]
    ts = min(TS_PROJ, s)
    nk = N_IN // D
    nt = s // ts
    ns, nx = len(scat), len(scat) + len(gath)

    def body(*refs):
        da_ref, db_ref, dc_ref, dg_ref, wt_ref, x_ref, g_ref, dxo_ref = refs[:8]
        ex_ins = refs[8:8 + nx]
        dxi_ref, dng_ref = refs[8 + nx:10 + nx]
        ex_outs = refs[10 + nx:10 + 2 * nx]
        acc = refs[10 + 2 * nx]
        sems = refs[11 + 2 * nx:]
        i, k = pl.program_id(0), pl.program_id(1)

        def exchange():
            return _exchange_copies(ex_ins[:ns], ex_outs[:ns], ex_ins[ns:], ex_outs[ns:], *sems)

        @pl.when((i == 0) & (k == 0))
        def _():
            dng_ref[...] = jnp.zeros_like(dng_ref)
            for cp in exchange():
                cp.start()

        @pl.when(k == 0)
        def _():
            acc[...] = jnp.zeros_like(acc)

        for ref, start, width in zip((da_ref, db_ref, dc_ref, dg_ref), DPROJ_STARTS, DPROJ_WIDTHS):
            @pl.when((k >= start) & (k < start + width))
            def _(ref=ref):
                acc[...] += _dot(ref[...], wt_ref[...])

        @pl.when(k == nk - 1)
        def _():
            xf = x_ref[...]
            g = g_ref[...]
            dh = acc[...]
            r = lax.rsqrt(_rowmean(xf * xf) + RMS_EPS)
            xhat = xf * r
            dng_ref[...] += _colsum(dh * xhat)
            dxhat = dh * g
            dxi_ref[...] = dxo_ref[...] + r * (dxhat - xhat * _rowmean(dxhat * xhat))

        @pl.when((i == nt - 1) & (k == nk - 1))
        def _():
            for cp in exchange():
                cp.wait()

    def dspec(start, width):
        return pl.BlockSpec((ts, D), lambda i, k: (i, jnp.clip(k - start, 0, width - 1)))

    tile = pl.BlockSpec((ts, D), lambda i, k: (i, 0))
    row = pl.BlockSpec((1, D), lambda i, k: (0, 0))
    return pl.pallas_call(
        body,
        name="bwd_proj_dx_exchange",
        grid=(nt, nk),
        in_specs=[dspec(st, w) for st, w in zip(DPROJ_STARTS, DPROJ_WIDTHS)] + [
            pl.BlockSpec((D, D), lambda i, k: (k, 0)), tile, row, tile,
        ] + [HBM_SPEC] * nx,
        out_specs=[tile, row] + [HBM_SPEC] * nx,
        out_shape=[jax.ShapeDtypeStruct((s, D), F32), jax.ShapeDtypeStruct((1, D), F32)]
        + _exchange_shapes(scat, gath),
        scratch_shapes=[pltpu.VMEM((ts, D), F32)] + _exchange_sems(nx),
        compiler_params=_cparams(("arbitrary", "arbitrary")),
    )(*d_arrs, w_in_t, x, g_row, dx_out, *scat, *gath)


def _bwd_proj_dw(h_t, d_arrs):
    s = h_t.shape[1]
    tk = min(2048, s)
    tn = 512
    per_shard = SHARD_N // tn
    starts = [st * D // tn for st in DPROJ_STARTS]
    widths = [w * D // tn for w in DPROJ_WIDTHS]
    nk = s // tk

    def body(ht_ref, da_ref, db_ref, dc_ref, dg_ref, out_ref, acc):
        n, k = pl.program_id(0), pl.program_id(1)

        @pl.when(k == 0)
        def _():
            acc[...] = jnp.zeros_like(acc)

        for ref, start, width in zip((da_ref, db_ref, dc_ref, dg_ref), starts, widths):
            @pl.when((n >= start) & (n < start + width))
            def _(ref=ref):
                acc[...] += _dot(ht_ref[...], ref[...])

        @pl.when(k == nk - 1)
        def _():
            out_ref[...] = acc[...].astype(CDT)

    def dspec(start, width):
        def index(n, k):
            active = (n >= start) & (n < start + width)
            return (jnp.where(active, k, 0), jnp.clip(n - start, 0, width - 1))
        return pl.BlockSpec((tk, tn), index)

    return pl.pallas_call(
        body,
        name="bwd_proj_dw",
        grid=(N_IN // tn, nk),
        in_specs=[pl.BlockSpec((D, tk), lambda n, k: (0, k))] + [dspec(st, w) for st, w in zip(starts, widths)],
        out_specs=pl.BlockSpec((None, D, tn), lambda n, k: (n // per_shard, 0, n % per_shard)),
        out_shape=jax.ShapeDtypeStruct((NDEV, D, SHARD_N), CDT),
        scratch_shapes=[pltpu.VMEM((D, tn), F32)],
        compiler_params=_cparams(("arbitrary", "arbitrary")),
    )(h_t, *d_arrs)


def _adamw(recvs, w, m, v, tr, name):
    nq = len(recvs)
    _, rows, c = recvs[0].shape
    assert rows % tr == 0 and w.shape == (nq * rows, c)
    steps = rows // tr

    def body(*refs):
        rrefs = refs[:nq]
        w_ref, m_ref, v_ref, g_out, d_out, m_out, v_out = refs[nq:]
        q = pl.program_id(0)
        for qi in range(nq):
            @pl.when(q == qi)
            def _(r=rrefs[qi]):
                g = r[0].astype(F32)
                for j in range(1, NDEV):
                    g = g + r[j].astype(F32)
                wv = w_ref[...]
                m2 = ADAM_B1 * m_ref[...] + (1.0 - ADAM_B1) * g
                v2 = ADAM_B2 * v_ref[...] + (1.0 - ADAM_B2) * (g * g)
                m_hat = m2 / (1.0 - ADAM_B1 ** ADAM_STEP)
                v_hat = v2 / (1.0 - ADAM_B2 ** ADAM_STEP)
                g_out[...] = g
                d_out[...] = -ADAM_LR * (m_hat / (jnp.sqrt(v_hat) + ADAM_EPS) + ADAM_WD * wv)
                m_out[...] = m2
                v_out[...] = v2

    def rspec(qi):
        return pl.BlockSpec((NDEV, tr, c), lambda q, t: (0, jnp.where(q == qi, t, 0), 0))

    tile = pl.BlockSpec((tr, c), lambda q, t: (q * steps + t, 0))
    shp = jax.ShapeDtypeStruct(w.shape, F32)
    return pl.pallas_call(
        body,
        name=name,
        grid=(nq, steps),
        in_specs=[rspec(qi) for qi in range(nq)] + [tile, tile, tile],
        out_specs=[tile] * 4,
        out_shape=[shp] * 4,
        compiler_params=_cparams(("arbitrary", "arbitrary")),
    )(*recvs, w, m, v)


SMALL_ROWS = 264
REP_NAMES = ("sgu_ln_g", "sgu_ln_b", "sgu_b", "pool_b", "pool_scale", "conv_b")


def _pack_small(pool_part, conv_part):
    n = pool_part.shape[0]
    flat = jnp.concatenate([pool_part.reshape(n, -1), conv_part.reshape(n, -1)], axis=1)
    return jnp.pad(flat, ((0, 0), (0, SMALL_ROWS * 128 - flat.shape[1]))).reshape(n, SMALL_ROWS, 128)


def _unpack_small(packed):
    flat = packed.reshape(DEPTH, SMALL_ROWS * 128)
    n_pool = POOL_GROUPS * (POOL_GDIM // NDEV) * POOL_GDIM
    pool = flat[:, :n_pool].reshape(DEPTH, POOL_GROUPS, POOL_GDIM // NDEV, POOL_GDIM)
    conv = flat[:, n_pool:n_pool + 3 * (D // NDEV)].reshape(DEPTH, 3, D // NDEV)
    return pool, conv


REP_LAYER_ROWS = len(REP_NAMES) * 8 + SGU_GROUPS * SGU_BLOCK
NORM_ROWS = 8 * (DEPTH + 1)


def _pack_rep_layer(vals, sgu_w_l):
    parts = [vals[nme].reshape(8, 128) for nme in REP_NAMES]
    parts.append(sgu_w_l.reshape(SGU_GROUPS * SGU_BLOCK, SGU_BLOCK))
    return jnp.concatenate(parts, axis=0)


def _pack_rep(per_layer, sgu_w):
    vecs = jnp.stack([per_layer[nme].reshape(DEPTH, 8, 128) for nme in REP_NAMES], axis=1)
    vecs = vecs.reshape(DEPTH, 8 * len(REP_NAMES), 128)
    mats = sgu_w.reshape(DEPTH, SGU_GROUPS * SGU_BLOCK, SGU_BLOCK)
    return jnp.concatenate([vecs, mats], axis=1).reshape(DEPTH * REP_LAYER_ROWS, 128)


def _unpack_rep(packed):
    p = packed.reshape(DEPTH, REP_LAYER_ROWS, 128)
    nv = 8 * len(REP_NAMES)
    vecs = p[:, :nv].reshape(DEPTH, len(REP_NAMES), D)
    out = {nme: vecs[:, q] for q, nme in enumerate(REP_NAMES)}
    out["sgu_b"] = out["sgu_b"].reshape(DEPTH, SGU_GROUPS, SGU_BLOCK)
    return out, p[:, nv:].reshape(DEPTH, SGU_GROUPS, SGU_BLOCK, SGU_BLOCK)


def _pack_norm(norm_g_like, final_g_like):
    return jnp.concatenate([norm_g_like.reshape(DEPTH * 8, 128), final_g_like.reshape(8, 128)], axis=0)


def _unpack_norm(packed):
    return packed[:DEPTH * 8].reshape(DEPTH, D), packed[DEPTH * 8:].reshape(D)


def kernel(x, norm_g, w_in, sgu_ln_g, sgu_ln_b, sgu_w, sgu_b, pool_w, pool_b, pool_scale, conv_w, conv_b, w_branch_a, w_branch_b, w_branch_c, w_out, final_g, loss_target, m_norm_g, m_w_in, m_sgu_ln_g, m_sgu_ln_b, m_sgu_w, m_sgu_b, m_pool_w, m_pool_b, m_pool_scale, m_conv_w, m_conv_b, m_w_branch_a, m_w_branch_b, m_w_branch_c, m_w_out, m_final_g, v_norm_g, v_w_in, v_sgu_ln_g, v_sgu_ln_b, v_sgu_w, v_sgu_b, v_pool_w, v_pool_b, v_pool_scale, v_conv_w, v_conv_b, v_w_branch_a, v_w_branch_b, v_w_branch_c, v_w_out, v_final_g):
    assert x.shape[0] == 1 and x.shape[2] == D
    s = x.shape[1]
    x0 = x.reshape(s, D)
    target = loss_target.reshape(s, D)

    w_br = jnp.stack([w_branch_a, w_branch_b, w_branch_c, w_out], axis=1)
    shards = [[w_in[l].astype(CDT), w_br[l].astype(CDT), pool_w[l].astype(CDT), conv_w[l]] for l in range(DEPTH)]

    def layout(gathered):
        g_in, g_br, g_pool, g_conv = gathered
        mats = jnp.transpose(g_br, (1, 0, 2, 3)).reshape(4, D, D)
        pool_full = jnp.transpose(g_pool, (1, 0, 2, 3)).reshape(POOL_GROUPS, POOL_GDIM, POOL_GDIM)
        return dict(
            g_in=g_in,
            w_in_t=jnp.transpose(g_in, (0, 2, 1)).reshape(N_IN, D),
            mats=mats, mats_t=jnp.transpose(mats, (0, 2, 1)),
            pool=pool_full, pool_t=jnp.transpose(pool_full, (0, 2, 1)),
            conv=jnp.transpose(g_conv, (1, 0, 2)).reshape(3, D),
        )

    pos = np.arange(SGU_BLOCK) // CHUNK
    mask = jnp.asarray(pos[None, :] <= pos[:, None])
    wm_all = jnp.where(mask[None, None], sgu_w, 0.0).astype(CDT)
    wm_all_t = jnp.transpose(wm_all, (0, 1, 3, 2))
    bias_all = jnp.repeat(jnp.transpose(sgu_b, (0, 2, 1)), SGU_BLOCK, axis=2)

    def row(a, l):
        return a[l].reshape(1, D)

    g_in0 = _gather_two_level(shards[0][0], "gather_w_in0")
    weights = [None] * DEPTH
    saved = []
    xl = x0
    for l in range(DEPTH):
        g_in = g_in0 if l == 0 else weights[l]["g_in"]
        behind = (shards[0][1:] if l == 0 else []) + (shards[l + 1] if l + 1 < DEPTH else [])
        proj, h_t, *got = _fwd_proj(xl, row(norm_g, l), g_in, behind)
        if l == 0:
            weights[0] = layout([g_in0] + got[:3])
            got = got[3:]
        if l + 1 < DEPTH:
            weights[l + 1] = layout(got)
        w = weights[l]
        pa = _fwd_a(proj, row(sgu_ln_g, l), row(sgu_ln_b, l), wm_all[l], bias_all[l], w["mats"][0])
        pb = _fwd_b(proj, w["pool"], row(pool_b, l), row(pool_scale, l), w["mats"][1])
        pc = _fwd_c(proj, w["conv"], row(conv_b, l), w["mats"][2])
        saved.append((xl, proj, h_t, pa, pb, pc))
        xl = _fwd_o(xl, pa, pb, pc, proj, w["mats"][3])

    dx, loss_cols, dfinal_g = _loss_head(xl, target, final_g.reshape(1, D))
    loss = lax.psum(0.5 / D * jnp.sum(loss_cols), ("x", "y", "c"))

    norm_grads = [None] * DEPTH
    recv_in, recv_br, recv_small, recv_rep = ([None] * DEPTH for _ in range(4))
    for l in reversed(range(DEPTH)):
        w = weights[l]
        xl, proj, h_t, pa, pb, pc = saved[l]
        dp, d_g, dwo = _bwd_o(dx, pa, pb, pc, proj, w["mats_t"][3])
        d_a, dwa, dwm, dbs, dlng, dlnb = _bwd_a(
            proj, dp, row(sgu_ln_g, l), row(sgu_ln_b, l), wm_all[l], wm_all_t[l], bias_all[l], w["mats_t"][0])
        d_b, dwb, dpw, dpb, dps = _bwd_b(
            proj, dp, w["pool"], w["pool_t"], row(pool_b, l), row(pool_scale, l), w["mats_t"][1])
        d_c, dwc, dcw, dcb = _bwd_c(proj, dp, w["conv"], row(conv_b, l), w["mats_t"][2])
        d_arrs = (d_a, d_b, d_c, d_g)
        dw_in = _bwd_proj_dw(h_t, d_arrs)

        dw_br = jnp.stack([dwa, dwb, dwc, dwo]).astype(CDT).reshape(4, NDEV, D // NDEV, D).transpose(1, 0, 2, 3)
        dpw_j = dpw.reshape(POOL_GROUPS, NDEV, POOL_GDIM // NDEV, POOL_GDIM).transpose(1, 0, 2, 3)
        dcw_j = dcw[0:3].reshape(3, NDEV, D // NDEV).transpose(1, 0, 2)
        small = _pack_small(dpw_j, dcw_j)
        rep = _pack_rep_layer(
            dict(sgu_ln_g=dlng, sgu_ln_b=dlnb, sgu_b=dbs, pool_b=dpb, pool_scale=dps, conv_b=dcb),
            jnp.where(mask[None], dwm, 0.0))
        dx, norm_grads[l], recv_in[l], recv_br[l], recv_small[l], recv_rep[l] = _bwd_proj_dx(
            d_arrs, w["w_in_t"], xl, row(norm_g, l), dx, [dw_in, dw_br, small], [rep])

    (recv_norm,) = _exchange([], [_pack_norm(jnp.stack(norm_grads), dfinal_g)], "gather_norm_grads")

    rep_w = dict(sgu_ln_g=sgu_ln_g, sgu_ln_b=sgu_ln_b, sgu_b=sgu_b, pool_b=pool_b, pool_scale=pool_scale, conv_b=conv_b)
    rep_m = dict(sgu_ln_g=m_sgu_ln_g, sgu_ln_b=m_sgu_ln_b, sgu_b=m_sgu_b, pool_b=m_pool_b, pool_scale=m_pool_scale,
                 conv_b=m_conv_b)
    rep_v = dict(sgu_ln_g=v_sgu_ln_g, sgu_ln_b=v_sgu_ln_b, sgu_b=v_sgu_b, pool_b=v_pool_b, pool_scale=v_pool_scale,
                 conv_b=v_conv_b)
    rep_out = _adamw(recv_rep, _pack_rep(rep_w, sgu_w), _pack_rep(rep_m, m_sgu_w), _pack_rep(rep_v, v_sgu_w),
                     REP_LAYER_ROWS // 2, "adamw_replicated")
    rep_res = [_unpack_rep(o) for o in rep_out]
    norm_out = _adamw([recv_norm], _pack_norm(norm_g, final_g), _pack_norm(m_norm_g, m_final_g),
                      _pack_norm(v_norm_g, v_final_g), NORM_ROWS, "adamw_norm")
    norm_res = [_unpack_norm(o) for o in norm_out]

    in_out = _adamw(recv_in, w_in.reshape(DEPTH * D, SHARD_N), m_w_in.reshape(DEPTH * D, SHARD_N),
                    v_w_in.reshape(DEPTH * D, SHARD_N), 64, "adamw_w_in")
    in_res = [o.reshape(DEPTH, D, SHARD_N) for o in in_out]

    def stack_br(a, b, c, o):
        return jnp.stack([a, b, c, o], axis=1).reshape(DEPTH * 4 * (D // NDEV), D)

    br_out = _adamw(
        [r.reshape(NDEV, 4 * (D // NDEV), D) for r in recv_br],
        stack_br(w_branch_a, w_branch_b, w_branch_c, w_out),
        stack_br(m_w_branch_a, m_w_branch_b, m_w_branch_c, m_w_out),
        stack_br(v_w_branch_a, v_w_branch_b, v_w_branch_c, v_w_out), 128, "adamw_w_branch")
    br_res = [o.reshape(DEPTH, 4, D // NDEV, D) for o in br_out]

    def stack_small(pw, cw):
        return _pack_small(pw, cw).reshape(DEPTH * SMALL_ROWS, 128)

    small_out = _adamw(recv_small, stack_small(pool_w, conv_w), stack_small(m_pool_w, m_conv_w),
                       stack_small(v_pool_w, v_conv_w), SMALL_ROWS, "adamw_small")
    small_res = [_unpack_small(o) for o in small_out]

    def leaves(kind):
        rep, sgu_w_k = rep_res[kind]
        norm_k, final_k = norm_res[kind]
        br = br_res[kind]
        pool_k, conv_k = small_res[kind]
        return [norm_k, in_res[kind], rep["sgu_ln_g"], rep["sgu_ln_b"], sgu_w_k, rep["sgu_b"], pool_k,
                rep["pool_b"], rep["pool_scale"], conv_k, rep["conv_b"], br[:, 0], br[:, 1], br[:, 2], br[:, 3],
                final_k]

    return (loss, dx.reshape(1, s, D), *leaves(0), *leaves(1), *leaves(2), *leaves(3))
```

```python
import functools

import jax
import jax.numpy as jnp
import numpy as np
from jax import lax
from jax.experimental import pallas as pl
from jax.experimental.pallas import tpu as pltpu

F32 = jnp.float32
CDT = jnp.bfloat16

D = 1024
N_IN = 12 * D
DEPTH = 4
NDEV = 8
SHARD_N = N_IN // NDEV
SGU_BLOCK = 128
SGU_GROUPS = 8
CHUNK = 64
POOL_GROUPS = 4
POOL_GDIM = D // POOL_GROUPS
POOL_WINDOWS = (2, 4, 8, 16)
RMS_EPS = 1e-6
LN_EPS = 1e-5
GELU_K0 = float(np.sqrt(2.0 / np.pi))
GELU_K1 = 0.044715

ADAM_LR = 0.001
ADAM_B1 = 0.9
ADAM_B2 = 0.999
ADAM_EPS = 1e-08
ADAM_WD = 0.01
ADAM_STEP = 10

HALO = 32
TS_MIX = 512
TS_PROJ = 1024
VMEM_LIMIT = 56 * 1024 * 1024

C_AU, C_AV, C_AZ, C_BP, C_BZ, C_CH, C_CB, C_CC, C_CZ, C_G0 = range(10)

TN_DIMS = (((0,), (0,)), ((), ()))
NT_DIMS = (((1,), (1,)), ((), ()))


def _cparams(sem):
    return pltpu.CompilerParams(dimension_semantics=sem, vmem_limit_bytes=VMEM_LIMIT)


def _dot(a, b):
    return jnp.dot(a, b, preferred_element_type=F32)


def _gelu(x):
    t = jnp.tanh(GELU_K0 * (x + GELU_K1 * (x * x * x)))
    return 0.5 * x * (1.0 + t), t


def _gelu_grad(x, t):
    return 0.5 * (1.0 + t) + 0.5 * x * (1.0 - t * t) * (GELU_K0 * (1.0 + 3.0 * GELU_K1 * (x * x)))


def _silu(x):
    s = jax.nn.sigmoid(x)
    return x * s, s


def _silu_grad(x, s):
    return s * (1.0 + x * (1.0 - s))


def _rowmean(x):
    return jnp.mean(x, axis=-1, keepdims=True)


def _colsum(x):
    return jnp.sum(x, axis=0, keepdims=True)


def _const_spec(shape):
    nd = len(shape)
    return pl.BlockSpec(shape, lambda *_: (0,) * nd)


def _chunk_spec(ts, chunk, width=1, rev_nt=None):
    if rev_nt is None:
        return pl.BlockSpec((ts, width * D), lambda i: (i, chunk))
    return pl.BlockSpec((ts, width * D), lambda i: (rev_nt - 1 - i, chunk))


def _halo_spec(ts, chunk, rev_nt=None):
    per = ts // HALO
    if rev_nt is None:
        return pl.BlockSpec((HALO, D), lambda i: (jnp.maximum(i * per - 1, 0), chunk))
    return pl.BlockSpec((HALO, D), lambda i: (jnp.maximum((rev_nt - 1 - i) * per - 1, 0), chunk))


def _exchange_copies(scat_ins, scat_outs, gath_ins, gath_outs, send_sems, recv_sems, local_sems):
    x, y, c = lax.axis_index("x"), lax.axis_index("y"), lax.axis_index("c")
    me = 4 * x + 2 * y + c
    arrs = [(i, o, True) for i, o in zip(scat_ins, scat_outs)] + [(i, o, False) for i, o in zip(gath_ins, gath_outs)]
    copies = []
    for k in range(1, NDEV):
        px = 1 - x if k & 4 else x
        py = 1 - y if k & 2 else y
        pc = 1 - c if k & 1 else c
        peer = 4 * px + 2 * py + pc
        for a, (src, dst, scatter) in enumerate(arrs):
            copies.append(pltpu.make_async_remote_copy(
                src_ref=src.at[peer] if scatter else src,
                dst_ref=dst.at[me],
                send_sem=send_sems.at[k - 1, a],
                recv_sem=recv_sems.at[k - 1, a],
                device_id=(px, py, pc),
                device_id_type=pl.DeviceIdType.MESH,
            ))
    for a, (src, dst, scatter) in enumerate(arrs):
        copies.append(pltpu.make_async_copy(src.at[me] if scatter else src, dst.at[me], local_sems.at[a]))
    return copies


def _exchange_shapes(scat, gath):
    return ([jax.ShapeDtypeStruct(a.shape, a.dtype) for a in scat]
            + [jax.ShapeDtypeStruct((NDEV,) + a.shape, a.dtype) for a in gath])


def _exchange_sems(n):
    return [pltpu.SemaphoreType.DMA((NDEV - 1, n)), pltpu.SemaphoreType.DMA((NDEV - 1, n)),
            pltpu.SemaphoreType.DMA((n,))]


HBM_SPEC = pl.BlockSpec(memory_space=pltpu.HBM)


def _exchange(scat, gath, name):
    ns, n = len(scat), len(scat) + len(gath)

    def body(*refs):
        ins, outs, sems = refs[:n], refs[n:2 * n], refs[2 * n:]
        copies = _exchange_copies(ins[:ns], outs[:ns], ins[ns:], outs[ns:], *sems)
        for cp in copies:
            cp.start()
        for cp in copies:
            cp.wait()

    return pl.pallas_call(
        body,
        name=name,
        out_shape=_exchange_shapes(scat, gath),
        in_specs=[HBM_SPEC] * n,
        out_specs=[HBM_SPEC] * n,
        scratch_shapes=_exchange_sems(n),
    )(*scat, *gath)


def _gather_two_level(shard, name):
    def body(x_ref, out_ref, send_sems, recv_sems, local_sem):
        x, y, c = lax.axis_index("x"), lax.axis_index("y"), lax.axis_index("c")
        me, sibling = (x, y, c), (x, y, 1 - c)
        chips = [(1 - x, y), (x, 1 - y), (1 - x, 1 - y)]

        def slot(px, py, pc):
            return out_ref.at[4 * px + 2 * py + pc]

        def copy(k, block, to, src=None):
            return pltpu.make_async_remote_copy(
                src_ref=slot(*block) if src is None else src,
                dst_ref=slot(*block),
                send_sem=send_sems.at[k],
                recv_sem=recv_sems.at[k],
                device_id=to,
                device_id_type=pl.DeviceIdType.MESH,
            )

        mine = pltpu.make_async_copy(x_ref, slot(*me), local_sem)
        mine.start()
        first = [copy(0, me, sibling, src=x_ref)]
        first += [copy(1 + j, me, (*chip, c), src=x_ref) for j, chip in enumerate(chips)]
        for cp in first:
            cp.start()
        passed = [copy(4 + j, (*chip, c), sibling) for j, chip in enumerate(chips)]
        for j, chip in enumerate(chips):
            copy(1 + j, (*chip, c), me).wait_recv()
            passed[j].start()
        copy(0, sibling, me).wait_recv()
        for j, chip in enumerate(chips):
            copy(4 + j, (*chip, 1 - c), me).wait_recv()
        for cp in first + passed:
            cp.wait_send()
        mine.wait()

    return pl.pallas_call(
        body,
        name=name,
        out_shape=jax.ShapeDtypeStruct((NDEV,) + shard.shape, shard.dtype),
        in_specs=[HBM_SPEC],
        out_specs=HBM_SPEC,
        scratch_shapes=[pltpu.SemaphoreType.DMA((NDEV - 1,)), pltpu.SemaphoreType.DMA((NDEV - 1,)),
                        pltpu.SemaphoreType.DMA],
    )(shard)


def _fwd_proj(x, g_row, w_gathered, next_shards=()):
    s = x.shape[0]
    ts = min(TS_PROJ, s)
    ng = len(next_shards)
    nt = s // ts

    def body(*refs):
        x_ref, g_ref, w_ref = refs[:3]
        gath_ins = refs[3:3 + ng]
        proj_ref, ht_ref = refs[3 + ng:5 + ng]
        gath_outs = refs[5 + ng:5 + 2 * ng]
        h_scr = refs[5 + 2 * ng]
        sems = refs[6 + 2 * ng:]
        i, j = pl.program_id(0), pl.program_id(1)

        if ng:
            @pl.when((i == 0) & (j == 0))
            def _():
                for cp in _exchange_copies((), (), gath_ins, gath_outs, *sems):
                    cp.start()

        @pl.when(j == 0)
        def _():
            xf = x_ref[...]
            h = xf * lax.rsqrt(_rowmean(xf * xf) + RMS_EPS) * g_ref[...]
            h_scr[...] = h.astype(CDT)
            ht_ref[...] = h.T.astype(CDT)

        proj_ref[...] = _dot(h_scr[...], w_ref[...]).astype(CDT)

        if ng:
            @pl.when((i == nt - 1) & (j == NDEV - 1))
            def _():
                for cp in _exchange_copies((), (), gath_ins, gath_outs, *sems):
                    cp.wait()

    return pl.pallas_call(
        body,
        name="fwd_proj_gather" if ng else "fwd_proj",
        grid=(nt, NDEV),
        in_specs=[
            pl.BlockSpec((ts, D), lambda i, j: (i, 0)),
            pl.BlockSpec((1, D), lambda i, j: (0, 0)),
            pl.BlockSpec((None, D, SHARD_N), lambda i, j: (j, 0, 0)),
        ] + [HBM_SPEC] * ng,
        out_specs=[
            pl.BlockSpec((ts, SHARD_N), lambda i, j: (i, j)),
            pl.BlockSpec((D, ts), lambda i, j: (0, i)),
        ] + [HBM_SPEC] * ng,
        out_shape=[jax.ShapeDtypeStruct((s, N_IN), CDT), jax.ShapeDtypeStruct((D, s), CDT)]
        + _exchange_shapes((), next_shards),
        scratch_shapes=[pltpu.VMEM((ts, D), CDT)] + (_exchange_sems(ng) if ng else []),
        compiler_params=_cparams(("arbitrary", "arbitrary")),
    )(x, g_row, w_gathered, *next_shards)


def _a_forward(au, av, az, lng, lnb, wm_ref, bias_ref, v_scr, mix_scr):
    ts = au.shape[0]
    u, tu = _gelu(au)
    vg, tv = _gelu(av)
    xc = vg - _rowmean(vg)
    rstd = lax.rsqrt(_rowmean(xc * xc) + LN_EPS)
    vhat = xc * rstd
    v_scr[...] = (vhat * lng + lnb).astype(CDT)
    for n in range(ts // SGU_BLOCK):
        rows = slice(n * SGU_BLOCK, (n + 1) * SGU_BLOCK)
        for g in range(SGU_GROUPS):
            cols = slice(g * SGU_BLOCK, (g + 1) * SGU_BLOCK)
            mix_scr[rows, cols] = _dot(wm_ref[g], v_scr[rows, cols]) + bias_ref[:, cols]
    mixed = mix_scr[...]
    sz, sg = _silu(az)
    return dict(u=u, tu=tu, tv=tv, rstd=rstd, vhat=vhat, mixed=mixed, sz=sz, sg=sg, ya=u * mixed * sz)


def _fwd_a(proj, lng, lnb, wm, bias_full, wa):
    s = proj.shape[0]
    ts = min(TS_MIX, s)

    def body(au_ref, av_ref, az_ref, lng_ref, lnb_ref, wm_ref, bias_ref, wa_ref, pa_ref, v_scr, mix_scr):
        f = _a_forward(au_ref[...].astype(F32), av_ref[...].astype(F32), az_ref[...].astype(F32),
                       lng_ref[...], lnb_ref[...], wm_ref, bias_ref, v_scr, mix_scr)
        pa_ref[...] = _dot(f["ya"].astype(CDT), wa_ref[...]).astype(CDT)

    return pl.pallas_call(
        body,
        name="fwd_a",
        grid=(s // ts,),
        in_specs=[
            _chunk_spec(ts, C_AU), _chunk_spec(ts, C_AV), _chunk_spec(ts, C_AZ),
            _const_spec((1, D)), _const_spec((1, D)),
            _const_spec((SGU_GROUPS, SGU_BLOCK, SGU_BLOCK)), _const_spec((SGU_BLOCK, D)),
            _const_spec((D, D)),
        ],
        out_specs=pl.BlockSpec((ts, D), lambda i: (i, 0)),
        out_shape=jax.ShapeDtypeStruct((s, D), CDT),
        scratch_shapes=[pltpu.VMEM((ts, D), CDT), pltpu.VMEM((ts, D), F32)],
        compiler_params=_cparams(("arbitrary",)),
    )(proj, proj, proj, lng, lnb, wm, bias_full, wa)


def _pool_d(pe, sa, sb, tile, ts):
    r = HALO + ts
    g = POOL_GDIM
    sa[8:r, :] = pe[8:r, :] + pe[7:r - 1, :]
    sb[16:r, g:] = sa[16:r, g:] + sa[14:r - 2, g:]
    sa[24:r, 2 * g:] = sb[24:r, 2 * g:] + sb[20:r - 4, 2 * g:]
    sb[32:r, 3 * g:] = sa[32:r, 3 * g:] + sa[24:r - 8, 3 * g:]
    pos1 = tile * ts + lax.broadcasted_iota(jnp.int32, (ts, 1), 0) + 1
    invs = [1.0 / jnp.minimum(pos1, w).astype(F32) for w in POOL_WINDOWS]
    sums = [sa[HALO:r, 0:g], sb[HALO:r, g:2 * g], sa[HALO:r, 2 * g:3 * g], sb[HALO:r, 3 * g:]]
    d = [sums[k] * invs[k] - pe[HALO:r, k * g:(k + 1) * g] for k in range(POOL_GROUPS)]
    return d, invs


def _b_forward(bp_ref, bph_ref, bz, pw_ref, pb, pscale, tile, pe, sa, sb, d_scr, lin_scr):
    ts = bz.shape[0]
    pe[0:HALO, :] = jnp.where(tile > 0, bph_ref[...].astype(F32), 0.0)
    pe[HALO:, :] = bp_ref[...].astype(F32)
    d, invs = _pool_d(pe, sa, sb, tile, ts)
    for k in range(POOL_GROUPS):
        cols = slice(k * POOL_GDIM, (k + 1) * POOL_GDIM)
        d_scr[:, cols] = d[k].astype(CDT)
        lin_scr[:, cols] = _dot(d_scr[:, cols], pw_ref[k])
    lin = lin_scr[...] + pb
    sz, sg = _silu(bz)
    return dict(lin=lin, sz=sz, sg=sg, invs=invs, yb=lin * pscale * sz)


def _fwd_b(proj, pool_w, pool_b, pool_scale, wb):
    s = proj.shape[0]
    ts = min(TS_MIX, s)

    def body(bp_ref, bph_ref, bz_ref, pw_ref, pb_ref, ps_ref, wb_ref, out_ref, pe, sa, sb, d_scr, lin_scr):
        f = _b_forward(bp_ref, bph_ref, bz_ref[...].astype(F32), pw_ref, pb_ref[...], ps_ref[...],
                       pl.program_id(0), pe, sa, sb, d_scr, lin_scr)
        out_ref[...] = _dot(f["yb"].astype(CDT), wb_ref[...]).astype(CDT)

    ext = pltpu.VMEM((HALO + ts, D), F32)
    return pl.pallas_call(
        body,
        name="fwd_b",
        grid=(s // ts,),
        in_specs=[
            _chunk_spec(ts, C_BP), _halo_spec(ts, C_BP), _chunk_spec(ts, C_BZ),
            _const_spec((POOL_GROUPS, POOL_GDIM, POOL_GDIM)), _const_spec((1, D)), _const_spec((1, D)),
            _const_spec((D, D)),
        ],
        out_specs=pl.BlockSpec((ts, D), lambda i: (i, 0)),
        out_shape=jax.ShapeDtypeStruct((s, D), CDT),
        scratch_shapes=[ext, ext, ext, pltpu.VMEM((ts, D), CDT), pltpu.VMEM((ts, D), F32)],
        compiler_params=_cparams(("arbitrary",)),
    )(proj, proj, proj, pool_w, pool_b, pool_scale, wb)


def _c_forward(ch_ref, chh_ref, cb, cc_ref, cch_ref, cz, cw_ref, cbias, tile, qe):
    ts = cb.shape[0]
    r = HALO + ts
    qe[0:HALO, :] = jnp.where(tile > 0, chh_ref[...].astype(F32) * cch_ref[...].astype(F32), 0.0)
    qe[HALO:, :] = cc_ref[...].astype(F32) * ch_ref[...].astype(F32)
    conv = (cw_ref[0:1, :] * qe[HALO - 2:r - 2, :] + cw_ref[1:2, :] * qe[HALO - 1:r - 1, :]
            + cw_ref[2:3, :] * qe[HALO:r, :]) + cbias
    sz, sg = _silu(cz)
    return dict(conv=conv, sz=sz, sg=sg, yc=cb * conv * sz)


def _fwd_c(proj, conv_w, conv_b, wc):
    s = proj.shape[0]
    ts = min(TS_MIX, s)

    def body(ch_ref, chh_ref, cb_ref, cc_ref, cch_ref, cz_ref, cw_ref, cbias_ref, wc_ref, out_ref, qe):
        f = _c_forward(ch_ref, chh_ref, cb_ref[...].astype(F32), cc_ref, cch_ref, cz_ref[...].astype(F32),
                       cw_ref, cbias_ref[...], pl.program_id(0), qe)
        out_ref[...] = _dot(f["yc"].astype(CDT), wc_ref[...]).astype(CDT)

    return pl.pallas_call(
        body,
        name="fwd_c",
        grid=(s // ts,),
        in_specs=[
            _chunk_spec(ts, C_CH), _halo_spec(ts, C_CH), _chunk_spec(ts, C_CB),
            _chunk_spec(ts, C_CC), _halo_spec(ts, C_CC), _chunk_spec(ts, C_CZ),
            _const_spec((3, D)), _const_spec((1, D)), _const_spec((D, D)),
        ],
        out_specs=pl.BlockSpec((ts, D), lambda i: (i, 0)),
        out_shape=jax.ShapeDtypeStruct((s, D), CDT),
        scratch_shapes=[pltpu.VMEM((HALO + ts, D), F32)],
        compiler_params=_cparams(("arbitrary",)),
    )(proj, proj, proj, proj, proj, proj, conv_w, conv_b, wc)


def _merge(pa, pb, pc, logits):
    gates = [jax.nn.sigmoid(logits[:, k * D:(k + 1) * D].astype(F32)) for k in range(3)]
    ps = [pa.astype(F32), pb.astype(F32), pc.astype(F32)]
    merged = gates[0] * ps[0] + gates[1] * ps[1] + gates[2] * ps[2]
    return gates, ps, merged


def _fwd_o(x, pa, pb, pc, proj, wo):
    s = x.shape[0]
    ts = min(TS_MIX, s)

    def body(x_ref, pa_ref, pb_ref, pc_ref, lg_ref, wo_ref, out_ref):
        _, _, merged = _merge(pa_ref[...], pb_ref[...], pc_ref[...], lg_ref[...])
        out_ref[...] = x_ref[...] + _dot(merged.astype(CDT), wo_ref[...])

    tile = pl.BlockSpec((ts, D), lambda i: (i, 0))
    return pl.pallas_call(
        body,
        name="fwd_o",
        grid=(s // ts,),
        in_specs=[tile, tile, tile, tile, _chunk_spec(ts, 3, width=3), _const_spec((D, D))],
        out_specs=tile,
        out_shape=jax.ShapeDtypeStruct((s, D), F32),
        compiler_params=_cparams(("arbitrary",)),
    )(x, pa, pb, pc, proj, wo)


def _loss_head(x, target, fg_row):
    s = x.shape[0]
    ts = min(TS_MIX, s)

    def body(x_ref, t_ref, g_ref, dx_ref, loss_ref, dg_ref):
        @pl.when(pl.program_id(0) == 0)
        def _():
            loss_ref[...] = jnp.zeros_like(loss_ref)
            dg_ref[...] = jnp.zeros_like(dg_ref)

        xf = x_ref[...]
        g = g_ref[...]
        r = lax.rsqrt(_rowmean(xf * xf) + RMS_EPS)
        xhat = xf * r
        err = xhat * g - t_ref[...]
        loss_ref[...] += _colsum(err * err)
        dy = err * (1.0 / D)
        dg_ref[...] += _colsum(dy * xhat)
        dxhat = dy * g
        dx_ref[...] = r * (dxhat - xhat * _rowmean(dxhat * xhat))

    tile = pl.BlockSpec((ts, D), lambda i: (i, 0))
    row = _const_spec((1, D))
    return pl.pallas_call(
        body,
        name="loss_head",
        grid=(s // ts,),
        in_specs=[tile, tile, row],
        out_specs=[tile, row, row],
        out_shape=[jax.ShapeDtypeStruct((s, D), F32), jax.ShapeDtypeStruct((1, D), F32),
                   jax.ShapeDtypeStruct((1, D), F32)],
        compiler_params=_cparams(("arbitrary",)),
    )(x, target, fg_row)


def _zero_first(refs):
    @pl.when(pl.program_id(0) == 0)
    def _():
        for r in refs:
            r[...] = jnp.zeros_like(r)


DPROJ_BLOCK = dict(c=(4, 0), b=(2, 2), a=(3, 2), g=(3, 3))
DPROJ_PERM = (5, 6, 7, 8, 3, 4, 0, 1, 2, 9, 10, 11)


def _dproj_spec(ts, part, rev_nt=None):
    width, block = DPROJ_BLOCK[part]
    if rev_nt is None:
        return pl.BlockSpec((ts, width * D), lambda i: (i, block))
    return pl.BlockSpec((ts, width * D), lambda i: (rev_nt - 1 - i, block))


def _bwd_o(dx, pa, pb, pc, proj, wo_t):
    s = dx.shape[0]
    ts = min(TS_MIX, s)

    def body(dx_ref, pa_ref, pb_ref, pc_ref, lg_ref, wot_ref, dp_ref, dg_ref, dwo_ref):
        _zero_first([dwo_ref])
        gates, ps, merged = _merge(pa_ref[...], pb_ref[...], pc_ref[...], lg_ref[...])
        dxb = dx_ref[...].astype(CDT)
        dwo_ref[...] += lax.dot_general(merged.astype(CDT), dxb, TN_DIMS, preferred_element_type=F32)
        dmerged = _dot(dxb, wot_ref[...])
        for k in range(3):
            cols = slice(k * D, (k + 1) * D)
            dpk = dmerged * gates[k]
            dp_ref[:, cols] = dpk.astype(CDT)
            dg_ref[:, cols] = (dpk * ps[k] * (1.0 - gates[k])).astype(CDT)

    tile = pl.BlockSpec((ts, D), lambda i: (i, 0))
    wide = pl.BlockSpec((ts, 3 * D), lambda i: (i, 0))
    return pl.pallas_call(
        body,
        name="bwd_o",
        grid=(s // ts,),
        in_specs=[tile, tile, tile, tile, _chunk_spec(ts, 3, width=3), _const_spec((D, D))],
        out_specs=[wide, _dproj_spec(ts, "g"), _const_spec((D, D))],
        out_shape=[jax.ShapeDtypeStruct((s, 3 * D), CDT), jax.ShapeDtypeStruct((s, N_IN), CDT),
                   jax.ShapeDtypeStruct((D, D), F32)],
        compiler_params=_cparams(("arbitrary",)),
    )(dx, pa, pb, pc, proj, wo_t)


def _bwd_a(proj, dp, lng, lnb, wm, wm_t, bias_full, wa_t, dproj):
    s = proj.shape[0]
    ts = min(TS_MIX, s)
    nt = s // ts

    def body(au_ref, av_ref, az_ref, dpa_ref, lng_ref, lnb_ref, wm_ref, wmt_ref, bias_ref, wat_ref, _dproj_in,
             da_ref, dwa_ref, dwm_ref, dbs_ref, dlng_ref, dlnb_ref,
             v_scr, mix_scr, dm_scr, dmf_scr, dv_scr, bsacc):
        _zero_first([dwa_ref, dwm_ref, dlng_ref, dlnb_ref, bsacc])
        au = au_ref[...].astype(F32)
        av = av_ref[...].astype(F32)
        az = az_ref[...].astype(F32)
        lng = lng_ref[...]
        f = _a_forward(au, av, az, lng, lnb_ref[...], wm_ref, bias_ref, v_scr, mix_scr)
        dpa = dpa_ref[...]
        dwa_ref[...] += lax.dot_general(f["ya"].astype(CDT), dpa, TN_DIMS, preferred_element_type=F32)
        dya = _dot(dpa, wat_ref[...])
        t1 = dya * f["mixed"]
        da_ref[:, 0:D] = (t1 * f["sz"] * _gelu_grad(au, f["tu"])).astype(CDT)
        da_ref[:, 2 * D:3 * D] = (t1 * f["u"] * _silu_grad(az, f["sg"])).astype(CDT)
        dmix = dya * f["u"] * f["sz"]
        dmf_scr[...] = dmix
        dm_scr[...] = dmix.astype(CDT)
        for n in range(ts // SGU_BLOCK):
            rows = slice(n * SGU_BLOCK, (n + 1) * SGU_BLOCK)
            for g in range(SGU_GROUPS):
                cols = slice(g * SGU_BLOCK, (g + 1) * SGU_BLOCK)
                dmb = dm_scr[rows, cols]
                dwm_ref[g] += lax.dot_general(dmb, v_scr[rows, cols], NT_DIMS, preferred_element_type=F32)
                dv_scr[rows, cols] = _dot(wmt_ref[g], dmb)
                bsacc[g] += dmf_scr[rows, cols]
        dvln = dv_scr[...]
        vhat = f["vhat"]
        dlng_ref[...] += _colsum(dvln * vhat)
        dlnb_ref[...] += _colsum(dvln)
        dvhat = dvln * lng
        dvg = f["rstd"] * (dvhat - _rowmean(dvhat) - vhat * _rowmean(dvhat * vhat))
        da_ref[:, D:2 * D] = (dvg * _gelu_grad(av, f["tv"])).astype(CDT)

        @pl.when(pl.program_id(0) == nt - 1)
        def _():
            ones = jnp.ones((8, SGU_BLOCK), F32)
            for g in range(SGU_GROUPS):
                red = lax.dot_general(ones, bsacc[g], NT_DIMS, preferred_element_type=F32,
                                      precision=lax.Precision.HIGHEST)
                dbs_ref[g:g + 1, :] = red[0:1, :]

    gshape = (SGU_GROUPS, SGU_BLOCK, SGU_BLOCK)
    return pl.pallas_call(
        body,
        name="bwd_a",
        grid=(nt,),
        in_specs=[
            _chunk_spec(ts, C_AU), _chunk_spec(ts, C_AV), _chunk_spec(ts, C_AZ), _chunk_spec(ts, 0),
            _const_spec((1, D)), _const_spec((1, D)), _const_spec(gshape), _const_spec(gshape),
            _const_spec((SGU_BLOCK, D)), _const_spec((D, D)), HBM_SPEC,
        ],
        out_specs=[
            _dproj_spec(ts, "a"), _const_spec((D, D)), _const_spec(gshape),
            _const_spec((SGU_GROUPS, SGU_BLOCK)), _const_spec((1, D)), _const_spec((1, D)),
        ],
        out_shape=[
            jax.ShapeDtypeStruct((s, N_IN), CDT), jax.ShapeDtypeStruct((D, D), F32),
            jax.ShapeDtypeStruct(gshape, F32), jax.ShapeDtypeStruct((SGU_GROUPS, SGU_BLOCK), F32),
            jax.ShapeDtypeStruct((1, D), F32), jax.ShapeDtypeStruct((1, D), F32),
        ],
        scratch_shapes=[
            pltpu.VMEM((ts, D), CDT), pltpu.VMEM((ts, D), F32), pltpu.VMEM((ts, D), CDT),
            pltpu.VMEM((ts, D), F32), pltpu.VMEM((ts, D), F32), pltpu.VMEM(gshape, F32),
        ],
        input_output_aliases={10: 0},
        compiler_params=_cparams(("arbitrary",)),
    )(proj, proj, proj, dp, lng, lnb, wm, wm_t, bias_full, wa_t, dproj)


def _bwd_b(proj, dp, pool_w, pool_w_t, pool_b, pool_scale, wb_t, dproj):
    s = proj.shape[0]
    ts = min(TS_MIX, s)
    nt = s // ts
    g = POOL_GDIM

    def body(bp_ref, bph_ref, bz_ref, dpb_ref, pw_ref, pwt_ref, pb_ref, ps_ref, wbt_ref, _dproj_in,
             db_ref, dwb_ref, dpw_ref, dpb_out_ref, dps_ref,
             pe, sa, sb, d_scr, lin_scr, ee, dl_scr, carry):
        _zero_first([dwb_ref, dpw_ref, dpb_out_ref, dps_ref, carry])
        tile = nt - 1 - pl.program_id(0)
        bz = bz_ref[...].astype(F32)
        pscale = ps_ref[...]
        f = _b_forward(bp_ref, bph_ref, bz, pw_ref, pb_ref[...], pscale, tile, pe, sa, sb, d_scr, lin_scr)
        dpb = dpb_ref[...]
        dwb_ref[...] += lax.dot_general(f["yb"].astype(CDT), dpb, TN_DIMS, preferred_element_type=F32)
        dyb = _dot(dpb, wbt_ref[...])
        t1 = dyb * f["lin"]
        dps_ref[...] += _colsum(t1 * f["sz"])
        db_ref[:, D:2 * D] = (t1 * pscale * _silu_grad(bz, f["sg"])).astype(CDT)
        dlin = dyb * pscale * f["sz"]
        dpb_out_ref[...] += _colsum(dlin)
        dl_scr[...] = dlin.astype(CDT)
        r = HALO + ts
        ee[ts:r, :] = carry[...]
        for k in range(POOL_GROUPS):
            cols = slice(k * g, (k + 1) * g)
            dlk = dl_scr[:, cols]
            dpw_ref[k] += lax.dot_general(d_scr[:, cols], dlk, TN_DIMS, preferred_element_type=F32)
            dd = _dot(dlk, pwt_ref[k])
            lin_scr[:, cols] = dd
            ee[0:ts, cols] = dd * f["invs"][k]
        carry[...] = ee[0:HALO, :]
        sa[0:ts + 24, :] = ee[0:ts + 24, :] + ee[1:ts + 25, :]
        sb[0:ts + 16, g:] = sa[0:ts + 16, g:] + sa[2:ts + 18, g:]
        sa[0:ts + 8, 2 * g:] = sb[0:ts + 8, 2 * g:] + sb[4:ts + 12, 2 * g:]
        sb[0:ts, 3 * g:] = sa[0:ts, 3 * g:] + sa[8:ts + 8, 3 * g:]
        sums = [sa[0:ts, 0:g], sb[0:ts, g:2 * g], sa[0:ts, 2 * g:3 * g], sb[0:ts, 3 * g:]]
        for k in range(POOL_GROUPS):
            cols = slice(k * g, (k + 1) * g)
            db_ref[:, cols] = (sums[k] - lin_scr[:, cols]).astype(CDT)

    ext = pltpu.VMEM((HALO + ts, D), F32)
    wshape = (POOL_GROUPS, g, g)
    return pl.pallas_call(
        body,
        name="bwd_b",
        grid=(nt,),
        in_specs=[
            _chunk_spec(ts, C_BP, rev_nt=nt), _halo_spec(ts, C_BP, rev_nt=nt), _chunk_spec(ts, C_BZ, rev_nt=nt),
            _chunk_spec(ts, 1, rev_nt=nt),
            _const_spec(wshape), _const_spec(wshape), _const_spec((1, D)), _const_spec((1, D)), _const_spec((D, D)),
            HBM_SPEC,
        ],
        out_specs=[
            _dproj_spec(ts, "b", rev_nt=nt), _const_spec((D, D)), _const_spec(wshape),
            _const_spec((1, D)), _const_spec((1, D)),
        ],
        out_shape=[
            jax.ShapeDtypeStruct((s, N_IN), CDT), jax.ShapeDtypeStruct((D, D), F32),
            jax.ShapeDtypeStruct(wshape, F32), jax.ShapeDtypeStruct((1, D), F32), jax.ShapeDtypeStruct((1, D), F32),
        ],
        scratch_shapes=[ext, ext, ext, pltpu.VMEM((ts, D), CDT), pltpu.VMEM((ts, D), F32), ext,
                        pltpu.VMEM((ts, D), CDT), pltpu.VMEM((HALO, D), F32)],
        input_output_aliases={9: 0},
        compiler_params=_cparams(("arbitrary",)),
    )(proj, proj, proj, dp, pool_w, pool_w_t, pool_b, pool_scale, wb_t, dproj)


def _bwd_c(proj, dp, conv_w, conv_b, wc_t, dproj):
    s = proj.shape[0]
    ts = min(TS_MIX, s)
    nt = s // ts

    def body(ch_ref, chh_ref, cb_ref, cc_ref, cch_ref, cz_ref, dpc_ref, cw_ref, cbias_ref, wct_ref, _dproj_in,
             dc_ref, dwc_ref, dcw_ref, dcb_ref, qe, de, carry):
        _zero_first([dwc_ref, dcw_ref, dcb_ref, carry])
        tile = nt - 1 - pl.program_id(0)
        cb = cb_ref[...].astype(F32)
        cz = cz_ref[...].astype(F32)
        f = _c_forward(ch_ref, chh_ref, cb, cc_ref, cch_ref, cz, cw_ref, cbias_ref[...], tile, qe)
        dpc = dpc_ref[...]
        dwc_ref[...] += lax.dot_general(f["yc"].astype(CDT), dpc, TN_DIMS, preferred_element_type=F32)
        dyc = _dot(dpc, wct_ref[...])
        t1 = dyc * f["conv"]
        dc_ref[:, D:2 * D] = (t1 * f["sz"]).astype(CDT)
        dc_ref[:, 3 * D:4 * D] = (t1 * cb * _silu_grad(cz, f["sg"])).astype(CDT)
        dconv = dyc * cb * f["sz"]
        r = HALO + ts
        dcb_ref[...] += _colsum(dconv)
        dcw_ref[0:1, :] += _colsum(dconv * qe[HALO - 2:r - 2, :])
        dcw_ref[1:2, :] += _colsum(dconv * qe[HALO - 1:r - 1, :])
        dcw_ref[2:3, :] += _colsum(dconv * qe[HALO:r, :])
        de[0:ts, :] = dconv
        de[ts:ts + 8, :] = carry[...]
        carry[...] = de[0:8, :]
        dq = cw_ref[2:3, :] * dconv + cw_ref[1:2, :] * de[1:ts + 1, :] + cw_ref[0:1, :] * de[2:ts + 2, :]
        dc_ref[:, 0:D] = (dq * cc_ref[...].astype(F32)).astype(CDT)
        dc_ref[:, 2 * D:3 * D] = (dq * ch_ref[...].astype(F32)).astype(CDT)

    return pl.pallas_call(
        body,
        name="bwd_c",
        grid=(nt,),
        in_specs=[
            _chunk_spec(ts, C_CH, rev_nt=nt), _halo_spec(ts, C_CH, rev_nt=nt), _chunk_spec(ts, C_CB, rev_nt=nt),
            _chunk_spec(ts, C_CC, rev_nt=nt), _halo_spec(ts, C_CC, rev_nt=nt), _chunk_spec(ts, C_CZ, rev_nt=nt),
            _chunk_spec(ts, 2, rev_nt=nt),
            _const_spec((3, D)), _const_spec((1, D)), _const_spec((D, D)), HBM_SPEC,
        ],
        out_specs=[
            _dproj_spec(ts, "c", rev_nt=nt), _const_spec((D, D)), _const_spec((8, D)),
            _const_spec((1, D)),
        ],
        out_shape=[
            jax.ShapeDtypeStruct((s, N_IN), CDT), jax.ShapeDtypeStruct((D, D), F32),
            jax.ShapeDtypeStruct((8, D), F32), jax.ShapeDtypeStruct((1, D), F32),
        ],
        scratch_shapes=[pltpu.VMEM((HALO + ts, D), F32), pltpu.VMEM((ts + 8, D), F32), pltpu.VMEM((8, D), F32)],
        input_output_aliases={10: 0},
        compiler_params=_cparams(("arbitrary",)),
    )(proj, proj, proj, proj, proj, proj, dp, conv_w, conv_b, wc_t, dproj)


TK_DX = 2 * D


def _bwd_proj_dx(dproj, w_in_t, x, g_row, dx_out, scat, gath):
    s = x.shape[0]
    ts = min(TS_PROJ, s)
    nk = N_IN // TK_DX
    nt = s // ts
    ns, nx = len(scat), len(scat) + len(gath)

    def body(*refs):
        d_ref, wt_ref, x_ref, g_ref, dxo_ref = refs[:5]
        ex_ins = refs[5:5 + nx]
        dxi_ref, dng_ref = refs[5 + nx:7 + nx]
        ex_outs = refs[7 + nx:7 + 2 * nx]
        acc = refs[7 + 2 * nx]
        sems = refs[8 + 2 * nx:]
        i, k = pl.program_id(0), pl.program_id(1)

        def exchange():
            return _exchange_copies(ex_ins[:ns], ex_outs[:ns], ex_ins[ns:], ex_outs[ns:], *sems)

        @pl.when((i == 0) & (k == 0))
        def _():
            dng_ref[...] = jnp.zeros_like(dng_ref)
            for cp in exchange():
                cp.start()

        @pl.when(k == 0)
        def _():
            acc[...] = jnp.zeros_like(acc)

        acc[...] += _dot(d_ref[...], wt_ref[...])

        @pl.when(k == nk - 1)
        def _():
            xf = x_ref[...]
            g = g_ref[...]
            dh = acc[...]
            r = lax.rsqrt(_rowmean(xf * xf) + RMS_EPS)
            xhat = xf * r
            dng_ref[...] += _colsum(dh * xhat)
            dxhat = dh * g
            dxi_ref[...] = dxo_ref[...] + r * (dxhat - xhat * _rowmean(dxhat * xhat))

        @pl.when((i == nt - 1) & (k == nk - 1))
        def _():
            for cp in exchange():
                cp.wait()

    tile = pl.BlockSpec((ts, D), lambda i, k: (i, 0))
    row = pl.BlockSpec((1, D), lambda i, k: (0, 0))
    return pl.pallas_call(
        body,
        name="bwd_proj_dx_exchange",
        grid=(nt, nk),
        in_specs=[
            pl.BlockSpec((ts, TK_DX), lambda i, k: (i, k)), pl.BlockSpec((TK_DX, D), lambda i, k: (k, 0)),
            tile, row, tile,
        ] + [HBM_SPEC] * nx,
        out_specs=[tile, row] + [HBM_SPEC] * nx,
        out_shape=[jax.ShapeDtypeStruct((s, D), F32), jax.ShapeDtypeStruct((1, D), F32)]
        + _exchange_shapes(scat, gath),
        scratch_shapes=[pltpu.VMEM((ts, D), F32)] + _exchange_sems(nx),
        compiler_params=_cparams(("arbitrary", "arbitrary")),
    )(dproj, w_in_t, x, g_row, dx_out, *scat, *gath)


TN_DW = 512


def _bwd_proj_dw(h_t, dproj):
    s = h_t.shape[1]
    tk = min(4096, s)
    per_shard = SHARD_N // TN_DW
    per_chunk = D // TN_DW
    nk = s // tk

    def body(ht_ref, d_ref, out_ref, acc):
        k = pl.program_id(1)

        @pl.when(k == 0)
        def _():
            acc[...] = jnp.zeros_like(acc)

        acc[...] += _dot(ht_ref[...], d_ref[...])

        @pl.when(k == nk - 1)
        def _():
            out_ref[...] = acc[...].astype(CDT)

    def out_index(n, k):
        chunk = n // per_chunk
        proj_chunk = sum(jnp.where(chunk == c, p, 0) for c, p in enumerate(DPROJ_PERM))
        col = proj_chunk * per_chunk + n % per_chunk
        return (col // per_shard, 0, col % per_shard)

    return pl.pallas_call(
        body,
        name="bwd_proj_dw",
        grid=(N_IN // TN_DW, nk),
        in_specs=[pl.BlockSpec((D, tk), lambda n, k: (0, k)), pl.BlockSpec((tk, TN_DW), lambda n, k: (k, n))],
        out_specs=pl.BlockSpec((None, D, TN_DW), out_index),
        out_shape=jax.ShapeDtypeStruct((NDEV, D, SHARD_N), CDT),
        scratch_shapes=[pltpu.VMEM((D, TN_DW), F32)],
        compiler_params=_cparams(("arbitrary", "arbitrary")),
    )(h_t, dproj)


def _adamw(recvs, w, m, v, tr, name):
    nq = len(recvs)
    _, rows, c = recvs[0].shape
    assert rows % tr == 0 and w.shape == (nq * rows, c)
    steps = rows // tr

    def body(*refs):
        rrefs = refs[:nq]
        w_ref, m_ref, v_ref, g_out, d_out, m_out, v_out = refs[nq:]
        q = pl.program_id(0)
        for qi in range(nq):
            @pl.when(q == qi)
            def _(r=rrefs[qi]):
                g = r[0].astype(F32)
                for j in range(1, NDEV):
                    g = g + r[j].astype(F32)
                wv = w_ref[...]
                m2 = ADAM_B1 * m_ref[...] + (1.0 - ADAM_B1) * g
                v2 = ADAM_B2 * v_ref[...] + (1.0 - ADAM_B2) * (g * g)
                m_hat = m2 / (1.0 - ADAM_B1 ** ADAM_STEP)
                v_hat = v2 / (1.0 - ADAM_B2 ** ADAM_STEP)
                g_out[...] = g
                d_out[...] = -ADAM_LR * (m_hat / (jnp.sqrt(v_hat) + ADAM_EPS) + ADAM_WD * wv)
                m_out[...] = m2
                v_out[...] = v2

    def rspec(qi):
        return pl.BlockSpec((NDEV, tr, c), lambda q, t: (0, jnp.where(q == qi, t, 0), 0))

    tile = pl.BlockSpec((tr, c), lambda q, t: (q * steps + t, 0))
    shp = jax.ShapeDtypeStruct(w.shape, F32)
    return pl.pallas_call(
        body,
        name=name,
        grid=(nq, steps),
        in_specs=[rspec(qi) for qi in range(nq)] + [tile, tile, tile],
        out_specs=[tile] * 4,
        out_shape=[shp] * 4,
        compiler_params=_cparams(("arbitrary", "arbitrary")),
    )(*recvs, w, m, v)


SMALL_ROWS = 264
REP_NAMES = ("sgu_ln_g", "sgu_ln_b", "sgu_b", "pool_b", "pool_scale", "conv_b")


def _pack_small(pool_part, conv_part):
    n = pool_part.shape[0]
    flat = jnp.concatenate([pool_part.reshape(n, -1), conv_part.reshape(n, -1)], axis=1)
    return jnp.pad(flat, ((0, 0), (0, SMALL_ROWS * 128 - flat.shape[1]))).reshape(n, SMALL_ROWS, 128)


def _unpack_small(packed):
    flat = packed.reshape(DEPTH, SMALL_ROWS * 128)
    n_pool = POOL_GROUPS * (POOL_GDIM // NDEV) * POOL_GDIM
    pool = flat[:, :n_pool].reshape(DEPTH, POOL_GROUPS, POOL_GDIM // NDEV, POOL_GDIM)
    conv = flat[:, n_pool:n_pool + 3 * (D // NDEV)].reshape(DEPTH, 3, D // NDEV)
    return pool, conv


REP_LAYER_ROWS = len(REP_NAMES) * 8 + SGU_GROUPS * SGU_BLOCK
NORM_ROWS = 8 * (DEPTH + 1)


def _pack_rep_layer(vals, sgu_w_l):
    parts = [vals[nme].reshape(8, 128) for nme in REP_NAMES]
    parts.append(sgu_w_l.reshape(SGU_GROUPS * SGU_BLOCK, SGU_BLOCK))
    return jnp.concatenate(parts, axis=0)


def _pack_rep(per_layer, sgu_w):
    vecs = jnp.stack([per_layer[nme].reshape(DEPTH, 8, 128) for nme in REP_NAMES], axis=1)
    vecs = vecs.reshape(DEPTH, 8 * len(REP_NAMES), 128)
    mats = sgu_w.reshape(DEPTH, SGU_GROUPS * SGU_BLOCK, SGU_BLOCK)
    return jnp.concatenate([vecs, mats], axis=1).reshape(DEPTH * REP_LAYER_ROWS, 128)


def _unpack_rep(packed):
    p = packed.reshape(DEPTH, REP_LAYER_ROWS, 128)
    nv = 8 * len(REP_NAMES)
    vecs = p[:, :nv].reshape(DEPTH, len(REP_NAMES), D)
    out = {nme: vecs[:, q] for q, nme in enumerate(REP_NAMES)}
    out["sgu_b"] = out["sgu_b"].reshape(DEPTH, SGU_GROUPS, SGU_BLOCK)
    return out, p[:, nv:].reshape(DEPTH, SGU_GROUPS, SGU_BLOCK, SGU_BLOCK)


def _pack_norm(norm_g_like, final_g_like):
    return jnp.concatenate([norm_g_like.reshape(DEPTH * 8, 128), final_g_like.reshape(8, 128)], axis=0)


def _unpack_norm(packed):
    return packed[:DEPTH * 8].reshape(DEPTH, D), packed[DEPTH * 8:].reshape(D)


def kernel(x, norm_g, w_in, sgu_ln_g, sgu_ln_b, sgu_w, sgu_b, pool_w, pool_b, pool_scale, conv_w, conv_b, w_branch_a, w_branch_b, w_branch_c, w_out, final_g, loss_target, m_norm_g, m_w_in, m_sgu_ln_g, m_sgu_ln_b, m_sgu_w, m_sgu_b, m_pool_w, m_pool_b, m_pool_scale, m_conv_w, m_conv_b, m_w_branch_a, m_w_branch_b, m_w_branch_c, m_w_out, m_final_g, v_norm_g, v_w_in, v_sgu_ln_g, v_sgu_ln_b, v_sgu_w, v_sgu_b, v_pool_w, v_pool_b, v_pool_scale, v_conv_w, v_conv_b, v_w_branch_a, v_w_branch_b, v_w_branch_c, v_w_out, v_final_g):
    assert x.shape[0] == 1 and x.shape[2] == D
    s = x.shape[1]
    x0 = x.reshape(s, D)
    target = loss_target.reshape(s, D)

    w_br = jnp.stack([w_branch_a, w_branch_b, w_branch_c, w_out], axis=1)
    shards = [[w_in[l].astype(CDT), w_br[l].astype(CDT), pool_w[l].astype(CDT), conv_w[l]] for l in range(DEPTH)]

    def layout(gathered):
        g_in, g_br, g_pool, g_conv = gathered
        mats = jnp.transpose(g_br, (1, 0, 2, 3)).reshape(4, D, D)
        pool_full = jnp.transpose(g_pool, (1, 0, 2, 3)).reshape(POOL_GROUPS, POOL_GDIM, POOL_GDIM)
        w_in_t = jnp.transpose(g_in, (0, 2, 1)).reshape(N_IN // D, D, D)
        return dict(
            g_in=g_in,
            w_in_t=jnp.concatenate([w_in_t[p] for p in DPROJ_PERM], axis=0),
            mats=mats, mats_t=jnp.transpose(mats, (0, 2, 1)),
            pool=pool_full, pool_t=jnp.transpose(pool_full, (0, 2, 1)),
            conv=jnp.transpose(g_conv, (1, 0, 2)).reshape(3, D),
        )

    pos = np.arange(SGU_BLOCK) // CHUNK
    mask = jnp.asarray(pos[None, :] <= pos[:, None])
    wm_all = jnp.where(mask[None, None], sgu_w, 0.0).astype(CDT)
    wm_all_t = jnp.transpose(wm_all, (0, 1, 3, 2))
    bias_all = jnp.repeat(jnp.transpose(sgu_b, (0, 2, 1)), SGU_BLOCK, axis=2)

    def row(a, l):
        return a[l].reshape(1, D)

    g_in0 = _gather_two_level(shards[0][0], "gather_w_in0")
    weights = [None] * DEPTH
    saved = []
    xl = x0
    for l in range(DEPTH):
        g_in = g_in0 if l == 0 else weights[l]["g_in"]
        behind = (shards[0][1:] if l == 0 else []) + (shards[l + 1] if l + 1 < DEPTH else [])
        proj, h_t, *got = _fwd_proj(xl, row(norm_g, l), g_in, behind)
        if l == 0:
            weights[0] = layout([g_in0] + got[:3])
            got = got[3:]
        if l + 1 < DEPTH:
            weights[l + 1] = layout(got)
        w = weights[l]
        pa = _fwd_a(proj, row(sgu_ln_g, l), row(sgu_ln_b, l), wm_all[l], bias_all[l], w["mats"][0])
        pb = _fwd_b(proj, w["pool"], row(pool_b, l), row(pool_scale, l), w["mats"][1])
        pc = _fwd_c(proj, w["conv"], row(conv_b, l), w["mats"][2])
        saved.append((xl, proj, h_t, pa, pb, pc))
        xl = _fwd_o(xl, pa, pb, pc, proj, w["mats"][3])

    dx, loss_cols, dfinal_g = _loss_head(xl, target, final_g.reshape(1, D))
    loss = lax.psum(0.5 / D * jnp.sum(loss_cols), ("x", "y", "c"))

    norm_grads = [None] * DEPTH
    recv_in, recv_br, recv_small, recv_rep = ([None] * DEPTH for _ in range(4))
    for l in reversed(range(DEPTH)):
        w = weights[l]
        xl, proj, h_t, pa, pb, pc = saved[l]
        dp, dproj, dwo = _bwd_o(dx, pa, pb, pc, proj, w["mats_t"][3])
        dproj, dwa, dwm, dbs, dlng, dlnb = _bwd_a(
            proj, dp, row(sgu_ln_g, l), row(sgu_ln_b, l), wm_all[l], wm_all_t[l], bias_all[l], w["mats_t"][0], dproj)
        dproj, dwb, dpw, dpb, dps = _bwd_b(
            proj, dp, w["pool"], w["pool_t"], row(pool_b, l), row(pool_scale, l), w["mats_t"][1], dproj)
        dproj, dwc, dcw, dcb = _bwd_c(proj, dp, w["conv"], row(conv_b, l), w["mats_t"][2], dproj)
        dw_in = _bwd_proj_dw(h_t, dproj)

        dw_br = jnp.stack([dwa, dwb, dwc, dwo]).astype(CDT).reshape(4, NDEV, D // NDEV, D).transpose(1, 0, 2, 3)
        dpw_j = dpw.reshape(POOL_GROUPS, NDEV, POOL_GDIM // NDEV, POOL_GDIM).transpose(1, 0, 2, 3)
        dcw_j = dcw[0:3].reshape(3, NDEV, D // NDEV).transpose(1, 0, 2)
        small = _pack_small(dpw_j, dcw_j)
        rep = _pack_rep_layer(
            dict(sgu_ln_g=dlng, sgu_ln_b=dlnb, sgu_b=dbs, pool_b=dpb, pool_scale=dps, conv_b=dcb),
            jnp.where(mask[None], dwm, 0.0))
        dx, norm_grads[l], recv_in[l], recv_br[l], recv_small[l], recv_rep[l] = _bwd_proj_dx(
            dproj, w["w_in_t"], xl, row(norm_g, l), dx, [dw_in, dw_br, small], [rep])

    (recv_norm,) = _exchange([], [_pack_norm(jnp.stack(norm_grads), dfinal_g)], "gather_norm_grads")

    rep_w = dict(sgu_ln_g=sgu_ln_g, sgu_ln_b=sgu_ln_b, sgu_b=sgu_b, pool_b=pool_b, pool_scale=pool_scale, conv_b=conv_b)
    rep_m = dict(sgu_ln_g=m_sgu_ln_g, sgu_ln_b=m_sgu_ln_b, sgu_b=m_sgu_b, pool_b=m_pool_b, pool_scale=m_pool_scale,
                 conv_b=m_conv_b)
    rep_v = dict(sgu_ln_g=v_sgu_ln_g, sgu_ln_b=v_sgu_ln_b, sgu_b=v_sgu_b, pool_b=v_pool_b, pool_scale=v_pool_scale,
                 conv_b=v_conv_b)
    rep_out = _adamw(recv_rep, _pack_rep(rep_w, sgu_w), _pack_rep(rep_m, m_sgu_w), _pack_rep(rep_v, v_sgu_w),
                     REP_LAYER_ROWS // 2, "adamw_replicated")
    rep_res = [_unpack_rep(o) for o in rep_out]
    norm_out = _adamw([recv_norm], _pack_norm(norm_g, final_g), _pack_norm(m_norm_g, m_final_g),
                      _pack_norm(v_norm_g, v_final_g), NORM_ROWS, "adamw_norm")
    norm_res = [_unpack_norm(o) for o in norm_out]

    in_out = _adamw(recv_in, w_in.reshape(DEPTH * D, SHARD_N), m_w_in.reshape(DEPTH * D, SHARD_N),
                    v_w_in.reshape(DEPTH * D, SHARD_N), 64, "adamw_w_in")
    in_res = [o.reshape(DEPTH, D, SHARD_N) for o in in_out]

    def stack_br(a, b, c, o):
        return jnp.stack([a, b, c, o], axis=1).reshape(DEPTH * 4 * (D // NDEV), D)

    br_out = _adamw(
        [r.reshape(NDEV, 4 * (D // NDEV), D) for r in recv_br],
        stack_br(w_branch_a, w_branch_b, w_branch_c, w_out),
        stack_br(m_w_branch_a, m_w_branch_b, m_w_branch_c, m_w_out),
        stack_br(v_w_branch_a, v_w_branch_b, v_w_branch_c, v_w_out), 128, "adamw_w_branch")
    br_res = [o.reshape(DEPTH, 4, D // NDEV, D) for o in br_out]

    def stack_small(pw, cw):
        return _pack_small(pw, cw).reshape(DEPTH * SMALL_ROWS, 128)

    small_out = _adamw(recv_small, stack_small(pool_w, conv_w), stack_small(m_pool_w, m_conv_w),
                       stack_small(v_pool_w, v_conv_w), SMALL_ROWS, "adamw_small")
    small_res = [_unpack_small(o) for o in small_out]

    def leaves(kind):
        rep, sgu_w_k = rep_res[kind]
        norm_k, final_k = norm_res[kind]
        br = br_res[kind]
        pool_k, conv_k = small_res[kind]
        return [norm_k, in_res[kind], rep["sgu_ln_g"], rep["sgu_ln_b"], sgu_w_k, rep["sgu_b"], pool_k,
                rep["pool_b"], rep["pool_scale"], conv_k, rep["conv_b"], br[:, 0], br[:, 1], br[:, 2], br[:, 3],
                final_k]

    return (loss, dx.reshape(1, s, D), *leaves(0), *leaves(1), *leaves(2), *leaves(3))
```

```python
import functools

import jax
import jax.numpy as jnp
import numpy as np
from jax import lax
from jax.experimental import pallas as pl
from jax.experimental.pallas import tpu as pltpu

F32 = jnp.float32
CDT = jnp.bfloat16

D = 1024
N_IN = 12 * D
DEPTH = 4
NDEV = 8
SHARD_N = N_IN // NDEV
SGU_BLOCK = 128
SGU_GROUPS = 8
CHUNK = 64
POOL_GROUPS = 4
POOL_GDIM = D // POOL_GROUPS
POOL_WINDOWS = (2, 4, 8, 16)
RMS_EPS = 1e-6
LN_EPS = 1e-5
GELU_K0 = float(np.sqrt(2.0 / np.pi))
GELU_K1 = 0.044715

ADAM_LR = 0.001
ADAM_B1 = 0.9
ADAM_B2 = 0.999
ADAM_EPS = 1e-08
ADAM_WD = 0.01
ADAM_STEP = 10

HALO = 32
TS_MIX = 512
TS_PROJ = 1024
VMEM_LIMIT = 56 * 1024 * 1024

C_AU, C_AV, C_AZ, C_BP, C_BZ, C_CH, C_CB, C_CC, C_CZ, C_G0 = range(10)

TN_DIMS = (((0,), (0,)), ((), ()))
NT_DIMS = (((1,), (1,)), ((), ()))


def _cparams(sem):
    return pltpu.CompilerParams(dimension_semantics=sem, vmem_limit_bytes=VMEM_LIMIT)


def _dot(a, b):
    return jnp.dot(a, b, preferred_element_type=F32)


def _gelu(x):
    t = jnp.tanh(GELU_K0 * (x + GELU_K1 * (x * x * x)))
    return 0.5 * x * (1.0 + t), t


def _gelu_grad(x, t):
    return 0.5 * (1.0 + t) + 0.5 * x * (1.0 - t * t) * (GELU_K0 * (1.0 + 3.0 * GELU_K1 * (x * x)))


def _silu(x):
    s = jax.nn.sigmoid(x)
    return x * s, s


def _silu_grad(x, s):
    return s * (1.0 + x * (1.0 - s))


def _rowmean(x):
    return jnp.mean(x, axis=-1, keepdims=True)


def _colsum(x):
    return jnp.sum(x, axis=0, keepdims=True)


def _const_spec(shape):
    nd = len(shape)
    return pl.BlockSpec(shape, lambda *_: (0,) * nd)


def _chunk_spec(ts, chunk, width=1, rev_nt=None):
    if rev_nt is None:
        return pl.BlockSpec((ts, width * D), lambda i: (i, chunk))
    return pl.BlockSpec((ts, width * D), lambda i: (rev_nt - 1 - i, chunk))


def _halo_spec(ts, chunk, rev_nt=None):
    per = ts // HALO
    if rev_nt is None:
        return pl.BlockSpec((HALO, D), lambda i: (jnp.maximum(i * per - 1, 0), chunk))
    return pl.BlockSpec((HALO, D), lambda i: (jnp.maximum((rev_nt - 1 - i) * per - 1, 0), chunk))


def _exchange_copies(scat_ins, scat_outs, gath_ins, gath_outs, send_sems, recv_sems, local_sems):
    x, y, c = lax.axis_index("x"), lax.axis_index("y"), lax.axis_index("c")
    me = 4 * x + 2 * y + c
    arrs = [(i, o, True) for i, o in zip(scat_ins, scat_outs)] + [(i, o, False) for i, o in zip(gath_ins, gath_outs)]
    copies = []
    for k in range(1, NDEV):
        px = 1 - x if k & 4 else x
        py = 1 - y if k & 2 else y
        pc = 1 - c if k & 1 else c
        peer = 4 * px + 2 * py + pc
        for a, (src, dst, scatter) in enumerate(arrs):
            copies.append(pltpu.make_async_remote_copy(
                src_ref=src.at[peer] if scatter else src,
                dst_ref=dst.at[me],
                send_sem=send_sems.at[k - 1, a],
                recv_sem=recv_sems.at[k - 1, a],
                device_id=(px, py, pc),
                device_id_type=pl.DeviceIdType.MESH,
            ))
    for a, (src, dst, scatter) in enumerate(arrs):
        copies.append(pltpu.make_async_copy(src.at[me] if scatter else src, dst.at[me], local_sems.at[a]))
    return copies


def _exchange_shapes(scat, gath):
    return ([jax.ShapeDtypeStruct(a.shape, a.dtype) for a in scat]
            + [jax.ShapeDtypeStruct((NDEV,) + a.shape, a.dtype) for a in gath])


def _exchange_sems(n):
    return [pltpu.SemaphoreType.DMA((NDEV - 1, n)), pltpu.SemaphoreType.DMA((NDEV - 1, n)),
            pltpu.SemaphoreType.DMA((n,))]


HBM_SPEC = pl.BlockSpec(memory_space=pltpu.HBM)


def _exchange(scat, gath, name):
    ns, n = len(scat), len(scat) + len(gath)

    def body(*refs):
        ins, outs, sems = refs[:n], refs[n:2 * n], refs[2 * n:]
        copies = _exchange_copies(ins[:ns], outs[:ns], ins[ns:], outs[ns:], *sems)
        for cp in copies:
            cp.start()
        for cp in copies:
            cp.wait()

    return pl.pallas_call(
        body,
        name=name,
        out_shape=_exchange_shapes(scat, gath),
        in_specs=[HBM_SPEC] * n,
        out_specs=[HBM_SPEC] * n,
        scratch_shapes=_exchange_sems(n),
    )(*scat, *gath)


def _gather_two_level(shard, name):
    def body(x_ref, out_ref, send_sems, recv_sems, local_sem):
        x, y, c = lax.axis_index("x"), lax.axis_index("y"), lax.axis_index("c")
        me, sibling = (x, y, c), (x, y, 1 - c)
        chips = [(1 - x, y), (x, 1 - y), (1 - x, 1 - y)]

        def slot(px, py, pc):
            return out_ref.at[4 * px + 2 * py + pc]

        def copy(k, block, to, src=None):
            return pltpu.make_async_remote_copy(
                src_ref=slot(*block) if src is None else src,
                dst_ref=slot(*block),
                send_sem=send_sems.at[k],
                recv_sem=recv_sems.at[k],
                device_id=to,
                device_id_type=pl.DeviceIdType.MESH,
            )

        mine = pltpu.make_async_copy(x_ref, slot(*me), local_sem)
        mine.start()
        first = [copy(0, me, sibling, src=x_ref)]
        first += [copy(1 + j, me, (*chip, c), src=x_ref) for j, chip in enumerate(chips)]
        for cp in first:
            cp.start()
        passed = [copy(4 + j, (*chip, c), sibling) for j, chip in enumerate(chips)]
        for j, chip in enumerate(chips):
            copy(1 + j, (*chip, c), me).wait_recv()
            passed[j].start()
        copy(0, sibling, me).wait_recv()
        for j, chip in enumerate(chips):
            copy(4 + j, (*chip, 1 - c), me).wait_recv()
        for cp in first + passed:
            cp.wait_send()
        mine.wait()

    return pl.pallas_call(
        body,
        name=name,
        out_shape=jax.ShapeDtypeStruct((NDEV,) + shard.shape, shard.dtype),
        in_specs=[HBM_SPEC],
        out_specs=HBM_SPEC,
        scratch_shapes=[pltpu.SemaphoreType.DMA((NDEV - 1,)), pltpu.SemaphoreType.DMA((NDEV - 1,)),
                        pltpu.SemaphoreType.DMA],
    )(shard)


SHARDS_PER_STEP = 2


def _fwd_proj(x, g_row, w_gathered, next_shards=()):
    s = x.shape[0]
    ts = min(TS_PROJ, s)
    ng = len(next_shards)
    nt = s // ts

    def body(*refs):
        x_ref, g_ref, w_ref = refs[:3]
        gath_ins = refs[3:3 + ng]
        proj_ref, ht_ref = refs[3 + ng:5 + ng]
        gath_outs = refs[5 + ng:5 + 2 * ng]
        h_scr = refs[5 + 2 * ng]
        sems = refs[6 + 2 * ng:]
        i, j = pl.program_id(0), pl.program_id(1)

        if ng:
            @pl.when((i == 0) & (j == 0))
            def _():
                for cp in _exchange_copies((), (), gath_ins, gath_outs, *sems):
                    cp.start()

        @pl.when(j == 0)
        def _():
            xf = x_ref[...]
            h = xf * lax.rsqrt(_rowmean(xf * xf) + RMS_EPS) * g_ref[...]
            h_scr[...] = h.astype(CDT)
            ht_ref[...] = h.T.astype(CDT)

        for q in range(SHARDS_PER_STEP):
            proj_ref[:, q * SHARD_N:(q + 1) * SHARD_N] = _dot(h_scr[...], w_ref[q]).astype(CDT)

        if ng:
            @pl.when((i == nt - 1) & (j == nj - 1))
            def _():
                for cp in _exchange_copies((), (), gath_ins, gath_outs, *sems):
                    cp.wait()

    nj = NDEV // SHARDS_PER_STEP
    return pl.pallas_call(
        body,
        name="fwd_proj_gather" if ng else "fwd_proj",
        grid=(nt, nj),
        in_specs=[
            pl.BlockSpec((ts, D), lambda i, j: (i, 0)),
            pl.BlockSpec((1, D), lambda i, j: (0, 0)),
            pl.BlockSpec((SHARDS_PER_STEP, D, SHARD_N), lambda i, j: (j, 0, 0)),
        ] + [HBM_SPEC] * ng,
        out_specs=[
            pl.BlockSpec((ts, SHARDS_PER_STEP * SHARD_N), lambda i, j: (i, j)),
            pl.BlockSpec((D, ts), lambda i, j: (0, i)),
        ] + [HBM_SPEC] * ng,
        out_shape=[jax.ShapeDtypeStruct((s, N_IN), CDT), jax.ShapeDtypeStruct((D, s), CDT)]
        + _exchange_shapes((), next_shards),
        scratch_shapes=[pltpu.VMEM((ts, D), CDT)] + (_exchange_sems(ng) if ng else []),
        compiler_params=_cparams(("arbitrary", "arbitrary")),
    )(x, g_row, w_gathered, *next_shards)


def _a_forward(au, av, az, lng, lnb, wm_ref, bias_ref, v_scr, mix_scr):
    ts = au.shape[0]
    u, tu = _gelu(au)
    vg, tv = _gelu(av)
    xc = vg - _rowmean(vg)
    rstd = lax.rsqrt(_rowmean(xc * xc) + LN_EPS)
    vhat = xc * rstd
    v_scr[...] = (vhat * lng + lnb).astype(CDT)
    for n in range(ts // SGU_BLOCK):
        rows = slice(n * SGU_BLOCK, (n + 1) * SGU_BLOCK)
        for g in range(SGU_GROUPS):
            cols = slice(g * SGU_BLOCK, (g + 1) * SGU_BLOCK)
            mix_scr[rows, cols] = _dot(wm_ref[g], v_scr[rows, cols]) + bias_ref[:, cols]
    mixed = mix_scr[...]
    sz, sg = _silu(az)
    return dict(u=u, tu=tu, tv=tv, rstd=rstd, vhat=vhat, mixed=mixed, sz=sz, sg=sg, ya=u * mixed * sz)


def _fwd_a(proj, lng, lnb, wm, bias_full, wa):
    s = proj.shape[0]
    ts = min(TS_MIX, s)

    def body(au_ref, av_ref, az_ref, lng_ref, lnb_ref, wm_ref, bias_ref, wa_ref, pa_ref, v_scr, mix_scr):
        f = _a_forward(au_ref[...].astype(F32), av_ref[...].astype(F32), az_ref[...].astype(F32),
                       lng_ref[...], lnb_ref[...], wm_ref, bias_ref, v_scr, mix_scr)
        pa_ref[...] = _dot(f["ya"].astype(CDT), wa_ref[...]).astype(CDT)

    return pl.pallas_call(
        body,
        name="fwd_a",
        grid=(s // ts,),
        in_specs=[
            _chunk_spec(ts, C_AU), _chunk_spec(ts, C_AV), _chunk_spec(ts, C_AZ),
            _const_spec((1, D)), _const_spec((1, D)),
            _const_spec((SGU_GROUPS, SGU_BLOCK, SGU_BLOCK)), _const_spec((SGU_BLOCK, D)),
            _const_spec((D, D)),
        ],
        out_specs=pl.BlockSpec((ts, D), lambda i: (i, 0)),
        out_shape=jax.ShapeDtypeStruct((s, D), CDT),
        scratch_shapes=[pltpu.VMEM((ts, D), CDT), pltpu.VMEM((ts, D), F32)],
        compiler_params=_cparams(("arbitrary",)),
    )(proj, proj, proj, lng, lnb, wm, bias_full, wa)


def _pool_d(pe, sa, sb, tile, ts):
    r = HALO + ts
    g = POOL_GDIM
    sa[8:r, :] = pe[8:r, :] + pe[7:r - 1, :]
    sb[16:r, g:] = sa[16:r, g:] + sa[14:r - 2, g:]
    sa[24:r, 2 * g:] = sb[24:r, 2 * g:] + sb[20:r - 4, 2 * g:]
    sb[32:r, 3 * g:] = sa[32:r, 3 * g:] + sa[24:r - 8, 3 * g:]
    pos1 = tile * ts + lax.broadcasted_iota(jnp.int32, (ts, 1), 0) + 1
    invs = [1.0 / jnp.minimum(pos1, w).astype(F32) for w in POOL_WINDOWS]
    sums = [sa[HALO:r, 0:g], sb[HALO:r, g:2 * g], sa[HALO:r, 2 * g:3 * g], sb[HALO:r, 3 * g:]]
    d = [sums[k] * invs[k] - pe[HALO:r, k * g:(k + 1) * g] for k in range(POOL_GROUPS)]
    return d, invs


def _b_forward(bp_ref, bph_ref, bz, pw_ref, pb, pscale, tile, pe, sa, sb, d_scr, lin_scr):
    ts = bz.shape[0]
    pe[0:HALO, :] = jnp.where(tile > 0, bph_ref[...].astype(F32), 0.0)
    pe[HALO:, :] = bp_ref[...].astype(F32)
    d, invs = _pool_d(pe, sa, sb, tile, ts)
    for k in range(POOL_GROUPS):
        cols = slice(k * POOL_GDIM, (k + 1) * POOL_GDIM)
        d_scr[:, cols] = d[k].astype(CDT)
        lin_scr[:, cols] = _dot(d_scr[:, cols], pw_ref[k])
    lin = lin_scr[...] + pb
    sz, sg = _silu(bz)
    return dict(lin=lin, sz=sz, sg=sg, invs=invs, yb=lin * pscale * sz)


def _fwd_b(proj, pool_w, pool_b, pool_scale, wb):
    s = proj.shape[0]
    ts = min(TS_MIX, s)

    def body(bp_ref, bph_ref, bz_ref, pw_ref, pb_ref, ps_ref, wb_ref, out_ref, pe, sa, sb, d_scr, lin_scr):
        f = _b_forward(bp_ref, bph_ref, bz_ref[...].astype(F32), pw_ref, pb_ref[...], ps_ref[...],
                       pl.program_id(0), pe, sa, sb, d_scr, lin_scr)
        out_ref[...] = _dot(f["yb"].astype(CDT), wb_ref[...]).astype(CDT)

    ext = pltpu.VMEM((HALO + ts, D), F32)
    return pl.pallas_call(
        body,
        name="fwd_b",
        grid=(s // ts,),
        in_specs=[
            _chunk_spec(ts, C_BP), _halo_spec(ts, C_BP), _chunk_spec(ts, C_BZ),
            _const_spec((POOL_GROUPS, POOL_GDIM, POOL_GDIM)), _const_spec((1, D)), _const_spec((1, D)),
            _const_spec((D, D)),
        ],
        out_specs=pl.BlockSpec((ts, D), lambda i: (i, 0)),
        out_shape=jax.ShapeDtypeStruct((s, D), CDT),
        scratch_shapes=[ext, ext, ext, pltpu.VMEM((ts, D), CDT), pltpu.VMEM((ts, D), F32)],
        compiler_params=_cparams(("arbitrary",)),
    )(proj, proj, proj, pool_w, pool_b, pool_scale, wb)


def _c_forward(ch_ref, chh_ref, cb, cc_ref, cch_ref, cz, cw_ref, cbias, tile, qe):
    ts = cb.shape[0]
    r = HALO + ts
    qe[0:HALO, :] = jnp.where(tile > 0, chh_ref[...].astype(F32) * cch_ref[...].astype(F32), 0.0)
    qe[HALO:, :] = cc_ref[...].astype(F32) * ch_ref[...].astype(F32)
    conv = (cw_ref[0:1, :] * qe[HALO - 2:r - 2, :] + cw_ref[1:2, :] * qe[HALO - 1:r - 1, :]
            + cw_ref[2:3, :] * qe[HALO:r, :]) + cbias
    sz, sg = _silu(cz)
    return dict(conv=conv, sz=sz, sg=sg, yc=cb * conv * sz)


def _fwd_c(proj, conv_w, conv_b, wc):
    s = proj.shape[0]
    ts = min(TS_MIX, s)

    def body(ch_ref, chh_ref, cb_ref, cc_ref, cch_ref, cz_ref, cw_ref, cbias_ref, wc_ref, out_ref, qe):
        f = _c_forward(ch_ref, chh_ref, cb_ref[...].astype(F32), cc_ref, cch_ref, cz_ref[...].astype(F32),
                       cw_ref, cbias_ref[...], pl.program_id(0), qe)
        out_ref[...] = _dot(f["yc"].astype(CDT), wc_ref[...]).astype(CDT)

    return pl.pallas_call(
        body,
        name="fwd_c",
        grid=(s // ts,),
        in_specs=[
            _chunk_spec(ts, C_CH), _halo_spec(ts, C_CH), _chunk_spec(ts, C_CB),
            _chunk_spec(ts, C_CC), _halo_spec(ts, C_CC), _chunk_spec(ts, C_CZ),
            _const_spec((3, D)), _const_spec((1, D)), _const_spec((D, D)),
        ],
        out_specs=pl.BlockSpec((ts, D), lambda i: (i, 0)),
        out_shape=jax.ShapeDtypeStruct((s, D), CDT),
        scratch_shapes=[pltpu.VMEM((HALO + ts, D), F32)],
        compiler_params=_cparams(("arbitrary",)),
    )(proj, proj, proj, proj, proj, proj, conv_w, conv_b, wc)


def _merge(pa, pb, pc, logits):
    gates = [jax.nn.sigmoid(logits[:, k * D:(k + 1) * D].astype(F32)) for k in range(3)]
    ps = [pa.astype(F32), pb.astype(F32), pc.astype(F32)]
    merged = gates[0] * ps[0] + gates[1] * ps[1] + gates[2] * ps[2]
    return gates, ps, merged


def _fwd_o(x, pa, pb, pc, proj, wo):
    s = x.shape[0]
    ts = min(TS_MIX, s)

    def body(x_ref, pa_ref, pb_ref, pc_ref, lg_ref, wo_ref, out_ref):
        _, _, merged = _merge(pa_ref[...], pb_ref[...], pc_ref[...], lg_ref[...])
        out_ref[...] = x_ref[...] + _dot(merged.astype(CDT), wo_ref[...])

    tile = pl.BlockSpec((ts, D), lambda i: (i, 0))
    return pl.pallas_call(
        body,
        name="fwd_o",
        grid=(s // ts,),
        in_specs=[tile, tile, tile, tile, _chunk_spec(ts, 3, width=3), _const_spec((D, D))],
        out_specs=tile,
        out_shape=jax.ShapeDtypeStruct((s, D), F32),
        compiler_params=_cparams(("arbitrary",)),
    )(x, pa, pb, pc, proj, wo)


def _loss_head(x, target, fg_row):
    s = x.shape[0]
    ts = min(TS_MIX, s)

    def body(x_ref, t_ref, g_ref, dx_ref, loss_ref, dg_ref):
        @pl.when(pl.program_id(0) == 0)
        def _():
            loss_ref[...] = jnp.zeros_like(loss_ref)
            dg_ref[...] = jnp.zeros_like(dg_ref)

        xf = x_ref[...]
        g = g_ref[...]
        r = lax.rsqrt(_rowmean(xf * xf) + RMS_EPS)
        xhat = xf * r
        err = xhat * g - t_ref[...]
        loss_ref[...] += _colsum(err * err)
        dy = err * (1.0 / D)
        dg_ref[...] += _colsum(dy * xhat)
        dxhat = dy * g
        dx_ref[...] = r * (dxhat - xhat * _rowmean(dxhat * xhat))

    tile = pl.BlockSpec((ts, D), lambda i: (i, 0))
    row = _const_spec((1, D))
    return pl.pallas_call(
        body,
        name="loss_head",
        grid=(s // ts,),
        in_specs=[tile, tile, row],
        out_specs=[tile, row, row],
        out_shape=[jax.ShapeDtypeStruct((s, D), F32), jax.ShapeDtypeStruct((1, D), F32),
                   jax.ShapeDtypeStruct((1, D), F32)],
        compiler_params=_cparams(("arbitrary",)),
    )(x, target, fg_row)


def _zero_first(refs):
    @pl.when(pl.program_id(0) == 0)
    def _():
        for r in refs:
            r[...] = jnp.zeros_like(r)


DPROJ_BLOCK = dict(c=(4, 0), b=(2, 2), a=(3, 2), g=(3, 3))
DPROJ_PERM = (5, 6, 7, 8, 3, 4, 0, 1, 2, 9, 10, 11)


def _dproj_spec(ts, part, rev_nt=None):
    width, block = DPROJ_BLOCK[part]
    if rev_nt is None:
        return pl.BlockSpec((ts, width * D), lambda i: (i, block))
    return pl.BlockSpec((ts, width * D), lambda i: (rev_nt - 1 - i, block))


def _bwd_o(dx, pa, pb, pc, proj, wo_t):
    s = dx.shape[0]
    ts = min(TS_MIX, s)

    def body(dx_ref, pa_ref, pb_ref, pc_ref, lg_ref, wot_ref, dp_ref, dg_ref, dwo_ref):
        _zero_first([dwo_ref])
        gates, ps, merged = _merge(pa_ref[...], pb_ref[...], pc_ref[...], lg_ref[...])
        dxb = dx_ref[...].astype(CDT)
        dwo_ref[...] += lax.dot_general(merged.astype(CDT), dxb, TN_DIMS, preferred_element_type=F32)
        dmerged = _dot(dxb, wot_ref[...])
        for k in range(3):
            cols = slice(k * D, (k + 1) * D)
            dpk = dmerged * gates[k]
            dp_ref[:, cols] = dpk.astype(CDT)
            dg_ref[:, cols] = (dpk * ps[k] * (1.0 - gates[k])).astype(CDT)

    tile = pl.BlockSpec((ts, D), lambda i: (i, 0))
    wide = pl.BlockSpec((ts, 3 * D), lambda i: (i, 0))
    return pl.pallas_call(
        body,
        name="bwd_o",
        grid=(s // ts,),
        in_specs=[tile, tile, tile, tile, _chunk_spec(ts, 3, width=3), _const_spec((D, D))],
        out_specs=[wide, _dproj_spec(ts, "g"), _const_spec((D, D))],
        out_shape=[jax.ShapeDtypeStruct((s, 3 * D), CDT), jax.ShapeDtypeStruct((s, N_IN), CDT),
                   jax.ShapeDtypeStruct((D, D), F32)],
        compiler_params=_cparams(("arbitrary",)),
    )(dx, pa, pb, pc, proj, wo_t)


def _bwd_a(proj, dp, lng, lnb, wm, wm_t, bias_full, wa_t, dproj):
    s = proj.shape[0]
    ts = min(TS_MIX, s)
    nt = s // ts

    def body(au_ref, av_ref, az_ref, dpa_ref, lng_ref, lnb_ref, wm_ref, wmt_ref, bias_ref, wat_ref, _dproj_in,
             da_ref, dwa_ref, dwm_ref, dbs_ref, dlng_ref, dlnb_ref,
             v_scr, mix_scr, dm_scr, dmf_scr, dv_scr, bsacc):
        _zero_first([dwa_ref, dwm_ref, dlng_ref, dlnb_ref, bsacc])
        au = au_ref[...].astype(F32)
        av = av_ref[...].astype(F32)
        az = az_ref[...].astype(F32)
        lng = lng_ref[...]
        f = _a_forward(au, av, az, lng, lnb_ref[...], wm_ref, bias_ref, v_scr, mix_scr)
        dpa = dpa_ref[...]
        dwa_ref[...] += lax.dot_general(f["ya"].astype(CDT), dpa, TN_DIMS, preferred_element_type=F32)
        dya = _dot(dpa, wat_ref[...])
        t1 = dya * f["mixed"]
        da_ref[:, 0:D] = (t1 * f["sz"] * _gelu_grad(au, f["tu"])).astype(CDT)
        da_ref[:, 2 * D:3 * D] = (t1 * f["u"] * _silu_grad(az, f["sg"])).astype(CDT)
        dmix = dya * f["u"] * f["sz"]
        dmf_scr[...] = dmix
        dm_scr[...] = dmix.astype(CDT)
        for n in range(ts // SGU_BLOCK):
            rows = slice(n * SGU_BLOCK, (n + 1) * SGU_BLOCK)
            for g in range(SGU_GROUPS):
                cols = slice(g * SGU_BLOCK, (g + 1) * SGU_BLOCK)
                dmb = dm_scr[rows, cols]
                dwm_ref[g] += lax.dot_general(dmb, v_scr[rows, cols], NT_DIMS, preferred_element_type=F32)
                dv_scr[rows, cols] = _dot(wmt_ref[g], dmb)
                bsacc[g] += dmf_scr[rows, cols]
        dvln = dv_scr[...]
        vhat = f["vhat"]
        dlng_ref[...] += _colsum(dvln * vhat)
        dlnb_ref[...] += _colsum(dvln)
        dvhat = dvln * lng
        dvg = f["rstd"] * (dvhat - _rowmean(dvhat) - vhat * _rowmean(dvhat * vhat))
        da_ref[:, D:2 * D] = (dvg * _gelu_grad(av, f["tv"])).astype(CDT)

        @pl.when(pl.program_id(0) == nt - 1)
        def _():
            ones = jnp.ones((8, SGU_BLOCK), F32)
            for g in range(SGU_GROUPS):
                red = lax.dot_general(ones, bsacc[g], NT_DIMS, preferred_element_type=F32,
                                      precision=lax.Precision.HIGHEST)
                dbs_ref[g:g + 1, :] = red[0:1, :]

    gshape = (SGU_GROUPS, SGU_BLOCK, SGU_BLOCK)
    return pl.pallas_call(
        body,
        name="bwd_a",
        grid=(nt,),
        in_specs=[
            _chunk_spec(ts, C_AU), _chunk_spec(ts, C_AV), _chunk_spec(ts, C_AZ), _chunk_spec(ts, 0),
            _const_spec((1, D)), _const_spec((1, D)), _const_spec(gshape), _const_spec(gshape),
            _const_spec((SGU_BLOCK, D)), _const_spec((D, D)), HBM_SPEC,
        ],
        out_specs=[
            _dproj_spec(ts, "a"), _const_spec((D, D)), _const_spec(gshape),
            _const_spec((SGU_GROUPS, SGU_BLOCK)), _const_spec((1, D)), _const_spec((1, D)),
        ],
        out_shape=[
            jax.ShapeDtypeStruct((s, N_IN), CDT), jax.ShapeDtypeStruct((D, D), F32),
            jax.ShapeDtypeStruct(gshape, F32), jax.ShapeDtypeStruct((SGU_GROUPS, SGU_BLOCK), F32),
            jax.ShapeDtypeStruct((1, D), F32), jax.ShapeDtypeStruct((1, D), F32),
        ],
        scratch_shapes=[
            pltpu.VMEM((ts, D), CDT), pltpu.VMEM((ts, D), F32), pltpu.VMEM((ts, D), CDT),
            pltpu.VMEM((ts, D), F32), pltpu.VMEM((ts, D), F32), pltpu.VMEM(gshape, F32),
        ],
        input_output_aliases={10: 0},
        compiler_params=_cparams(("arbitrary",)),
    )(proj, proj, proj, dp, lng, lnb, wm, wm_t, bias_full, wa_t, dproj)


def _bwd_b(proj, dp, pool_w, pool_w_t, pool_b, pool_scale, wb_t, dproj):
    s = proj.shape[0]
    ts = min(TS_MIX, s)
    nt = s // ts
    g = POOL_GDIM

    def body(bp_ref, bph_ref, bz_ref, dpb_ref, pw_ref, pwt_ref, pb_ref, ps_ref, wbt_ref, _dproj_in,
             db_ref, dwb_ref, dpw_ref, dpb_out_ref, dps_ref,
             pe, sa, sb, d_scr, lin_scr, ee, dl_scr, carry):
        _zero_first([dwb_ref, dpw_ref, dpb_out_ref, dps_ref, carry])
        tile = nt - 1 - pl.program_id(0)
        bz = bz_ref[...].astype(F32)
        pscale = ps_ref[...]
        f = _b_forward(bp_ref, bph_ref, bz, pw_ref, pb_ref[...], pscale, tile, pe, sa, sb, d_scr, lin_scr)
        dpb = dpb_ref[...]
        dwb_ref[...] += lax.dot_general(f["yb"].astype(CDT), dpb, TN_DIMS, preferred_element_type=F32)
        dyb = _dot(dpb, wbt_ref[...])
        t1 = dyb * f["lin"]
        dps_ref[...] += _colsum(t1 * f["sz"])
        db_ref[:, D:2 * D] = (t1 * pscale * _silu_grad(bz, f["sg"])).astype(CDT)
        dlin = dyb * pscale * f["sz"]
        dpb_out_ref[...] += _colsum(dlin)
        dl_scr[...] = dlin.astype(CDT)
        r = HALO + ts
        ee[ts:r, :] = carry[...]
        for k in range(POOL_GROUPS):
            cols = slice(k * g, (k + 1) * g)
            dlk = dl_scr[:, cols]
            dpw_ref[k] += lax.dot_general(d_scr[:, cols], dlk, TN_DIMS, preferred_element_type=F32)
            dd = _dot(dlk, pwt_ref[k])
            lin_scr[:, cols] = dd
            ee[0:ts, cols] = dd * f["invs"][k]
        carry[...] = ee[0:HALO, :]
        sa[0:ts + 24, :] = ee[0:ts + 24, :] + ee[1:ts + 25, :]
        sb[0:ts + 16, g:] = sa[0:ts + 16, g:] + sa[2:ts + 18, g:]
        sa[0:ts + 8, 2 * g:] = sb[0:ts + 8, 2 * g:] + sb[4:ts + 12, 2 * g:]
        sb[0:ts, 3 * g:] = sa[0:ts, 3 * g:] + sa[8:ts + 8, 3 * g:]
        sums = [sa[0:ts, 0:g], sb[0:ts, g:2 * g], sa[0:ts, 2 * g:3 * g], sb[0:ts, 3 * g:]]
        for k in range(POOL_GROUPS):
            cols = slice(k * g, (k + 1) * g)
            db_ref[:, cols] = (sums[k] - lin_scr[:, cols]).astype(CDT)

    ext = pltpu.VMEM((HALO + ts, D), F32)
    wshape = (POOL_GROUPS, g, g)
    return pl.pallas_call(
        body,
        name="bwd_b",
        grid=(nt,),
        in_specs=[
            _chunk_spec(ts, C_BP, rev_nt=nt), _halo_spec(ts, C_BP, rev_nt=nt), _chunk_spec(ts, C_BZ, rev_nt=nt),
            _chunk_spec(ts, 1, rev_nt=nt),
            _const_spec(wshape), _const_spec(wshape), _const_spec((1, D)), _const_spec((1, D)), _const_spec((D, D)),
            HBM_SPEC,
        ],
        out_specs=[
            _dproj_spec(ts, "b", rev_nt=nt), _const_spec((D, D)), _const_spec(wshape),
            _const_spec((1, D)), _const_spec((1, D)),
        ],
        out_shape=[
            jax.ShapeDtypeStruct((s, N_IN), CDT), jax.ShapeDtypeStruct((D, D), F32),
            jax.ShapeDtypeStruct(wshape, F32), jax.ShapeDtypeStruct((1, D), F32), jax.ShapeDtypeStruct((1, D), F32),
        ],
        scratch_shapes=[ext, ext, ext, pltpu.VMEM((ts, D), CDT), pltpu.VMEM((ts, D), F32), ext,
                        pltpu.VMEM((ts, D), CDT), pltpu.VMEM((HALO, D), F32)],
        input_output_aliases={9: 0},
        compiler_params=_cparams(("arbitrary",)),
    )(proj, proj, proj, dp, pool_w, pool_w_t, pool_b, pool_scale, wb_t, dproj)


def _bwd_c(proj, dp, conv_w, conv_b, wc_t, dproj):
    s = proj.shape[0]
    ts = min(TS_MIX, s)
    nt = s // ts

    def body(ch_ref, chh_ref, cb_ref, cc_ref, cch_ref, cz_ref, dpc_ref, cw_ref, cbias_ref, wct_ref, _dproj_in,
             dc_ref, dwc_ref, dcw_ref, dcb_ref, qe, de, carry):
        _zero_first([dwc_ref, dcw_ref, dcb_ref, carry])
        tile = nt - 1 - pl.program_id(0)
        cb = cb_ref[...].astype(F32)
        cz = cz_ref[...].astype(F32)
        f = _c_forward(ch_ref, chh_ref, cb, cc_ref, cch_ref, cz, cw_ref, cbias_ref[...], tile, qe)
        dpc = dpc_ref[...]
        dwc_ref[...] += lax.dot_general(f["yc"].astype(CDT), dpc, TN_DIMS, preferred_element_type=F32)
        dyc = _dot(dpc, wct_ref[...])
        t1 = dyc * f["conv"]
        dc_ref[:, D:2 * D] = (t1 * f["sz"]).astype(CDT)
        dc_ref[:, 3 * D:4 * D] = (t1 * cb * _silu_grad(cz, f["sg"])).astype(CDT)
        dconv = dyc * cb * f["sz"]
        r = HALO + ts
        dcb_ref[...] += _colsum(dconv)
        dcw_ref[0:1, :] += _colsum(dconv * qe[HALO - 2:r - 2, :])
        dcw_ref[1:2, :] += _colsum(dconv * qe[HALO - 1:r - 1, :])
        dcw_ref[2:3, :] += _colsum(dconv * qe[HALO:r, :])
        de[0:ts, :] = dconv
        de[ts:ts + 8, :] = carry[...]
        carry[...] = de[0:8, :]
        dq = cw_ref[2:3, :] * dconv + cw_ref[1:2, :] * de[1:ts + 1, :] + cw_ref[0:1, :] * de[2:ts + 2, :]
        dc_ref[:, 0:D] = (dq * cc_ref[...].astype(F32)).astype(CDT)
        dc_ref[:, 2 * D:3 * D] = (dq * ch_ref[...].astype(F32)).astype(CDT)

    return pl.pallas_call(
        body,
        name="bwd_c",
        grid=(nt,),
        in_specs=[
            _chunk_spec(ts, C_CH, rev_nt=nt), _halo_spec(ts, C_CH, rev_nt=nt), _chunk_spec(ts, C_CB, rev_nt=nt),
            _chunk_spec(ts, C_CC, rev_nt=nt), _halo_spec(ts, C_CC, rev_nt=nt), _chunk_spec(ts, C_CZ, rev_nt=nt),
            _chunk_spec(ts, 2, rev_nt=nt),
            _const_spec((3, D)), _const_spec((1, D)), _const_spec((D, D)), HBM_SPEC,
        ],
        out_specs=[
            _dproj_spec(ts, "c", rev_nt=nt), _const_spec((D, D)), _const_spec((8, D)),
            _const_spec((1, D)),
        ],
        out_shape=[
            jax.ShapeDtypeStruct((s, N_IN), CDT), jax.ShapeDtypeStruct((D, D), F32),
            jax.ShapeDtypeStruct((8, D), F32), jax.ShapeDtypeStruct((1, D), F32),
        ],
        scratch_shapes=[pltpu.VMEM((HALO + ts, D), F32), pltpu.VMEM((ts + 8, D), F32), pltpu.VMEM((8, D), F32)],
        input_output_aliases={10: 0},
        compiler_params=_cparams(("arbitrary",)),
    )(proj, proj, proj, proj, proj, proj, dp, conv_w, conv_b, wc_t, dproj)


TK_DX = 2 * D


def _bwd_proj_dx(dproj, w_in_t, x, g_row, dx_out, scat, gath):
    s = x.shape[0]
    ts = min(TS_PROJ, s)
    nk = N_IN // TK_DX
    nt = s // ts
    ns, nx = len(scat), len(scat) + len(gath)

    def body(*refs):
        d_ref, wt_ref, x_ref, g_ref, dxo_ref = refs[:5]
        ex_ins = refs[5:5 + nx]
        dxi_ref, dng_ref = refs[5 + nx:7 + nx]
        ex_outs = refs[7 + nx:7 + 2 * nx]
        acc = refs[7 + 2 * nx]
        sems = refs[8 + 2 * nx:]
        i, k = pl.program_id(0), pl.program_id(1)

        def exchange():
            return _exchange_copies(ex_ins[:ns], ex_outs[:ns], ex_ins[ns:], ex_outs[ns:], *sems)

        @pl.when((i == 0) & (k == 0))
        def _():
            dng_ref[...] = jnp.zeros_like(dng_ref)
            for cp in exchange():
                cp.start()

        @pl.when(k == 0)
        def _():
            acc[...] = jnp.zeros_like(acc)

        acc[...] += _dot(d_ref[...], wt_ref[...])

        @pl.when(k == nk - 1)
        def _():
            xf = x_ref[...]
            g = g_ref[...]
            dh = acc[...]
            r = lax.rsqrt(_rowmean(xf * xf) + RMS_EPS)
            xhat = xf * r
            dng_ref[...] += _colsum(dh * xhat)
            dxhat = dh * g
            dxi_ref[...] = dxo_ref[...] + r * (dxhat - xhat * _rowmean(dxhat * xhat))

        @pl.when((i == nt - 1) & (k == nk - 1))
        def _():
            for cp in exchange():
                cp.wait()

    tile = pl.BlockSpec((ts, D), lambda i, k: (i, 0))
    row = pl.BlockSpec((1, D), lambda i, k: (0, 0))
    return pl.pallas_call(
        body,
        name="bwd_proj_dx_exchange",
        grid=(nt, nk),
        in_specs=[
            pl.BlockSpec((ts, TK_DX), lambda i, k: (i, k)), pl.BlockSpec((TK_DX, D), lambda i, k: (k, 0)),
            tile, row, tile,
        ] + [HBM_SPEC] * nx,
        out_specs=[tile, row] + [HBM_SPEC] * nx,
        out_shape=[jax.ShapeDtypeStruct((s, D), F32), jax.ShapeDtypeStruct((1, D), F32)]
        + _exchange_shapes(scat, gath),
        scratch_shapes=[pltpu.VMEM((ts, D), F32)] + _exchange_sems(nx),
        compiler_params=_cparams(("arbitrary", "arbitrary")),
    )(dproj, w_in_t, x, g_row, dx_out, *scat, *gath)


TN_DW = 512


def _bwd_proj_dw(h_t, dproj):
    s = h_t.shape[1]
    tk = min(4096, s)
    per_shard = SHARD_N // TN_DW
    per_chunk = D // TN_DW
    nk = s // tk

    def body(ht_ref, d_ref, out_ref, acc):
        k = pl.program_id(1)

        @pl.when(k == 0)
        def _():
            acc[...] = jnp.zeros_like(acc)

        acc[...] += _dot(ht_ref[...], d_ref[...])

        @pl.when(k == nk - 1)
        def _():
            out_ref[...] = acc[...].astype(CDT)

    def out_index(n, k):
        chunk = n // per_chunk
        proj_chunk = sum(jnp.where(chunk == c, p, 0) for c, p in enumerate(DPROJ_PERM))
        col = proj_chunk * per_chunk + n % per_chunk
        return (col // per_shard, 0, col % per_shard)

    return pl.pallas_call(
        body,
        name="bwd_proj_dw",
        grid=(N_IN // TN_DW, nk),
        in_specs=[pl.BlockSpec((D, tk), lambda n, k: (0, k)), pl.BlockSpec((tk, TN_DW), lambda n, k: (k, n))],
        out_specs=pl.BlockSpec((None, D, TN_DW), out_index),
        out_shape=jax.ShapeDtypeStruct((NDEV, D, SHARD_N), CDT),
        scratch_shapes=[pltpu.VMEM((D, TN_DW), F32)],
        compiler_params=_cparams(("arbitrary", "arbitrary")),
    )(h_t, dproj)


def _adamw(recvs, w, m, v, tr, name):
    nq = len(recvs)
    _, rows, c = recvs[0].shape
    assert rows % tr == 0 and w.shape == (nq * rows, c)
    steps = rows // tr

    def body(*refs):
        rrefs = refs[:nq]
        w_ref, m_ref, v_ref, g_out, d_out, m_out, v_out = refs[nq:]
        q = pl.program_id(0)
        for qi in range(nq):
            @pl.when(q == qi)
            def _(r=rrefs[qi]):
                g = r[0].astype(F32)
                for j in range(1, NDEV):
                    g = g + r[j].astype(F32)
                wv = w_ref[...]
                m2 = ADAM_B1 * m_ref[...] + (1.0 - ADAM_B1) * g
                v2 = ADAM_B2 * v_ref[...] + (1.0 - ADAM_B2) * (g * g)
                m_hat = m2 / (1.0 - ADAM_B1 ** ADAM_STEP)
                v_hat = v2 / (1.0 - ADAM_B2 ** ADAM_STEP)
                g_out[...] = g
                d_out[...] = -ADAM_LR * (m_hat / (jnp.sqrt(v_hat) + ADAM_EPS) + ADAM_WD * wv)
                m_out[...] = m2
                v_out[...] = v2

    def rspec(qi):
        return pl.BlockSpec((NDEV, tr, c), lambda q, t: (0, jnp.where(q == qi, t, 0), 0))

    tile = pl.BlockSpec((tr, c), lambda q, t: (q * steps + t, 0))
    shp = jax.ShapeDtypeStruct(w.shape, F32)
    return pl.pallas_call(
        body,
        name=name,
        grid=(nq, steps),
        in_specs=[rspec(qi) for qi in range(nq)] + [tile, tile, tile],
        out_specs=[tile] * 4,
        out_shape=[shp] * 4,
        compiler_params=_cparams(("arbitrary", "arbitrary")),
    )(*recvs, w, m, v)


SMALL_ROWS = 264
REP_NAMES = ("sgu_ln_g", "sgu_ln_b", "sgu_b", "pool_b", "pool_scale", "conv_b")


def _pack_small(pool_part, conv_part):
    n = pool_part.shape[0]
    flat = jnp.concatenate([pool_part.reshape(n, -1), conv_part.reshape(n, -1)], axis=1)
    return jnp.pad(flat, ((0, 0), (0, SMALL_ROWS * 128 - flat.shape[1]))).reshape(n, SMALL_ROWS, 128)


def _unpack_small(packed):
    flat = packed.reshape(DEPTH, SMALL_ROWS * 128)
    n_pool = POOL_GROUPS * (POOL_GDIM // NDEV) * POOL_GDIM
    pool = flat[:, :n_pool].reshape(DEPTH, POOL_GROUPS, POOL_GDIM // NDEV, POOL_GDIM)
    conv = flat[:, n_pool:n_pool + 3 * (D // NDEV)].reshape(DEPTH, 3, D // NDEV)
    return pool, conv


REP_LAYER_ROWS = len(REP_NAMES) * 8 + SGU_GROUPS * SGU_BLOCK
NORM_ROWS = 8 * (DEPTH + 1)


def _pack_rep_layer(vals, sgu_w_l):
    parts = [vals[nme].reshape(8, 128) for nme in REP_NAMES]
    parts.append(sgu_w_l.reshape(SGU_GROUPS * SGU_BLOCK, SGU_BLOCK))
    return jnp.concatenate(parts, axis=0)


def _pack_rep(per_layer, sgu_w):
    vecs = jnp.stack([per_layer[nme].reshape(DEPTH, 8, 128) for nme in REP_NAMES], axis=1)
    vecs = vecs.reshape(DEPTH, 8 * len(REP_NAMES), 128)
    mats = sgu_w.reshape(DEPTH, SGU_GROUPS * SGU_BLOCK, SGU_BLOCK)
    return jnp.concatenate([vecs, mats], axis=1).reshape(DEPTH * REP_LAYER_ROWS, 128)


def _unpack_rep(packed):
    p = packed.reshape(DEPTH, REP_LAYER_ROWS, 128)
    nv = 8 * len(REP_NAMES)
    vecs = p[:, :nv].reshape(DEPTH, len(REP_NAMES), D)
    out = {nme: vecs[:, q] for q, nme in enumerate(REP_NAMES)}
    out["sgu_b"] = out["sgu_b"].reshape(DEPTH, SGU_GROUPS, SGU_BLOCK)
    return out, p[:, nv:].reshape(DEPTH, SGU_GROUPS, SGU_BLOCK, SGU_BLOCK)


def _pack_norm(norm_g_like, final_g_like):
    return jnp.concatenate([norm_g_like.reshape(DEPTH * 8, 128), final_g_like.reshape(8, 128)], axis=0)


def _unpack_norm(packed):
    return packed[:DEPTH * 8].reshape(DEPTH, D), packed[DEPTH * 8:].reshape(D)


def kernel(x, norm_g, w_in, sgu_ln_g, sgu_ln_b, sgu_w, sgu_b, pool_w, pool_b, pool_scale, conv_w, conv_b, w_branch_a, w_branch_b, w_branch_c, w_out, final_g, loss_target, m_norm_g, m_w_in, m_sgu_ln_g, m_sgu_ln_b, m_sgu_w, m_sgu_b, m_pool_w, m_pool_b, m_pool_scale, m_conv_w, m_conv_b, m_w_branch_a, m_w_branch_b, m_w_branch_c, m_w_out, m_final_g, v_norm_g, v_w_in, v_sgu_ln_g, v_sgu_ln_b, v_sgu_w, v_sgu_b, v_pool_w, v_pool_b, v_pool_scale, v_conv_w, v_conv_b, v_w_branch_a, v_w_branch_b, v_w_branch_c, v_w_out, v_final_g):
    assert x.shape[0] == 1 and x.shape[2] == D
    s = x.shape[1]
    x0 = x.reshape(s, D)
    target = loss_target.reshape(s, D)

    w_br = jnp.stack([w_branch_a, w_branch_b, w_branch_c, w_out], axis=1)
    shards = [[w_in[l].astype(CDT), w_br[l].astype(CDT), pool_w[l].astype(CDT), conv_w[l]] for l in range(DEPTH)]

    def layout(gathered):
        g_in, g_br, g_pool, g_conv = gathered
        mats = jnp.transpose(g_br, (1, 0, 2, 3)).reshape(4, D, D)
        pool_full = jnp.transpose(g_pool, (1, 0, 2, 3)).reshape(POOL_GROUPS, POOL_GDIM, POOL_GDIM)
        w_in_t = jnp.transpose(g_in, (0, 2, 1)).reshape(N_IN // D, D, D)
        return dict(
            g_in=g_in,
            w_in_t=jnp.concatenate([w_in_t[p] for p in DPROJ_PERM], axis=0),
            mats=mats, mats_t=jnp.transpose(mats, (0, 2, 1)),
            pool=pool_full, pool_t=jnp.transpose(pool_full, (0, 2, 1)),
            conv=jnp.transpose(g_conv, (1, 0, 2)).reshape(3, D),
        )

    pos = np.arange(SGU_BLOCK) // CHUNK
    mask = jnp.asarray(pos[None, :] <= pos[:, None])
    wm_all = jnp.where(mask[None, None], sgu_w, 0.0).astype(CDT)
    wm_all_t = jnp.transpose(wm_all, (0, 1, 3, 2))
    bias_all = jnp.repeat(jnp.transpose(sgu_b, (0, 2, 1)), SGU_BLOCK, axis=2)

    def row(a, l):
        return a[l].reshape(1, D)

    g_in0 = _gather_two_level(shards[0][0], "gather_w_in0")
    weights = [None] * DEPTH
    saved = []
    xl = x0
    for l in range(DEPTH):
        g_in = g_in0 if l == 0 else weights[l]["g_in"]
        behind = (shards[0][1:] if l == 0 else []) + (shards[l + 1] if l + 1 < DEPTH else [])
        proj, h_t, *got = _fwd_proj(xl, row(norm_g, l), g_in, behind)
        if l == 0:
            weights[0] = layout([g_in0] + got[:3])
            got = got[3:]
        if l + 1 < DEPTH:
            weights[l + 1] = layout(got)
        w = weights[l]
        pa = _fwd_a(proj, row(sgu_ln_g, l), row(sgu_ln_b, l), wm_all[l], bias_all[l], w["mats"][0])
        pb = _fwd_b(proj, w["pool"], row(pool_b, l), row(pool_scale, l), w["mats"][1])
        pc = _fwd_c(proj, w["conv"], row(conv_b, l), w["mats"][2])
        saved.append((xl, proj, h_t, pa, pb, pc))
        xl = _fwd_o(xl, pa, pb, pc, proj, w["mats"][3])

    dx, loss_cols, dfinal_g = _loss_head(xl, target, final_g.reshape(1, D))
    loss = lax.psum(0.5 / D * jnp.sum(loss_cols), ("x", "y", "c"))

    norm_grads = [None] * DEPTH
    recv_in, recv_br, recv_small, recv_rep = ([None] * DEPTH for _ in range(4))
    for l in reversed(range(DEPTH)):
        w = weights[l]
        xl, proj, h_t, pa, pb, pc = saved[l]
        dp, dproj, dwo = _bwd_o(dx, pa, pb, pc, proj, w["mats_t"][3])
        dproj, dwa, dwm, dbs, dlng, dlnb = _bwd_a(
            proj, dp, row(sgu_ln_g, l), row(sgu_ln_b, l), wm_all[l], wm_all_t[l], bias_all[l], w["mats_t"][0], dproj)
        dproj, dwb, dpw, dpb, dps = _bwd_b(
            proj, dp, w["pool"], w["pool_t"], row(pool_b, l), row(pool_scale, l), w["mats_t"][1], dproj)
        dproj, dwc, dcw, dcb = _bwd_c(proj, dp, w["conv"], row(conv_b, l), w["mats_t"][2], dproj)
        dw_in = _bwd_proj_dw(h_t, dproj)

        dw_br = jnp.stack([dwa, dwb, dwc, dwo]).astype(CDT).reshape(4, NDEV, D // NDEV, D).transpose(1, 0, 2, 3)
        dpw_j = dpw.reshape(POOL_GROUPS, NDEV, POOL_GDIM // NDEV, POOL_GDIM).transpose(1, 0, 2, 3)
        dcw_j = dcw[0:3].reshape(3, NDEV, D // NDEV).transpose(1, 0, 2)
        small = _pack_small(dpw_j, dcw_j)
        rep = _pack_rep_layer(
            dict(sgu_ln_g=dlng, sgu_ln_b=dlnb, sgu_b=dbs, pool_b=dpb, pool_scale=dps, conv_b=dcb),
            jnp.where(mask[None], dwm, 0.0))
        dx, norm_grads[l], recv_in[l], recv_br[l], recv_small[l], recv_rep[l] = _bwd_proj_dx(
            dproj, w["w_in_t"], xl, row(norm_g, l), dx, [dw_in, dw_br, small], [rep])

    (recv_norm,) = _exchange([], [_pack_norm(jnp.stack(norm_grads), dfinal_g)], "gather_norm_grads")

    rep_w = dict(sgu_ln_g=sgu_ln_g, sgu_ln_b=sgu_ln_b, sgu_b=sgu_b, pool_b=pool_b, pool_scale=pool_scale, conv_b=conv_b)
    rep_m = dict(sgu_ln_g=m_sgu_ln_g, sgu_ln_b=m_sgu_ln_b, sgu_b=m_sgu_b, pool_b=m_pool_b, pool_scale=m_pool_scale,
                 conv_b=m_conv_b)
    rep_v = dict(sgu_ln_g=v_sgu_ln_g, sgu_ln_b=v_sgu_ln_b, sgu_b=v_sgu_b, pool_b=v_pool_b, pool_scale=v_pool_scale,
                 conv_b=v_conv_b)
    rep_out = _adamw(recv_rep, _pack_rep(rep_w, sgu_w), _pack_rep(rep_m, m_sgu_w), _pack_rep(rep_v, v_sgu_w),
                     REP_LAYER_ROWS // 2, "adamw_replicated")
    rep_res = [_unpack_rep(o) for o in rep_out]
    norm_out = _adamw([recv_norm], _pack_norm(norm_g, final_g), _pack_norm(m_norm_g, m_final_g),
                      _pack_norm(v_norm_g, v_final_g), NORM_ROWS, "adamw_norm")
    norm_res = [_unpack_norm(o) for o in norm_out]

    in_out = _adamw(recv_in, w_in.reshape(DEPTH * D, SHARD_N), m_w_in.reshape(DEPTH * D, SHARD_N),
                    v_w_in.reshape(DEPTH * D, SHARD_N), 128, "adamw_w_in")
    in_res = [o.reshape(DEPTH, D, SHARD_N) for o in in_out]

    def stack_br(a, b, c, o):
        return jnp.stack([a, b, c, o], axis=1).reshape(DEPTH * 4 * (D // NDEV), D)

    br_out = _adamw(
        [r.reshape(NDEV, 4 * (D // NDEV), D) for r in recv_br],
        stack_br(w_branch_a, w_branch_b, w_branch_c, w_out),
        stack_br(m_w_branch_a, m_w_branch_b, m_w_branch_c, m_w_out),
        stack_br(v_w_branch_a, v_w_branch_b, v_w_branch_c, v_w_out), 128, "adamw_w_branch")
    br_res = [o.reshape(DEPTH, 4, D // NDEV, D) for o in br_out]

    def stack_small(pw, cw):
        return _pack_small(pw, cw).reshape(DEPTH * SMALL_ROWS, 128)

    small_out = _adamw(recv_small, stack_small(pool_w, conv_w), stack_small(m_pool_w, m_conv_w),
                       stack_small(v_pool_w, v_conv_w), SMALL_ROWS, "adamw_small")
    small_res = [_unpack_small(o) for o in small_out]

    def leaves(kind):
        rep, sgu_w_k = rep_res[kind]
        norm_k, final_k = norm_res[kind]
        br = br_res[kind]
        pool_k, conv_k = small_res[kind]
        return [norm_k, in_res[kind], rep["sgu_ln_g"], rep["sgu_ln_b"], sgu_w_k, rep["sgu_b"], pool_k,
                rep["pool_b"], rep["pool_scale"], conv_k, rep["conv_b"], br[:, 0], br[:, 1], br[:, 2], br[:, 3],
                final_k]

    return (loss, dx.reshape(1, s, D), *leaves(0), *leaves(1), *leaves(2), *leaves(3))
```

```python
import functools

import jax
import jax.numpy as jnp
import numpy as np
from jax import lax
from jax.experimental import pallas as pl
from jax.experimental.pallas import tpu as pltpu

F32 = jnp.float32
CDT = jnp.bfloat16

D = 1024
N_IN = 12 * D
DEPTH = 4
NDEV = 8
SHARD_N = N_IN // NDEV
SGU_BLOCK = 128
SGU_GROUPS = 8
CHUNK = 64
POOL_GROUPS = 4
POOL_GDIM = D // POOL_GROUPS
POOL_WINDOWS = (2, 4, 8, 16)
RMS_EPS = 1e-6
LN_EPS = 1e-5
GELU_K0 = float(np.sqrt(2.0 / np.pi))
GELU_K1 = 0.044715

ADAM_LR = 0.001
ADAM_B1 = 0.9
ADAM_B2 = 0.999
ADAM_EPS = 1e-08
ADAM_WD = 0.01
ADAM_STEP = 10

HALO = 32
TS_MIX = 512
TS_PROJ = 1024
VMEM_LIMIT = 56 * 1024 * 1024

C_AU, C_AV, C_AZ, C_BP, C_BZ, C_CH, C_CB, C_CC, C_CZ, C_G0 = range(10)

TN_DIMS = (((0,), (0,)), ((), ()))
NT_DIMS = (((1,), (1,)), ((), ()))


def _cparams(sem):
    return pltpu.CompilerParams(dimension_semantics=sem, vmem_limit_bytes=VMEM_LIMIT)


def _dot(a, b):
    return jnp.dot(a, b, preferred_element_type=F32)


def _gelu(x):
    t = jnp.tanh(GELU_K0 * (x + GELU_K1 * (x * x * x)))
    return 0.5 * x * (1.0 + t), t


def _gelu_grad(x, t):
    return 0.5 * (1.0 + t) + 0.5 * x * (1.0 - t * t) * (GELU_K0 * (1.0 + 3.0 * GELU_K1 * (x * x)))


def _silu(x):
    s = jax.nn.sigmoid(x)
    return x * s, s


def _silu_grad(x, s):
    return s * (1.0 + x * (1.0 - s))


def _rowmean(x):
    return jnp.mean(x, axis=-1, keepdims=True)


def _colsum(x):
    return jnp.sum(x, axis=0, keepdims=True)


def _const_spec(shape):
    nd = len(shape)
    return pl.BlockSpec(shape, lambda *_: (0,) * nd)


def _chunk_spec(ts, chunk, width=1, rev_nt=None):
    if rev_nt is None:
        return pl.BlockSpec((ts, width * D), lambda i: (i, chunk))
    return pl.BlockSpec((ts, width * D), lambda i: (rev_nt - 1 - i, chunk))


def _halo_spec(ts, chunk, rev_nt=None):
    per = ts // HALO
    if rev_nt is None:
        return pl.BlockSpec((HALO, D), lambda i: (jnp.maximum(i * per - 1, 0), chunk))
    return pl.BlockSpec((HALO, D), lambda i: (jnp.maximum((rev_nt - 1 - i) * per - 1, 0), chunk))


def _exchange_copies(scat_ins, scat_outs, gath_ins, gath_outs, send_sems, recv_sems, local_sems):
    x, y, c = lax.axis_index("x"), lax.axis_index("y"), lax.axis_index("c")
    me = 4 * x + 2 * y + c
    arrs = [(i, o, True) for i, o in zip(scat_ins, scat_outs)] + [(i, o, False) for i, o in zip(gath_ins, gath_outs)]
    copies = []
    for k in range(1, NDEV):
        px = 1 - x if k & 4 else x
        py = 1 - y if k & 2 else y
        pc = 1 - c if k & 1 else c
        peer = 4 * px + 2 * py + pc
        for a, (src, dst, scatter) in enumerate(arrs):
            copies.append(pltpu.make_async_remote_copy(
                src_ref=src.at[peer] if scatter else src,
                dst_ref=dst.at[me],
                send_sem=send_sems.at[k - 1, a],
                recv_sem=recv_sems.at[k - 1, a],
                device_id=(px, py, pc),
                device_id_type=pl.DeviceIdType.MESH,
            ))
    for a, (src, dst, scatter) in enumerate(arrs):
        copies.append(pltpu.make_async_copy(src.at[me] if scatter else src, dst.at[me], local_sems.at[a]))
    return copies


def _exchange_shapes(scat, gath):
    return ([jax.ShapeDtypeStruct(a.shape, a.dtype) for a in scat]
            + [jax.ShapeDtypeStruct((NDEV,) + a.shape, a.dtype) for a in gath])


def _exchange_sems(n):
    return [pltpu.SemaphoreType.DMA((NDEV - 1, n)), pltpu.SemaphoreType.DMA((NDEV - 1, n)),
            pltpu.SemaphoreType.DMA((n,))]


HBM_SPEC = pl.BlockSpec(memory_space=pltpu.HBM)


def _exchange(scat, gath, name):
    ns, n = len(scat), len(scat) + len(gath)

    def body(*refs):
        ins, outs, sems = refs[:n], refs[n:2 * n], refs[2 * n:]
        copies = _exchange_copies(ins[:ns], outs[:ns], ins[ns:], outs[ns:], *sems)
        for cp in copies:
            cp.start()
        for cp in copies:
            cp.wait()

    return pl.pallas_call(
        body,
        name=name,
        out_shape=_exchange_shapes(scat, gath),
        in_specs=[HBM_SPEC] * n,
        out_specs=[HBM_SPEC] * n,
        scratch_shapes=_exchange_sems(n),
    )(*scat, *gath)


def _gather_two_level(shard, name):
    def body(x_ref, out_ref, send_sems, recv_sems, local_sem):
        x, y, c = lax.axis_index("x"), lax.axis_index("y"), lax.axis_index("c")
        me, sibling = (x, y, c), (x, y, 1 - c)
        chips = [(1 - x, y), (x, 1 - y), (1 - x, 1 - y)]

        def slot(px, py, pc):
            return out_ref.at[4 * px + 2 * py + pc]

        def copy(k, block, to, src=None):
            return pltpu.make_async_remote_copy(
                src_ref=slot(*block) if src is None else src,
                dst_ref=slot(*block),
                send_sem=send_sems.at[k],
                recv_sem=recv_sems.at[k],
                device_id=to,
                device_id_type=pl.DeviceIdType.MESH,
            )

        mine = pltpu.make_async_copy(x_ref, slot(*me), local_sem)
        mine.start()
        first = [copy(0, me, sibling, src=x_ref)]
        first += [copy(1 + j, me, (*chip, c), src=x_ref) for j, chip in enumerate(chips)]
        for cp in first:
            cp.start()
        passed = [copy(4 + j, (*chip, c), sibling) for j, chip in enumerate(chips)]
        for j, chip in enumerate(chips):
            copy(1 + j, (*chip, c), me).wait_recv()
            passed[j].start()
        copy(0, sibling, me).wait_recv()
        for j, chip in enumerate(chips):
            copy(4 + j, (*chip, 1 - c), me).wait_recv()
        for cp in first + passed:
            cp.wait_send()
        mine.wait()

    return pl.pallas_call(
        body,
        name=name,
        out_shape=jax.ShapeDtypeStruct((NDEV,) + shard.shape, shard.dtype),
        in_specs=[HBM_SPEC],
        out_specs=HBM_SPEC,
        scratch_shapes=[pltpu.SemaphoreType.DMA((NDEV - 1,)), pltpu.SemaphoreType.DMA((NDEV - 1,)),
                        pltpu.SemaphoreType.DMA],
    )(shard)


SHARDS_PER_STEP = 2


def _fwd_proj(x, g_row, w_gathered, next_shards=()):
    s = x.shape[0]
    ts = min(TS_PROJ, s)
    ng = len(next_shards)
    nt = s // ts

    def body(*refs):
        x_ref, g_ref, w_ref = refs[:3]
        gath_ins = refs[3:3 + ng]
        proj_ref, ht_ref = refs[3 + ng:5 + ng]
        gath_outs = refs[5 + ng:5 + 2 * ng]
        h_scr = refs[5 + 2 * ng]
        sems = refs[6 + 2 * ng:]
        i, j = pl.program_id(0), pl.program_id(1)

        if ng:
            @pl.when((i == 0) & (j == 0))
            def _():
                for cp in _exchange_copies((), (), gath_ins, gath_outs, *sems):
                    cp.start()

        @pl.when(j == 0)
        def _():
            xf = x_ref[...]
            h = xf * lax.rsqrt(_rowmean(xf * xf) + RMS_EPS) * g_ref[...]
            h_scr[...] = h.astype(CDT)
            ht_ref[...] = h.T.astype(CDT)

        for q in range(SHARDS_PER_STEP):
            proj_ref[:, q * SHARD_N:(q + 1) * SHARD_N] = _dot(h_scr[...], w_ref[q]).astype(CDT)

        if ng:
            @pl.when((i == nt - 1) & (j == nj - 1))
            def _():
                for cp in _exchange_copies((), (), gath_ins, gath_outs, *sems):
                    cp.wait()

    nj = NDEV // SHARDS_PER_STEP
    return pl.pallas_call(
        body,
        name="fwd_proj_gather" if ng else "fwd_proj",
        grid=(nt, nj),
        in_specs=[
            pl.BlockSpec((ts, D), lambda i, j: (i, 0)),
            pl.BlockSpec((1, D), lambda i, j: (0, 0)),
            pl.BlockSpec((SHARDS_PER_STEP, D, SHARD_N), lambda i, j: (j, 0, 0)),
        ] + [HBM_SPEC] * ng,
        out_specs=[
            pl.BlockSpec((ts, SHARDS_PER_STEP * SHARD_N), lambda i, j: (i, j)),
            pl.BlockSpec((D, ts), lambda i, j: (0, i)),
        ] + [HBM_SPEC] * ng,
        out_shape=[jax.ShapeDtypeStruct((s, N_IN), CDT), jax.ShapeDtypeStruct((D, s), CDT)]
        + _exchange_shapes((), next_shards),
        scratch_shapes=[pltpu.VMEM((ts, D), CDT)] + (_exchange_sems(ng) if ng else []),
        compiler_params=_cparams(("arbitrary", "arbitrary")),
    )(x, g_row, w_gathered, *next_shards)


def _a_forward(au, av, az, lng, lnb, wm_ref, bias_ref, v_scr, mix_scr):
    ts = au.shape[0]
    u, tu = _gelu(au)
    vg, tv = _gelu(av)
    xc = vg - _rowmean(vg)
    rstd = lax.rsqrt(_rowmean(xc * xc) + LN_EPS)
    vhat = xc * rstd
    v_scr[...] = (vhat * lng + lnb).astype(CDT)
    for n in range(ts // SGU_BLOCK):
        rows = slice(n * SGU_BLOCK, (n + 1) * SGU_BLOCK)
        for g in range(SGU_GROUPS):
            cols = slice(g * SGU_BLOCK, (g + 1) * SGU_BLOCK)
            mix_scr[rows, cols] = _dot(wm_ref[g], v_scr[rows, cols]) + bias_ref[:, cols]
    mixed = mix_scr[...]
    sz, sg = _silu(az)
    return dict(u=u, tu=tu, tv=tv, rstd=rstd, vhat=vhat, mixed=mixed, sz=sz, sg=sg, ya=u * mixed * sz)


def _fwd_a(proj, lng, lnb, wm, bias_full, wa):
    s = proj.shape[0]
    ts = min(TS_MIX, s)

    def body(au_ref, av_ref, az_ref, lng_ref, lnb_ref, wm_ref, bias_ref, wa_ref, pa_ref, v_scr, mix_scr):
        f = _a_forward(au_ref[...].astype(F32), av_ref[...].astype(F32), az_ref[...].astype(F32),
                       lng_ref[...], lnb_ref[...], wm_ref, bias_ref, v_scr, mix_scr)
        pa_ref[...] = _dot(f["ya"].astype(CDT), wa_ref[...]).astype(CDT)

    return pl.pallas_call(
        body,
        name="fwd_a",
        grid=(s // ts,),
        in_specs=[
            _chunk_spec(ts, C_AU), _chunk_spec(ts, C_AV), _chunk_spec(ts, C_AZ),
            _const_spec((1, D)), _const_spec((1, D)),
            _const_spec((SGU_GROUPS, SGU_BLOCK, SGU_BLOCK)), _const_spec((SGU_BLOCK, D)),
            _const_spec((D, D)),
        ],
        out_specs=pl.BlockSpec((ts, D), lambda i: (i, 0)),
        out_shape=jax.ShapeDtypeStruct((s, D), CDT),
        scratch_shapes=[pltpu.VMEM((ts, D), CDT), pltpu.VMEM((ts, D), F32)],
        compiler_params=_cparams(("arbitrary",)),
    )(proj, proj, proj, lng, lnb, wm, bias_full, wa)


def _pool_d(pe, sa, sb, tile, ts):
    r = HALO + ts
    g = POOL_GDIM
    sa[8:r, :] = pe[8:r, :] + pe[7:r - 1, :]
    sb[16:r, g:] = sa[16:r, g:] + sa[14:r - 2, g:]
    sa[24:r, 2 * g:] = sb[24:r, 2 * g:] + sb[20:r - 4, 2 * g:]
    sb[32:r, 3 * g:] = sa[32:r, 3 * g:] + sa[24:r - 8, 3 * g:]
    pos1 = tile * ts + lax.broadcasted_iota(jnp.int32, (ts, 1), 0) + 1
    invs = [1.0 / jnp.minimum(pos1, w).astype(F32) for w in POOL_WINDOWS]
    sums = [sa[HALO:r, 0:g], sb[HALO:r, g:2 * g], sa[HALO:r, 2 * g:3 * g], sb[HALO:r, 3 * g:]]
    d = [sums[k] * invs[k] - pe[HALO:r, k * g:(k + 1) * g] for k in range(POOL_GROUPS)]
    return d, invs


def _b_forward(bp_ref, bph_ref, bz, pw_ref, pb, pscale, tile, pe, sa, sb, d_scr, lin_scr):
    ts = bz.shape[0]
    pe[0:HALO, :] = jnp.where(tile > 0, bph_ref[...].astype(F32), 0.0)
    pe[HALO:, :] = bp_ref[...].astype(F32)
    d, invs = _pool_d(pe, sa, sb, tile, ts)
    for k in range(POOL_GROUPS):
        cols = slice(k * POOL_GDIM, (k + 1) * POOL_GDIM)
        d_scr[:, cols] = d[k].astype(CDT)
        lin_scr[:, cols] = _dot(d_scr[:, cols], pw_ref[k])
    lin = lin_scr[...] + pb
    sz, sg = _silu(bz)
    return dict(lin=lin, sz=sz, sg=sg, invs=invs, yb=lin * pscale * sz)


def _fwd_b(proj, pool_w, pool_b, pool_scale, wb):
    s = proj.shape[0]
    ts = min(TS_MIX, s)

    def body(bp_ref, bph_ref, bz_ref, pw_ref, pb_ref, ps_ref, wb_ref, out_ref, pe, sa, sb, d_scr, lin_scr):
        f = _b_forward(bp_ref, bph_ref, bz_ref[...].astype(F32), pw_ref, pb_ref[...], ps_ref[...],
                       pl.program_id(0), pe, sa, sb, d_scr, lin_scr)
        out_ref[...] = _dot(f["yb"].astype(CDT), wb_ref[...]).astype(CDT)

    ext = pltpu.VMEM((HALO + ts, D), F32)
    return pl.pallas_call(
        body,
        name="fwd_b",
        grid=(s // ts,),
        in_specs=[
            _chunk_spec(ts, C_BP), _halo_spec(ts, C_BP), _chunk_spec(ts, C_BZ),
            _const_spec((POOL_GROUPS, POOL_GDIM, POOL_GDIM)), _const_spec((1, D)), _const_spec((1, D)),
            _const_spec((D, D)),
        ],
        out_specs=pl.BlockSpec((ts, D), lambda i: (i, 0)),
        out_shape=jax.ShapeDtypeStruct((s, D), CDT),
        scratch_shapes=[ext, ext, ext, pltpu.VMEM((ts, D), CDT), pltpu.VMEM((ts, D), F32)],
        compiler_params=_cparams(("arbitrary",)),
    )(proj, proj, proj, pool_w, pool_b, pool_scale, wb)


def _c_forward(ch_ref, chh_ref, cb, cc_ref, cch_ref, cz, cw_ref, cbias, tile, qe):
    ts = cb.shape[0]
    r = HALO + ts
    qe[0:HALO, :] = jnp.where(tile > 0, chh_ref[...].astype(F32) * cch_ref[...].astype(F32), 0.0)
    qe[HALO:, :] = cc_ref[...].astype(F32) * ch_ref[...].astype(F32)
    conv = (cw_ref[0:1, :] * qe[HALO - 2:r - 2, :] + cw_ref[1:2, :] * qe[HALO - 1:r - 1, :]
            + cw_ref[2:3, :] * qe[HALO:r, :]) + cbias
    sz, sg = _silu(cz)
    return dict(conv=conv, sz=sz, sg=sg, yc=cb * conv * sz)


def _fwd_c(proj, conv_w, conv_b, wc):
    s = proj.shape[0]
    ts = min(TS_MIX, s)

    def body(ch_ref, chh_ref, cb_ref, cc_ref, cch_ref, cz_ref, cw_ref, cbias_ref, wc_ref, out_ref, qe):
        f = _c_forward(ch_ref, chh_ref, cb_ref[...].astype(F32), cc_ref, cch_ref, cz_ref[...].astype(F32),
                       cw_ref, cbias_ref[...], pl.program_id(0), qe)
        out_ref[...] = _dot(f["yc"].astype(CDT), wc_ref[...]).astype(CDT)

    return pl.pallas_call(
        body,
        name="fwd_c",
        grid=(s // ts,),
        in_specs=[
            _chunk_spec(ts, C_CH), _halo_spec(ts, C_CH), _chunk_spec(ts, C_CB),
            _chunk_spec(ts, C_CC), _halo_spec(ts, C_CC), _chunk_spec(ts, C_CZ),
            _const_spec((3, D)), _const_spec((1, D)), _const_spec((D, D)),
        ],
        out_specs=pl.BlockSpec((ts, D), lambda i: (i, 0)),
        out_shape=jax.ShapeDtypeStruct((s, D), CDT),
        scratch_shapes=[pltpu.VMEM((HALO + ts, D), F32)],
        compiler_params=_cparams(("arbitrary",)),
    )(proj, proj, proj, proj, proj, proj, conv_w, conv_b, wc)


def _merge(pa, pb, pc, logits):
    gates = [jax.nn.sigmoid(logits[:, k * D:(k + 1) * D].astype(F32)) for k in range(3)]
    ps = [pa.astype(F32), pb.astype(F32), pc.astype(F32)]
    merged = gates[0] * ps[0] + gates[1] * ps[1] + gates[2] * ps[2]
    return gates, ps, merged


def _fwd_o(x, pa, pb, pc, proj, wo):
    s = x.shape[0]
    ts = min(TS_MIX, s)

    def body(x_ref, pa_ref, pb_ref, pc_ref, lg_ref, wo_ref, out_ref):
        _, _, merged = _merge(pa_ref[...], pb_ref[...], pc_ref[...], lg_ref[...])
        out_ref[...] = x_ref[...] + _dot(merged.astype(CDT), wo_ref[...])

    tile = pl.BlockSpec((ts, D), lambda i: (i, 0))
    return pl.pallas_call(
        body,
        name="fwd_o",
        grid=(s // ts,),
        in_specs=[tile, tile, tile, tile, _chunk_spec(ts, 3, width=3), _const_spec((D, D))],
        out_specs=tile,
        out_shape=jax.ShapeDtypeStruct((s, D), F32),
        compiler_params=_cparams(("arbitrary",)),
    )(x, pa, pb, pc, proj, wo)


def _loss_head(x, target, fg_row):
    s = x.shape[0]
    ts = min(TS_MIX, s)

    def body(x_ref, t_ref, g_ref, dx_ref, loss_ref, dg_ref):
        @pl.when(pl.program_id(0) == 0)
        def _():
            loss_ref[...] = jnp.zeros_like(loss_ref)
            dg_ref[...] = jnp.zeros_like(dg_ref)

        xf = x_ref[...]
        g = g_ref[...]
        r = lax.rsqrt(_rowmean(xf * xf) + RMS_EPS)
        xhat = xf * r
        err = xhat * g - t_ref[...]
        loss_ref[...] += _colsum(err * err)
        dy = err * (1.0 / D)
        dg_ref[...] += _colsum(dy * xhat)
        dxhat = dy * g
        dx_ref[...] = r * (dxhat - xhat * _rowmean(dxhat * xhat))

    tile = pl.BlockSpec((ts, D), lambda i: (i, 0))
    row = _const_spec((1, D))
    return pl.pallas_call(
        body,
        name="loss_head",
        grid=(s // ts,),
        in_specs=[tile, tile, row],
        out_specs=[tile, row, row],
        out_shape=[jax.ShapeDtypeStruct((s, D), F32), jax.ShapeDtypeStruct((1, D), F32),
                   jax.ShapeDtypeStruct((1, D), F32)],
        compiler_params=_cparams(("arbitrary",)),
    )(x, target, fg_row)


def _zero_first(refs):
    @pl.when(pl.program_id(0) == 0)
    def _():
        for r in refs:
            r[...] = jnp.zeros_like(r)


DPROJ_BLOCK = dict(c=(4, 0), b=(2, 2), a=(3, 2), g=(3, 3))
DPROJ_PERM = (5, 6, 7, 8, 3, 4, 0, 1, 2, 9, 10, 11)


def _dproj_spec(ts, part, rev_nt=None):
    width, block = DPROJ_BLOCK[part]
    if rev_nt is None:
        return pl.BlockSpec((ts, width * D), lambda i: (i, block))
    return pl.BlockSpec((ts, width * D), lambda i: (rev_nt - 1 - i, block))


def _bwd_o(dx, pa, pb, pc, proj, wo_t):
    s = dx.shape[0]
    ts = min(TS_MIX, s)

    def body(dx_ref, pa_ref, pb_ref, pc_ref, lg_ref, wot_ref, dp_ref, dg_ref, dwo_ref):
        _zero_first([dwo_ref])
        gates, ps, merged = _merge(pa_ref[...], pb_ref[...], pc_ref[...], lg_ref[...])
        dxb = dx_ref[...].astype(CDT)
        dwo_ref[...] += lax.dot_general(merged.astype(CDT), dxb, TN_DIMS, preferred_element_type=F32)
        dmerged = _dot(dxb, wot_ref[...])
        for k in range(3):
            cols = slice(k * D, (k + 1) * D)
            dpk = dmerged * gates[k]
            dp_ref[:, cols] = dpk.astype(CDT)
            dg_ref[:, cols] = (dpk * ps[k] * (1.0 - gates[k])).astype(CDT)

    tile = pl.BlockSpec((ts, D), lambda i: (i, 0))
    wide = pl.BlockSpec((ts, 3 * D), lambda i: (i, 0))
    return pl.pallas_call(
        body,
        name="bwd_o",
        grid=(s // ts,),
        in_specs=[tile, tile, tile, tile, _chunk_spec(ts, 3, width=3), _const_spec((D, D))],
        out_specs=[wide, _dproj_spec(ts, "g"), _const_spec((D, D))],
        out_shape=[jax.ShapeDtypeStruct((s, 3 * D), CDT), jax.ShapeDtypeStruct((s, N_IN), CDT),
                   jax.ShapeDtypeStruct((D, D), F32)],
        compiler_params=_cparams(("arbitrary",)),
    )(dx, pa, pb, pc, proj, wo_t)


def _bwd_a(proj, dp, lng, lnb, wm, wm_t, bias_full, wa_t, dproj):
    s = proj.shape[0]
    ts = min(TS_MIX, s)
    nt = s // ts

    def body(au_ref, av_ref, az_ref, dpa_ref, lng_ref, lnb_ref, wm_ref, wmt_ref, bias_ref, wat_ref, _dproj_in,
             da_ref, dwa_ref, dwm_ref, dbs_ref, dlng_ref, dlnb_ref,
             v_scr, mix_scr, dm_scr, dmf_scr, dv_scr, bsacc):
        _zero_first([dwa_ref, dwm_ref, dlng_ref, dlnb_ref, bsacc])
        au = au_ref[...].astype(F32)
        av = av_ref[...].astype(F32)
        az = az_ref[...].astype(F32)
        lng = lng_ref[...]
        f = _a_forward(au, av, az, lng, lnb_ref[...], wm_ref, bias_ref, v_scr, mix_scr)
        dpa = dpa_ref[...]
        dwa_ref[...] += lax.dot_general(f["ya"].astype(CDT), dpa, TN_DIMS, preferred_element_type=F32)
        dya = _dot(dpa, wat_ref[...])
        t1 = dya * f["mixed"]
        da_ref[:, 0:D] = (t1 * f["sz"] * _gelu_grad(au, f["tu"])).astype(CDT)
        da_ref[:, 2 * D:3 * D] = (t1 * f["u"] * _silu_grad(az, f["sg"])).astype(CDT)
        dmix = dya * f["u"] * f["sz"]
        dmf_scr[...] = dmix
        dm_scr[...] = dmix.astype(CDT)
        for n in range(ts // SGU_BLOCK):
            rows = slice(n * SGU_BLOCK, (n + 1) * SGU_BLOCK)
            for g in range(SGU_GROUPS):
                cols = slice(g * SGU_BLOCK, (g + 1) * SGU_BLOCK)
                dmb = dm_scr[rows, cols]
                dwm_ref[g] += lax.dot_general(dmb, v_scr[rows, cols], NT_DIMS, preferred_element_type=F32)
                dv_scr[rows, cols] = _dot(wmt_ref[g], dmb)
                bsacc[g] += dmf_scr[rows, cols]
        dvln = dv_scr[...]
        vhat = f["vhat"]
        dlng_ref[...] += _colsum(dvln * vhat)
        dlnb_ref[...] += _colsum(dvln)
        dvhat = dvln * lng
        dvg = f["rstd"] * (dvhat - _rowmean(dvhat) - vhat * _rowmean(dvhat * vhat))
        da_ref[:, D:2 * D] = (dvg * _gelu_grad(av, f["tv"])).astype(CDT)

        @pl.when(pl.program_id(0) == nt - 1)
        def _():
            ones = jnp.ones((8, SGU_BLOCK), F32)
            for g in range(SGU_GROUPS):
                red = lax.dot_general(ones, bsacc[g], NT_DIMS, preferred_element_type=F32,
                                      precision=lax.Precision.HIGHEST)
                dbs_ref[g:g + 1, :] = red[0:1, :]

    gshape = (SGU_GROUPS, SGU_BLOCK, SGU_BLOCK)
    return pl.pallas_call(
        body,
        name="bwd_a",
        grid=(nt,),
        in_specs=[
            _chunk_spec(ts, C_AU), _chunk_spec(ts, C_AV), _chunk_spec(ts, C_AZ), _chunk_spec(ts, 0),
            _const_spec((1, D)), _const_spec((1, D)), _const_spec(gshape), _const_spec(gshape),
            _const_spec((SGU_BLOCK, D)), _const_spec((D, D)), HBM_SPEC,
        ],
        out_specs=[
            _dproj_spec(ts, "a"), _const_spec((D, D)), _const_spec(gshape),
            _const_spec((SGU_GROUPS, SGU_BLOCK)), _const_spec((1, D)), _const_spec((1, D)),
        ],
        out_shape=[
            jax.ShapeDtypeStruct((s, N_IN), CDT), jax.ShapeDtypeStruct((D, D), F32),
            jax.ShapeDtypeStruct(gshape, F32), jax.ShapeDtypeStruct((SGU_GROUPS, SGU_BLOCK), F32),
            jax.ShapeDtypeStruct((1, D), F32), jax.ShapeDtypeStruct((1, D), F32),
        ],
        scratch_shapes=[
            pltpu.VMEM((ts, D), CDT), pltpu.VMEM((ts, D), F32), pltpu.VMEM((ts, D), CDT),
            pltpu.VMEM((ts, D), F32), pltpu.VMEM((ts, D), F32), pltpu.VMEM(gshape, F32),
        ],
        input_output_aliases={10: 0},
        compiler_params=_cparams(("arbitrary",)),
    )(proj, proj, proj, dp, lng, lnb, wm, wm_t, bias_full, wa_t, dproj)


def _bwd_b(proj, dp, pool_w, pool_w_t, pool_b, pool_scale, wb_t, dproj):
    s = proj.shape[0]
    ts = min(TS_MIX, s)
    nt = s // ts
    g = POOL_GDIM

    def body(bp_ref, bph_ref, bz_ref, dpb_ref, pw_ref, pwt_ref, pb_ref, ps_ref, wbt_ref, _dproj_in,
             db_ref, dwb_ref, dpw_ref, dpb_out_ref, dps_ref,
             pe, sa, sb, d_scr, lin_scr, ee, dl_scr, carry):
        _zero_first([dwb_ref, dpw_ref, dpb_out_ref, dps_ref, carry])
        tile = nt - 1 - pl.program_id(0)
        bz = bz_ref[...].astype(F32)
        pscale = ps_ref[...]
        f = _b_forward(bp_ref, bph_ref, bz, pw_ref, pb_ref[...], pscale, tile, pe, sa, sb, d_scr, lin_scr)
        dpb = dpb_ref[...]
        dwb_ref[...] += lax.dot_general(f["yb"].astype(CDT), dpb, TN_DIMS, preferred_element_type=F32)
        dyb = _dot(dpb, wbt_ref[...])
        t1 = dyb * f["lin"]
        dps_ref[...] += _colsum(t1 * f["sz"])
        db_ref[:, D:2 * D] = (t1 * pscale * _silu_grad(bz, f["sg"])).astype(CDT)
        dlin = dyb * pscale * f["sz"]
        dpb_out_ref[...] += _colsum(dlin)
        dl_scr[...] = dlin.astype(CDT)
        r = HALO + ts
        ee[ts:r, :] = carry[...]
        for k in range(POOL_GROUPS):
            cols = slice(k * g, (k + 1) * g)
            dlk = dl_scr[:, cols]
            dpw_ref[k] += lax.dot_general(d_scr[:, cols], dlk, TN_DIMS, preferred_element_type=F32)
            dd = _dot(dlk, pwt_ref[k])
            lin_scr[:, cols] = dd
            ee[0:ts, cols] = dd * f["invs"][k]
        carry[...] = ee[0:HALO, :]
        sa[0:ts + 24, :] = ee[0:ts + 24, :] + ee[1:ts + 25, :]
        sb[0:ts + 16, g:] = sa[0:ts + 16, g:] + sa[2:ts + 18, g:]
        sa[0:ts + 8, 2 * g:] = sb[0:ts + 8, 2 * g:] + sb[4:ts + 12, 2 * g:]
        sb[0:ts, 3 * g:] = sa[0:ts, 3 * g:] + sa[8:ts + 8, 3 * g:]
        sums = [sa[0:ts, 0:g], sb[0:ts, g:2 * g], sa[0:ts, 2 * g:3 * g], sb[0:ts, 3 * g:]]
        for k in range(POOL_GROUPS):
            cols = slice(k * g, (k + 1) * g)
            db_ref[:, cols] = (sums[k] - lin_scr[:, cols]).astype(CDT)

    ext = pltpu.VMEM((HALO + ts, D), F32)
    wshape = (POOL_GROUPS, g, g)
    return pl.pallas_call(
        body,
        name="bwd_b",
        grid=(nt,),
        in_specs=[
            _chunk_spec(ts, C_BP, rev_nt=nt), _halo_spec(ts, C_BP, rev_nt=nt), _chunk_spec(ts, C_BZ, rev_nt=nt),
            _chunk_spec(ts, 1, rev_nt=nt),
            _const_spec(wshape), _const_spec(wshape), _const_spec((1, D)), _const_spec((1, D)), _const_spec((D, D)),
            HBM_SPEC,
        ],
        out_specs=[
            _dproj_spec(ts, "b", rev_nt=nt), _const_spec((D, D)), _const_spec(wshape),
            _const_spec((1, D)), _const_spec((1, D)),
        ],
        out_shape=[
            jax.ShapeDtypeStruct((s, N_IN), CDT), jax.ShapeDtypeStruct((D, D), F32),
            jax.ShapeDtypeStruct(wshape, F32), jax.ShapeDtypeStruct((1, D), F32), jax.ShapeDtypeStruct((1, D), F32),
        ],
        scratch_shapes=[ext, ext, ext, pltpu.VMEM((ts, D), CDT), pltpu.VMEM((ts, D), F32), ext,
                        pltpu.VMEM((ts, D), CDT), pltpu.VMEM((HALO, D), F32)],
        input_output_aliases={9: 0},
        compiler_params=_cparams(("arbitrary",)),
    )(proj, proj, proj, dp, pool_w, pool_w_t, pool_b, pool_scale, wb_t, dproj)


def _bwd_c(proj, dp, conv_w, conv_b, wc_t, dproj):
    s = proj.shape[0]
    ts = min(TS_MIX, s)
    nt = s // ts

    def body(ch_ref, chh_ref, cb_ref, cc_ref, cch_ref, cz_ref, dpc_ref, cw_ref, cbias_ref, wct_ref, _dproj_in,
             dc_ref, dwc_ref, dcw_ref, dcb_ref, qe, de, carry):
        _zero_first([dwc_ref, dcw_ref, dcb_ref, carry])
        tile = nt - 1 - pl.program_id(0)
        cb = cb_ref[...].astype(F32)
        cz = cz_ref[...].astype(F32)
        f = _c_forward(ch_ref, chh_ref, cb, cc_ref, cch_ref, cz, cw_ref, cbias_ref[...], tile, qe)
        dpc = dpc_ref[...]
        dwc_ref[...] += lax.dot_general(f["yc"].astype(CDT), dpc, TN_DIMS, preferred_element_type=F32)
        dyc = _dot(dpc, wct_ref[...])
        t1 = dyc * f["conv"]
        dc_ref[:, D:2 * D] = (t1 * f["sz"]).astype(CDT)
        dc_ref[:, 3 * D:4 * D] = (t1 * cb * _silu_grad(cz, f["sg"])).astype(CDT)
        dconv = dyc * cb * f["sz"]
        r = HALO + ts
        dcb_ref[...] += _colsum(dconv)
        dcw_ref[0:1, :] += _colsum(dconv * qe[HALO - 2:r - 2, :])
        dcw_ref[1:2, :] += _colsum(dconv * qe[HALO - 1:r - 1, :])
        dcw_ref[2:3, :] += _colsum(dconv * qe[HALO:r, :])
        de[0:ts, :] = dconv
        de[ts:ts + 8, :] = carry[...]
        carry[...] = de[0:8, :]
        dq = cw_ref[2:3, :] * dconv + cw_ref[1:2, :] * de[1:ts + 1, :] + cw_ref[0:1, :] * de[2:ts + 2, :]
        dc_ref[:, 0:D] = (dq * cc_ref[...].astype(F32)).astype(CDT)
        dc_ref[:, 2 * D:3 * D] = (dq * ch_ref[...].astype(F32)).astype(CDT)

    return pl.pallas_call(
        body,
        name="bwd_c",
        grid=(nt,),
        in_specs=[
            _chunk_spec(ts, C_CH, rev_nt=nt), _halo_spec(ts, C_CH, rev_nt=nt), _chunk_spec(ts, C_CB, rev_nt=nt),
            _chunk_spec(ts, C_CC, rev_nt=nt), _halo_spec(ts, C_CC, rev_nt=nt), _chunk_spec(ts, C_CZ, rev_nt=nt),
            _chunk_spec(ts, 2, rev_nt=nt),
            _const_spec((3, D)), _const_spec((1, D)), _const_spec((D, D)), HBM_SPEC,
        ],
        out_specs=[
            _dproj_spec(ts, "c", rev_nt=nt), _const_spec((D, D)), _const_spec((8, D)),
            _const_spec((1, D)),
        ],
        out_shape=[
            jax.ShapeDtypeStruct((s, N_IN), CDT), jax.ShapeDtypeStruct((D, D), F32),
            jax.ShapeDtypeStruct((8, D), F32), jax.ShapeDtypeStruct((1, D), F32),
        ],
        scratch_shapes=[pltpu.VMEM((HALO + ts, D), F32), pltpu.VMEM((ts + 8, D), F32), pltpu.VMEM((8, D), F32)],
        input_output_aliases={10: 0},
        compiler_params=_cparams(("arbitrary",)),
    )(proj, proj, proj, proj, proj, proj, dp, conv_w, conv_b, wc_t, dproj)


TK_DX = 2 * D


def _bwd_proj_dx(dproj, w_in_t, x, g_row, dx_out, scat, gath):
    s = x.shape[0]
    ts = min(TS_PROJ, s)
    nk = N_IN // TK_DX
    nt = s // ts
    ns, nx = len(scat), len(scat) + len(gath)

    def body(*refs):
        d_ref, wt_ref, x_ref, g_ref, dxo_ref = refs[:5]
        ex_ins = refs[5:5 + nx]
        dxi_ref, dng_ref = refs[5 + nx:7 + nx]
        ex_outs = refs[7 + nx:7 + 2 * nx]
        acc = refs[7 + 2 * nx]
        sems = refs[8 + 2 * nx:]
        i, k = pl.program_id(0), pl.program_id(1)

        def exchange():
            return _exchange_copies(ex_ins[:ns], ex_outs[:ns], ex_ins[ns:], ex_outs[ns:], *sems)

        @pl.when((i == 0) & (k == 0))
        def _():
            dng_ref[...] = jnp.zeros_like(dng_ref)
            for cp in exchange():
                cp.start()

        @pl.when(k == 0)
        def _():
            acc[...] = jnp.zeros_like(acc)

        acc[...] += _dot(d_ref[...], wt_ref[...])

        @pl.when(k == nk - 1)
        def _():
            xf = x_ref[...]
            g = g_ref[...]
            dh = acc[...]
            r = lax.rsqrt(_rowmean(xf * xf) + RMS_EPS)
            xhat = xf * r
            dng_ref[...] += _colsum(dh * xhat)
            dxhat = dh * g
            dxi_ref[...] = dxo_ref[...] + r * (dxhat - xhat * _rowmean(dxhat * xhat))

        @pl.when((i == nt - 1) & (k == nk - 1))
        def _():
            for cp in exchange():
                cp.wait()

    tile = pl.BlockSpec((ts, D), lambda i, k: (i, 0))
    row = pl.BlockSpec((1, D), lambda i, k: (0, 0))
    return pl.pallas_call(
        body,
        name="bwd_proj_dx_exchange",
        grid=(nt, nk),
        in_specs=[
            pl.BlockSpec((ts, TK_DX), lambda i, k: (i, k)), pl.BlockSpec((TK_DX, D), lambda i, k: (k, 0)),
            tile, row, tile,
        ] + [HBM_SPEC] * nx,
        out_specs=[tile, row] + [HBM_SPEC] * nx,
        out_shape=[jax.ShapeDtypeStruct((s, D), F32), jax.ShapeDtypeStruct((1, D), F32)]
        + _exchange_shapes(scat, gath),
        scratch_shapes=[pltpu.VMEM((ts, D), F32)] + _exchange_sems(nx),
        compiler_params=_cparams(("arbitrary", "arbitrary")),
    )(dproj, w_in_t, x, g_row, dx_out, *scat, *gath)


TN_DW = 512


def _bwd_proj_dw(h_t, dproj):
    s = h_t.shape[1]
    tk = min(8192, s)
    per_shard = SHARD_N // TN_DW
    per_chunk = D // TN_DW
    nk = s // tk

    def body(ht_ref, d_ref, out_ref, acc):
        k = pl.program_id(1)

        @pl.when(k == 0)
        def _():
            acc[...] = jnp.zeros_like(acc)

        acc[...] += _dot(ht_ref[...], d_ref[...])

        @pl.when(k == nk - 1)
        def _():
            out_ref[...] = acc[...].astype(CDT)

    def out_index(n, k):
        chunk = n // per_chunk
        proj_chunk = sum(jnp.where(chunk == c, p, 0) for c, p in enumerate(DPROJ_PERM))
        col = proj_chunk * per_chunk + n % per_chunk
        return (col // per_shard, 0, col % per_shard)

    return pl.pallas_call(
        body,
        name="bwd_proj_dw",
        grid=(N_IN // TN_DW, nk),
        in_specs=[pl.BlockSpec((D, tk), lambda n, k: (0, k)), pl.BlockSpec((tk, TN_DW), lambda n, k: (k, n))],
        out_specs=pl.BlockSpec((None, D, TN_DW), out_index),
        out_shape=jax.ShapeDtypeStruct((NDEV, D, SHARD_N), CDT),
        scratch_shapes=[pltpu.VMEM((D, TN_DW), F32)],
        compiler_params=_cparams(("arbitrary", "arbitrary")),
    )(h_t, dproj)


def _adamw(recvs, w, m, v, tr, name):
    nq = len(recvs)
    _, rows, c = recvs[0].shape
    assert rows % tr == 0 and w.shape == (nq * rows, c)
    steps = rows // tr

    def body(*refs):
        rrefs = refs[:nq]
        w_ref, m_ref, v_ref, g_out, d_out, m_out, v_out = refs[nq:]
        q = pl.program_id(0)
        for qi in range(nq):
            @pl.when(q == qi)
            def _(r=rrefs[qi]):
                g = r[0].astype(F32)
                for j in range(1, NDEV):
                    g = g + r[j].astype(F32)
                wv = w_ref[...]
                m2 = ADAM_B1 * m_ref[...] + (1.0 - ADAM_B1) * g
                v2 = ADAM_B2 * v_ref[...] + (1.0 - ADAM_B2) * (g * g)
                m_hat = m2 / (1.0 - ADAM_B1 ** ADAM_STEP)
                v_hat = v2 / (1.0 - ADAM_B2 ** ADAM_STEP)
                g_out[...] = g
                d_out[...] = -ADAM_LR * (m_hat / (jnp.sqrt(v_hat) + ADAM_EPS) + ADAM_WD * wv)
                m_out[...] = m2
                v_out[...] = v2

    def rspec(qi):
        return pl.BlockSpec((NDEV, tr, c), lambda q, t: (0, jnp.where(q == qi, t, 0), 0))

    tile = pl.BlockSpec((tr, c), lambda q, t: (q * steps + t, 0))
    shp = jax.ShapeDtypeStruct(w.shape, F32)
    return pl.pallas_call(
        body,
        name=name,
        grid=(nq, steps),
        in_specs=[rspec(qi) for qi in range(nq)] + [tile, tile, tile],
        out_specs=[tile] * 4,
        out_shape=[shp] * 4,
        compiler_params=_cparams(("arbitrary", "arbitrary")),
    )(*recvs, w, m, v)


SMALL_ROWS = 264
REP_NAMES = ("sgu_ln_g", "sgu_ln_b", "sgu_b", "pool_b", "pool_scale", "conv_b")


def _pack_small(pool_part, conv_part):
    n = pool_part.shape[0]
    flat = jnp.concatenate([pool_part.reshape(n, -1), conv_part.reshape(n, -1)], axis=1)
    return jnp.pad(flat, ((0, 0), (0, SMALL_ROWS * 128 - flat.shape[1]))).reshape(n, SMALL_ROWS, 128)


def _unpack_small(packed):
    flat = packed.reshape(DEPTH, SMALL_ROWS * 128)
    n_pool = POOL_GROUPS * (POOL_GDIM // NDEV) * POOL_GDIM
    pool = flat[:, :n_pool].reshape(DEPTH, POOL_GROUPS, POOL_GDIM // NDEV, POOL_GDIM)
    conv = flat[:, n_pool:n_pool + 3 * (D // NDEV)].reshape(DEPTH, 3, D // NDEV)
    return pool, conv


REP_LAYER_ROWS = len(REP_NAMES) * 8 + SGU_GROUPS * SGU_BLOCK
NORM_ROWS = 8 * (DEPTH + 1)


def _pack_rep_layer(vals, sgu_w_l):
    parts = [vals[nme].reshape(8, 128) for nme in REP_NAMES]
    parts.append(sgu_w_l.reshape(SGU_GROUPS * SGU_BLOCK, SGU_BLOCK))
    return jnp.concatenate(parts, axis=0)


def _pack_rep(per_layer, sgu_w):
    vecs = jnp.stack([per_layer[nme].reshape(DEPTH, 8, 128) for nme in REP_NAMES], axis=1)
    vecs = vecs.reshape(DEPTH, 8 * len(REP_NAMES), 128)
    mats = sgu_w.reshape(DEPTH, SGU_GROUPS * SGU_BLOCK, SGU_BLOCK)
    return jnp.concatenate([vecs, mats], axis=1).reshape(DEPTH * REP_LAYER_ROWS, 128)


def _unpack_rep(packed):
    p = packed.reshape(DEPTH, REP_LAYER_ROWS, 128)
    nv = 8 * len(REP_NAMES)
    vecs = p[:, :nv].reshape(DEPTH, len(REP_NAMES), D)
    out = {nme: vecs[:, q] for q, nme in enumerate(REP_NAMES)}
    out["sgu_b"] = out["sgu_b"].reshape(DEPTH, SGU_GROUPS, SGU_BLOCK)
    return out, p[:, nv:].reshape(DEPTH, SGU_GROUPS, SGU_BLOCK, SGU_BLOCK)


def _pack_norm(norm_g_like, final_g_like):
    return jnp.concatenate([norm_g_like.reshape(DEPTH * 8, 128), final_g_like.reshape(8, 128)], axis=0)


def _unpack_norm(packed):
    return packed[:DEPTH * 8].reshape(DEPTH, D), packed[DEPTH * 8:].reshape(D)


def kernel(x, norm_g, w_in, sgu_ln_g, sgu_ln_b, sgu_w, sgu_b, pool_w, pool_b, pool_scale, conv_w, conv_b, w_branch_a, w_branch_b, w_branch_c, w_out, final_g, loss_target, m_norm_g, m_w_in, m_sgu_ln_g, m_sgu_ln_b, m_sgu_w, m_sgu_b, m_pool_w, m_pool_b, m_pool_scale, m_conv_w, m_conv_b, m_w_branch_a, m_w_branch_b, m_w_branch_c, m_w_out, m_final_g, v_norm_g, v_w_in, v_sgu_ln_g, v_sgu_ln_b, v_sgu_w, v_sgu_b, v_pool_w, v_pool_b, v_pool_scale, v_conv_w, v_conv_b, v_w_branch_a, v_w_branch_b, v_w_branch_c, v_w_out, v_final_g):
    assert x.shape[0] == 1 and x.shape[2] == D
    s = x.shape[1]
    x0 = x.reshape(s, D)
    target = loss_target.reshape(s, D)

    w_br = jnp.stack([w_branch_a, w_branch_b, w_branch_c, w_out], axis=1)
    shards = [[w_in[l].astype(CDT), w_br[l].astype(CDT), pool_w[l].astype(CDT), conv_w[l]] for l in range(DEPTH)]

    def layout(gathered):
        g_in, g_br, g_pool, g_conv = gathered
        mats = jnp.transpose(g_br, (1, 0, 2, 3)).reshape(4, D, D)
        pool_full = jnp.transpose(g_pool, (1, 0, 2, 3)).reshape(POOL_GROUPS, POOL_GDIM, POOL_GDIM)
        w_in_t = jnp.transpose(g_in, (0, 2, 1)).reshape(N_IN // D, D, D)
        return dict(
            g_in=g_in,
            w_in_t=jnp.concatenate([w_in_t[p] for p in DPROJ_PERM], axis=0),
            mats=mats, mats_t=jnp.transpose(mats, (0, 2, 1)),
            pool=pool_full, pool_t=jnp.transpose(pool_full, (0, 2, 1)),
            conv=jnp.transpose(g_conv, (1, 0, 2)).reshape(3, D),
        )

    pos = np.arange(SGU_BLOCK) // CHUNK
    mask = jnp.asarray(pos[None, :] <= pos[:, None])
    wm_all = jnp.where(mask[None, None], sgu_w, 0.0).astype(CDT)
    wm_all_t = jnp.transpose(wm_all, (0, 1, 3, 2))
    bias_all = jnp.repeat(jnp.transpose(sgu_b, (0, 2, 1)), SGU_BLOCK, axis=2)

    def row(a, l):
        return a[l].reshape(1, D)

    g_in0 = _gather_two_level(shards[0][0], "gather_w_in0")
    weights = [None] * DEPTH
    saved = []
    xl = x0
    for l in range(DEPTH):
        g_in = g_in0 if l == 0 else weights[l]["g_in"]
        behind = (shards[0][1:] if l == 0 else []) + (shards[l + 1] if l + 1 < DEPTH else [])
        proj, h_t, *got = _fwd_proj(xl, row(norm_g, l), g_in, behind)
        if l == 0:
            weights[0] = layout([g_in0] + got[:3])
            got = got[3:]
        if l + 1 < DEPTH:
            weights[l + 1] = layout(got)
        w = weights[l]
        pa = _fwd_a(proj, row(sgu_ln_g, l), row(sgu_ln_b, l), wm_all[l], bias_all[l], w["mats"][0])
        pb = _fwd_b(proj, w["pool"], row(pool_b, l), row(pool_scale, l), w["mats"][1])
        pc = _fwd_c(proj, w["conv"], row(conv_b, l), w["mats"][2])
        saved.append((xl, proj, h_t, pa, pb, pc))
        xl = _fwd_o(xl, pa, pb, pc, proj, w["mats"][3])

    dx, loss_cols, dfinal_g = _loss_head(xl, target, final_g.reshape(1, D))
    loss = lax.psum(0.5 / D * jnp.sum(loss_cols), ("x", "y", "c"))

    norm_grads = [None] * DEPTH
    recv_in, recv_br, recv_small, recv_rep = ([None] * DEPTH for _ in range(4))
    for l in reversed(range(DEPTH)):
        w = weights[l]
        xl, proj, h_t, pa, pb, pc = saved[l]
        dp, dproj, dwo = _bwd_o(dx, pa, pb, pc, proj, w["mats_t"][3])
        dproj, dwa, dwm, dbs, dlng, dlnb = _bwd_a(
            proj, dp, row(sgu_ln_g, l), row(sgu_ln_b, l), wm_all[l], wm_all_t[l], bias_all[l], w["mats_t"][0], dproj)
        dproj, dwb, dpw, dpb, dps = _bwd_b(
            proj, dp, w["pool"], w["pool_t"], row(pool_b, l), row(pool_scale, l), w["mats_t"][1], dproj)
        dproj, dwc, dcw, dcb = _bwd_c(proj, dp, w["conv"], row(conv_b, l), w["mats_t"][2], dproj)
        dw_in = _bwd_proj_dw(h_t, dproj)

        dw_br = jnp.stack([dwa, dwb, dwc, dwo]).astype(CDT).reshape(4, NDEV, D // NDEV, D).transpose(1, 0, 2, 3)
        dpw_j = dpw.reshape(POOL_GROUPS, NDEV, POOL_GDIM // NDEV, POOL_GDIM).transpose(1, 0, 2, 3)
        dcw_j = dcw[0:3].reshape(3, NDEV, D // NDEV).transpose(1, 0, 2)
        small = _pack_small(dpw_j, dcw_j)
        rep = _pack_rep_layer(
            dict(sgu_ln_g=dlng, sgu_ln_b=dlnb, sgu_b=dbs, pool_b=dpb, pool_scale=dps, conv_b=dcb),
            jnp.where(mask[None], dwm, 0.0))
        dx, norm_grads[l], recv_in[l], recv_br[l], recv_small[l], recv_rep[l] = _bwd_proj_dx(
            dproj, w["w_in_t"], xl, row(norm_g, l), dx, [dw_in, dw_br, small], [rep])

    (recv_norm,) = _exchange([], [_pack_norm(jnp.stack(norm_grads), dfinal_g)], "gather_norm_grads")

    rep_w = dict(sgu_ln_g=sgu_ln_g, sgu_ln_b=sgu_ln_b, sgu_b=sgu_b, pool_b=pool_b, pool_scale=pool_scale, conv_b=conv_b)
    rep_m = dict(sgu_ln_g=m_sgu_ln_g, sgu_ln_b=m_sgu_ln_b, sgu_b=m_sgu_b, pool_b=m_pool_b, pool_scale=m_pool_scale,
                 conv_b=m_conv_b)
    rep_v = dict(sgu_ln_g=v_sgu_ln_g, sgu_ln_b=v_sgu_ln_b, sgu_b=v_sgu_b, pool_b=v_pool_b, pool_scale=v_pool_scale,
                 conv_b=v_conv_b)
    rep_out = _adamw(recv_rep, _pack_rep(rep_w, sgu_w), _pack_rep(rep_m, m_sgu_w), _pack_rep(rep_v, v_sgu_w),
                     REP_LAYER_ROWS // 2, "adamw_replicated")
    rep_res = [_unpack_rep(o) for o in rep_out]
    norm_out = _adamw([recv_norm], _pack_norm(norm_g, final_g), _pack_norm(m_norm_g, m_final_g),
                      _pack_norm(v_norm_g, v_final_g), NORM_ROWS, "adamw_norm")
    norm_res = [_unpack_norm(o) for o in norm_out]

    in_out = _adamw(recv_in, w_in.reshape(DEPTH * D, SHARD_N), m_w_in.reshape(DEPTH * D, SHARD_N),
                    v_w_in.reshape(DEPTH * D, SHARD_N), 128, "adamw_w_in")
    in_res = [o.reshape(DEPTH, D, SHARD_N) for o in in_out]

    def stack_br(a, b, c, o):
        return jnp.stack([a, b, c, o], axis=1).reshape(DEPTH * 4 * (D // NDEV), D)

    br_out = _adamw(
        [r.reshape(NDEV, 4 * (D // NDEV), D) for r in recv_br],
        stack_br(w_branch_a, w_branch_b, w_branch_c, w_out),
        stack_br(m_w_branch_a, m_w_branch_b, m_w_branch_c, m_w_out),
        stack_br(v_w_branch_a, v_w_branch_b, v_w_branch_c, v_w_out), 128, "adamw_w_branch")
    br_res = [o.reshape(DEPTH, 4, D // NDEV, D) for o in br_out]

    def stack_small(pw, cw):
        return _pack_small(pw, cw).reshape(DEPTH * SMALL_ROWS, 128)

    small_out = _adamw(recv_small, stack_small(pool_w, conv_w), stack_small(m_pool_w, m_conv_w),
                       stack_small(v_pool_w, v_conv_w), SMALL_ROWS, "adamw_small")
    small_res = [_unpack_small(o) for o in small_out]

    def leaves(kind):
        rep, sgu_w_k = rep_res[kind]
        norm_k, final_k = norm_res[kind]
        br = br_res[kind]
        pool_k, conv_k = small_res[kind]
        return [norm_k, in_res[kind], rep["sgu_ln_g"], rep["sgu_ln_b"], sgu_w_k, rep["sgu_b"], pool_k,
                rep["pool_b"], rep["pool_scale"], conv_k, rep["conv_b"], br[:, 0], br[:, 1], br[:, 2], br[:, 3],
                final_k]

    return (loss, dx.reshape(1, s, D), *leaves(0), *leaves(1), *leaves(2), *leaves(3))
```

```python
import functools

import jax
import jax.numpy as jnp
import numpy as np
from jax import lax
from jax.experimental import pallas as pl
from jax.experimental.pallas import tpu as pltpu

F32 = jnp.float32
CDT = jnp.bfloat16

D = 1024
N_IN = 12 * D
DEPTH = 4
NDEV = 8
SHARD_N = N_IN // NDEV
SGU_BLOCK = 128
SGU_GROUPS = 8
CHUNK = 64
POOL_GROUPS = 4
POOL_GDIM = D // POOL_GROUPS
POOL_WINDOWS = (2, 4, 8, 16)
RMS_EPS = 1e-6
LN_EPS = 1e-5
GELU_K0 = float(np.sqrt(2.0 / np.pi))
GELU_K1 = 0.044715

ADAM_LR = 0.001
ADAM_B1 = 0.9
ADAM_B2 = 0.999
ADAM_EPS = 1e-08
ADAM_WD = 0.01
ADAM_STEP = 10

HALO = 32
TS_MIX = 512
TS_PROJ = 1024
VMEM_LIMIT = 56 * 1024 * 1024

C_AU, C_AV, C_AZ, C_BP, C_BZ, C_CH, C_CB, C_CC, C_CZ, C_G0 = range(10)

TN_DIMS = (((0,), (0,)), ((), ()))
NT_DIMS = (((1,), (1,)), ((), ()))


def _cparams(sem):
    return pltpu.CompilerParams(dimension_semantics=sem, vmem_limit_bytes=VMEM_LIMIT)


def _dot(a, b):
    return jnp.dot(a, b, preferred_element_type=F32)


def _gelu(x):
    t = jnp.tanh(GELU_K0 * (x + GELU_K1 * (x * x * x)))
    return 0.5 * x * (1.0 + t), t


def _gelu_grad(x, t):
    return 0.5 * (1.0 + t) + 0.5 * x * (1.0 - t * t) * (GELU_K0 * (1.0 + 3.0 * GELU_K1 * (x * x)))


def _silu(x):
    s = jax.nn.sigmoid(x)
    return x * s, s


def _silu_grad(x, s):
    return s * (1.0 + x * (1.0 - s))


def _rowmean(x):
    return jnp.mean(x, axis=-1, keepdims=True)


def _colsum(x):
    return jnp.sum(x, axis=0, keepdims=True)


def _const_spec(shape):
    nd = len(shape)
    return pl.BlockSpec(shape, lambda *_: (0,) * nd)


def _chunk_spec(ts, chunk, width=1, rev_nt=None):
    if rev_nt is None:
        return pl.BlockSpec((ts, width * D), lambda i: (i, chunk))
    return pl.BlockSpec((ts, width * D), lambda i: (rev_nt - 1 - i, chunk))


def _halo_spec(ts, chunk, rev_nt=None):
    per = ts // HALO
    if rev_nt is None:
        return pl.BlockSpec((HALO, D), lambda i: (jnp.maximum(i * per - 1, 0), chunk))
    return pl.BlockSpec((HALO, D), lambda i: (jnp.maximum((rev_nt - 1 - i) * per - 1, 0), chunk))


def _exchange_copies(scat_ins, scat_outs, gath_ins, gath_outs, send_sems, recv_sems, local_sems):
    x, y, c = lax.axis_index("x"), lax.axis_index("y"), lax.axis_index("c")
    me = 4 * x + 2 * y + c
    arrs = [(i, o, True) for i, o in zip(scat_ins, scat_outs)] + [(i, o, False) for i, o in zip(gath_ins, gath_outs)]
    copies = []
    for k in range(1, NDEV):
        px = 1 - x if k & 4 else x
        py = 1 - y if k & 2 else y
        pc = 1 - c if k & 1 else c
        peer = 4 * px + 2 * py + pc
        for a, (src, dst, scatter) in enumerate(arrs):
            copies.append(pltpu.make_async_remote_copy(
                src_ref=src.at[peer] if scatter else src,
                dst_ref=dst.at[me],
                send_sem=send_sems.at[k - 1, a],
                recv_sem=recv_sems.at[k - 1, a],
                device_id=(px, py, pc),
                device_id_type=pl.DeviceIdType.MESH,
            ))
    for a, (src, dst, scatter) in enumerate(arrs):
        copies.append(pltpu.make_async_copy(src.at[me] if scatter else src, dst.at[me], local_sems.at[a]))
    return copies


def _exchange_shapes(scat, gath):
    return ([jax.ShapeDtypeStruct(a.shape, a.dtype) for a in scat]
            + [jax.ShapeDtypeStruct((NDEV,) + a.shape, a.dtype) for a in gath])


def _exchange_sems(n):
    return [pltpu.SemaphoreType.DMA((NDEV - 1, n)), pltpu.SemaphoreType.DMA((NDEV - 1, n)),
            pltpu.SemaphoreType.DMA((n,))]


HBM_SPEC = pl.BlockSpec(memory_space=pltpu.HBM)


def _exchange(scat, gath, name):
    ns, n = len(scat), len(scat) + len(gath)

    def body(*refs):
        ins, outs, sems = refs[:n], refs[n:2 * n], refs[2 * n:]
        copies = _exchange_copies(ins[:ns], outs[:ns], ins[ns:], outs[ns:], *sems)
        for cp in copies:
            cp.start()
        for cp in copies:
            cp.wait()

    return pl.pallas_call(
        body,
        name=name,
        out_shape=_exchange_shapes(scat, gath),
        in_specs=[HBM_SPEC] * n,
        out_specs=[HBM_SPEC] * n,
        scratch_shapes=_exchange_sems(n),
    )(*scat, *gath)


def _gather_two_level(shard, name):
    def body(x_ref, out_ref, send_sems, recv_sems, local_sem):
        x, y, c = lax.axis_index("x"), lax.axis_index("y"), lax.axis_index("c")
        me, sibling = (x, y, c), (x, y, 1 - c)
        chips = [(1 - x, y), (x, 1 - y), (1 - x, 1 - y)]

        def slot(px, py, pc):
            return out_ref.at[4 * px + 2 * py + pc]

        def copy(k, block, to, src=None):
            return pltpu.make_async_remote_copy(
                src_ref=slot(*block) if src is None else src,
                dst_ref=slot(*block),
                send_sem=send_sems.at[k],
                recv_sem=recv_sems.at[k],
                device_id=to,
                device_id_type=pl.DeviceIdType.MESH,
            )

        mine = pltpu.make_async_copy(x_ref, slot(*me), local_sem)
        mine.start()
        first = [copy(0, me, sibling, src=x_ref)]
        first += [copy(1 + j, me, (*chip, c), src=x_ref) for j, chip in enumerate(chips)]
        for cp in first:
            cp.start()
        passed = [copy(4 + j, (*chip, c), sibling) for j, chip in enumerate(chips)]
        for j, chip in enumerate(chips):
            copy(1 + j, (*chip, c), me).wait_recv()
            passed[j].start()
        copy(0, sibling, me).wait_recv()
        for j, chip in enumerate(chips):
            copy(4 + j, (*chip, 1 - c), me).wait_recv()
        for cp in first + passed:
            cp.wait_send()
        mine.wait()

    return pl.pallas_call(
        body,
        name=name,
        out_shape=jax.ShapeDtypeStruct((NDEV,) + shard.shape, shard.dtype),
        in_specs=[HBM_SPEC],
        out_specs=HBM_SPEC,
        scratch_shapes=[pltpu.SemaphoreType.DMA((NDEV - 1,)), pltpu.SemaphoreType.DMA((NDEV - 1,)),
                        pltpu.SemaphoreType.DMA],
    )(shard)


SHARDS_PER_STEP = 2


def _fwd_proj(x, g_row, w_gathered, next_shards=()):
    s = x.shape[0]
    ts = min(TS_PROJ, s)
    ng = len(next_shards)
    nt = s // ts

    def body(*refs):
        x_ref, g_ref, w_ref = refs[:3]
        gath_ins = refs[3:3 + ng]
        proj_ref, ht_ref = refs[3 + ng:5 + ng]
        gath_outs = refs[5 + ng:5 + 2 * ng]
        h_scr = refs[5 + 2 * ng]
        sems = refs[6 + 2 * ng:]
        i, j = pl.program_id(0), pl.program_id(1)

        if ng:
            @pl.when((i == 0) & (j == 0))
            def _():
                for cp in _exchange_copies((), (), gath_ins, gath_outs, *sems):
                    cp.start()

        @pl.when(j == 0)
        def _():
            xf = x_ref[...]
            h = xf * lax.rsqrt(_rowmean(xf * xf) + RMS_EPS) * g_ref[...]
            h_scr[...] = h.astype(CDT)
            ht_ref[...] = h.T.astype(CDT)

        for q in range(SHARDS_PER_STEP):
            proj_ref[:, q * SHARD_N:(q + 1) * SHARD_N] = _dot(h_scr[...], w_ref[q]).astype(CDT)

        if ng:
            @pl.when((i == nt - 1) & (j == nj - 1))
            def _():
                for cp in _exchange_copies((), (), gath_ins, gath_outs, *sems):
                    cp.wait()

    nj = NDEV // SHARDS_PER_STEP
    return pl.pallas_call(
        body,
        name="fwd_proj_gather" if ng else "fwd_proj",
        grid=(nt, nj),
        in_specs=[
            pl.BlockSpec((ts, D), lambda i, j: (i, 0)),
            pl.BlockSpec((1, D), lambda i, j: (0, 0)),
            pl.BlockSpec((SHARDS_PER_STEP, D, SHARD_N), lambda i, j: (j, 0, 0)),
        ] + [HBM_SPEC] * ng,
        out_specs=[
            pl.BlockSpec((ts, SHARDS_PER_STEP * SHARD_N), lambda i, j: (i, j)),
            pl.BlockSpec((D, ts), lambda i, j: (0, i)),
        ] + [HBM_SPEC] * ng,
        out_shape=[jax.ShapeDtypeStruct((s, N_IN), CDT), jax.ShapeDtypeStruct((D, s), CDT)]
        + _exchange_shapes((), next_shards),
        scratch_shapes=[pltpu.VMEM((ts, D), CDT)] + (_exchange_sems(ng) if ng else []),
        compiler_params=_cparams(("arbitrary", "arbitrary")),
    )(x, g_row, w_gathered, *next_shards)


def _a_forward(au, av, az, lng, lnb, wm_ref, bias_ref, v_scr, mix_scr):
    ts = au.shape[0]
    u, tu = _gelu(au)
    vg, tv = _gelu(av)
    xc = vg - _rowmean(vg)
    rstd = lax.rsqrt(_rowmean(xc * xc) + LN_EPS)
    vhat = xc * rstd
    v_scr[...] = (vhat * lng + lnb).astype(CDT)
    for n in range(ts // SGU_BLOCK):
        rows = slice(n * SGU_BLOCK, (n + 1) * SGU_BLOCK)
        for g in range(SGU_GROUPS):
            cols = slice(g * SGU_BLOCK, (g + 1) * SGU_BLOCK)
            mix_scr[rows, cols] = _dot(wm_ref[g], v_scr[rows, cols]) + bias_ref[:, cols]
    mixed = mix_scr[...]
    sz, sg = _silu(az)
    return dict(u=u, tu=tu, tv=tv, rstd=rstd, vhat=vhat, mixed=mixed, sz=sz, sg=sg, ya=u * mixed * sz)


def _fwd_a(proj, lng, lnb, wm, bias_full, wa):
    s = proj.shape[0]
    ts = min(TS_MIX, s)

    def body(au_ref, av_ref, az_ref, lng_ref, lnb_ref, wm_ref, bias_ref, wa_ref, pa_ref, v_scr, mix_scr):
        f = _a_forward(au_ref[...].astype(F32), av_ref[...].astype(F32), az_ref[...].astype(F32),
                       lng_ref[...], lnb_ref[...], wm_ref, bias_ref, v_scr, mix_scr)
        pa_ref[...] = _dot(f["ya"].astype(CDT), wa_ref[...]).astype(CDT)

    return pl.pallas_call(
        body,
        name="fwd_a",
        grid=(s // ts,),
        in_specs=[
            _chunk_spec(ts, C_AU), _chunk_spec(ts, C_AV), _chunk_spec(ts, C_AZ),
            _const_spec((1, D)), _const_spec((1, D)),
            _const_spec((SGU_GROUPS, SGU_BLOCK, SGU_BLOCK)), _const_spec((SGU_BLOCK, D)),
            _const_spec((D, D)),
        ],
        out_specs=pl.BlockSpec((ts, D), lambda i: (i, 0)),
        out_shape=jax.ShapeDtypeStruct((s, D), CDT),
        scratch_shapes=[pltpu.VMEM((ts, D), CDT), pltpu.VMEM((ts, D), F32)],
        compiler_params=_cparams(("arbitrary",)),
    )(proj, proj, proj, lng, lnb, wm, bias_full, wa)


def _pool_d(pe, sa, sb, tile, ts):
    r = HALO + ts
    g = POOL_GDIM
    sa[8:r, :] = pe[8:r, :] + pe[7:r - 1, :]
    sb[16:r, g:] = sa[16:r, g:] + sa[14:r - 2, g:]
    sa[24:r, 2 * g:] = sb[24:r, 2 * g:] + sb[20:r - 4, 2 * g:]
    sb[32:r, 3 * g:] = sa[32:r, 3 * g:] + sa[24:r - 8, 3 * g:]
    pos1 = tile * ts + lax.broadcasted_iota(jnp.int32, (ts, 1), 0) + 1
    invs = [1.0 / jnp.minimum(pos1, w).astype(F32) for w in POOL_WINDOWS]
    sums = [sa[HALO:r, 0:g], sb[HALO:r, g:2 * g], sa[HALO:r, 2 * g:3 * g], sb[HALO:r, 3 * g:]]
    d = [sums[k] * invs[k] - pe[HALO:r, k * g:(k + 1) * g] for k in range(POOL_GROUPS)]
    return d, invs


def _b_forward(bp_ref, bph_ref, bz, pw_ref, pb, pscale, tile, pe, sa, sb, d_scr, lin_scr):
    ts = bz.shape[0]
    pe[0:HALO, :] = jnp.where(tile > 0, bph_ref[...].astype(F32), 0.0)
    pe[HALO:, :] = bp_ref[...].astype(F32)
    d, invs = _pool_d(pe, sa, sb, tile, ts)
    for k in range(POOL_GROUPS):
        cols = slice(k * POOL_GDIM, (k + 1) * POOL_GDIM)
        d_scr[:, cols] = d[k].astype(CDT)
        lin_scr[:, cols] = _dot(d_scr[:, cols], pw_ref[k])
    lin = lin_scr[...] + pb
    sz, sg = _silu(bz)
    return dict(lin=lin, sz=sz, sg=sg, invs=invs, yb=lin * pscale * sz)


def _fwd_b(proj, pool_w, pool_b, pool_scale, wb):
    s = proj.shape[0]
    ts = min(TS_MIX, s)

    def body(bp_ref, bph_ref, bz_ref, pw_ref, pb_ref, ps_ref, wb_ref, out_ref, pe, sa, sb, d_scr, lin_scr):
        f = _b_forward(bp_ref, bph_ref, bz_ref[...].astype(F32), pw_ref, pb_ref[...], ps_ref[...],
                       pl.program_id(0), pe, sa, sb, d_scr, lin_scr)
        out_ref[...] = _dot(f["yb"].astype(CDT), wb_ref[...]).astype(CDT)

    ext = pltpu.VMEM((HALO + ts, D), F32)
    return pl.pallas_call(
        body,
        name="fwd_b",
        grid=(s // ts,),
        in_specs=[
            _chunk_spec(ts, C_BP), _halo_spec(ts, C_BP), _chunk_spec(ts, C_BZ),
            _const_spec((POOL_GROUPS, POOL_GDIM, POOL_GDIM)), _const_spec((1, D)), _const_spec((1, D)),
            _const_spec((D, D)),
        ],
        out_specs=pl.BlockSpec((ts, D), lambda i: (i, 0)),
        out_shape=jax.ShapeDtypeStruct((s, D), CDT),
        scratch_shapes=[ext, ext, ext, pltpu.VMEM((ts, D), CDT), pltpu.VMEM((ts, D), F32)],
        compiler_params=_cparams(("arbitrary",)),
    )(proj, proj, proj, pool_w, pool_b, pool_scale, wb)


def _c_forward(ch_ref, chh_ref, cb, cc_ref, cch_ref, cz, cw_ref, cbias, tile, qe):
    ts = cb.shape[0]
    r = HALO + ts
    qe[0:HALO, :] = jnp.where(tile > 0, chh_ref[...].astype(F32) * cch_ref[...].astype(F32), 0.0)
    qe[HALO:, :] = cc_ref[...].astype(F32) * ch_ref[...].astype(F32)
    conv = (cw_ref[0:1, :] * qe[HALO - 2:r - 2, :] + cw_ref[1:2, :] * qe[HALO - 1:r - 1, :]
            + cw_ref[2:3, :] * qe[HALO:r, :]) + cbias
    sz, sg = _silu(cz)
    return dict(conv=conv, sz=sz, sg=sg, yc=cb * conv * sz)


def _merge(pa, pb, pc, logits):
    gates = [jax.nn.sigmoid(logits[:, k * D:(k + 1) * D].astype(F32)) for k in range(3)]
    ps = [pa.astype(F32), pb.astype(F32), pc.astype(F32)]
    merged = gates[0] * ps[0] + gates[1] * ps[1] + gates[2] * ps[2]
    return gates, ps, merged


def _fwd_c_o(proj, conv_w, conv_b, wc, x, pa, pb, wo):
    s = proj.shape[0]
    ts = min(TS_MIX, s)

    def body(ch_ref, chh_ref, cb_ref, cc_ref, cch_ref, cz_ref, cw_ref, cbias_ref, wc_ref,
             x_ref, pa_ref, pb_ref, lg_ref, wo_ref, pc_ref, out_ref, qe):
        f = _c_forward(ch_ref, chh_ref, cb_ref[...].astype(F32), cc_ref, cch_ref, cz_ref[...].astype(F32),
                       cw_ref, cbias_ref[...], pl.program_id(0), qe)
        pc = _dot(f["yc"].astype(CDT), wc_ref[...]).astype(CDT)
        pc_ref[...] = pc
        _, _, merged = _merge(pa_ref[...], pb_ref[...], pc, lg_ref[...])
        out_ref[...] = x_ref[...] + _dot(merged.astype(CDT), wo_ref[...])

    tile = pl.BlockSpec((ts, D), lambda i: (i, 0))
    return pl.pallas_call(
        body,
        name="fwd_c_o",
        grid=(s // ts,),
        in_specs=[
            _chunk_spec(ts, C_CH), _halo_spec(ts, C_CH), _chunk_spec(ts, C_CB),
            _chunk_spec(ts, C_CC), _halo_spec(ts, C_CC), _chunk_spec(ts, C_CZ),
            _const_spec((3, D)), _const_spec((1, D)), _const_spec((D, D)),
            tile, tile, tile, _chunk_spec(ts, 3, width=3), _const_spec((D, D)),
        ],
        out_specs=[tile, tile],
        out_shape=[jax.ShapeDtypeStruct((s, D), CDT), jax.ShapeDtypeStruct((s, D), F32)],
        scratch_shapes=[pltpu.VMEM((HALO + ts, D), F32)],
        compiler_params=_cparams(("arbitrary",)),
    )(proj, proj, proj, proj, proj, proj, conv_w, conv_b, wc, x, pa, pb, proj, wo)


def _loss_head(x, target, fg_row):
    s = x.shape[0]
    ts = min(TS_MIX, s)

    def body(x_ref, t_ref, g_ref, dx_ref, loss_ref, dg_ref):
        @pl.when(pl.program_id(0) == 0)
        def _():
            loss_ref[...] = jnp.zeros_like(loss_ref)
            dg_ref[...] = jnp.zeros_like(dg_ref)

        xf = x_ref[...]
        g = g_ref[...]
        r = lax.rsqrt(_rowmean(xf * xf) + RMS_EPS)
        xhat = xf * r
        err = xhat * g - t_ref[...]
        loss_ref[...] += _colsum(err * err)
        dy = err * (1.0 / D)
        dg_ref[...] += _colsum(dy * xhat)
        dxhat = dy * g
        dx_ref[...] = r * (dxhat - xhat * _rowmean(dxhat * xhat))

    tile = pl.BlockSpec((ts, D), lambda i: (i, 0))
    row = _const_spec((1, D))
    return pl.pallas_call(
        body,
        name="loss_head",
        grid=(s // ts,),
        in_specs=[tile, tile, row],
        out_specs=[tile, row, row],
        out_shape=[jax.ShapeDtypeStruct((s, D), F32), jax.ShapeDtypeStruct((1, D), F32),
                   jax.ShapeDtypeStruct((1, D), F32)],
        compiler_params=_cparams(("arbitrary",)),
    )(x, target, fg_row)


def _zero_first(refs):
    @pl.when(pl.program_id(0) == 0)
    def _():
        for r in refs:
            r[...] = jnp.zeros_like(r)


DPROJ_BLOCK = dict(c=(4, 0), b=(2, 2), a=(3, 2), g=(3, 3))
DPROJ_PERM = (5, 6, 7, 8, 3, 4, 0, 1, 2, 9, 10, 11)


def _dproj_spec(ts, part, rev_nt=None):
    width, block = DPROJ_BLOCK[part]
    if rev_nt is None:
        return pl.BlockSpec((ts, width * D), lambda i: (i, block))
    return pl.BlockSpec((ts, width * D), lambda i: (rev_nt - 1 - i, block))


def _bwd_o(dx, pa, pb, pc, proj, wo_t):
    s = dx.shape[0]
    ts = min(TS_MIX, s)

    def body(dx_ref, pa_ref, pb_ref, pc_ref, lg_ref, wot_ref, dp_ref, dg_ref, dwo_ref):
        _zero_first([dwo_ref])
        gates, ps, merged = _merge(pa_ref[...], pb_ref[...], pc_ref[...], lg_ref[...])
        dxb = dx_ref[...].astype(CDT)
        dwo_ref[...] += lax.dot_general(merged.astype(CDT), dxb, TN_DIMS, preferred_element_type=F32)
        dmerged = _dot(dxb, wot_ref[...])
        for k in range(3):
            cols = slice(k * D, (k + 1) * D)
            dpk = dmerged * gates[k]
            dp_ref[:, cols] = dpk.astype(CDT)
            dg_ref[:, cols] = (dpk * ps[k] * (1.0 - gates[k])).astype(CDT)

    tile = pl.BlockSpec((ts, D), lambda i: (i, 0))
    wide = pl.BlockSpec((ts, 3 * D), lambda i: (i, 0))
    return pl.pallas_call(
        body,
        name="bwd_o",
        grid=(s // ts,),
        in_specs=[tile, tile, tile, tile, _chunk_spec(ts, 3, width=3), _const_spec((D, D))],
        out_specs=[wide, _dproj_spec(ts, "g"), _const_spec((D, D))],
        out_shape=[jax.ShapeDtypeStruct((s, 3 * D), CDT), jax.ShapeDtypeStruct((s, N_IN), CDT),
                   jax.ShapeDtypeStruct((D, D), F32)],
        compiler_params=_cparams(("arbitrary",)),
    )(dx, pa, pb, pc, proj, wo_t)


def _bwd_a(proj, dp, lng, lnb, wm, wm_t, bias_full, wa_t, dproj):
    s = proj.shape[0]
    ts = min(TS_MIX, s)
    nt = s // ts

    def body(au_ref, av_ref, az_ref, dpa_ref, lng_ref, lnb_ref, wm_ref, wmt_ref, bias_ref, wat_ref, _dproj_in,
             da_ref, dwa_ref, dwm_ref, dbs_ref, dlng_ref, dlnb_ref,
             v_scr, mix_scr, dm_scr, dmf_scr, dv_scr, bsacc):
        _zero_first([dwa_ref, dwm_ref, dlng_ref, dlnb_ref, bsacc])
        au = au_ref[...].astype(F32)
        av = av_ref[...].astype(F32)
        az = az_ref[...].astype(F32)
        lng = lng_ref[...]
        f = _a_forward(au, av, az, lng, lnb_ref[...], wm_ref, bias_ref, v_scr, mix_scr)
        dpa = dpa_ref[...]
        dwa_ref[...] += lax.dot_general(f["ya"].astype(CDT), dpa, TN_DIMS, preferred_element_type=F32)
        dya = _dot(dpa, wat_ref[...])
        t1 = dya * f["mixed"]
        da_ref[:, 0:D] = (t1 * f["sz"] * _gelu_grad(au, f["tu"])).astype(CDT)
        da_ref[:, 2 * D:3 * D] = (t1 * f["u"] * _silu_grad(az, f["sg"])).astype(CDT)
        dmix = dya * f["u"] * f["sz"]
        dmf_scr[...] = dmix
        dm_scr[...] = dmix.astype(CDT)
        for n in range(ts // SGU_BLOCK):
            rows = slice(n * SGU_BLOCK, (n + 1) * SGU_BLOCK)
            for g in range(SGU_GROUPS):
                cols = slice(g * SGU_BLOCK, (g + 1) * SGU_BLOCK)
                dmb = dm_scr[rows, cols]
                dwm_ref[g] += lax.dot_general(dmb, v_scr[rows, cols], NT_DIMS, preferred_element_type=F32)
                dv_scr[rows, cols] = _dot(wmt_ref[g], dmb)
                bsacc[g] += dmf_scr[rows, cols]
        dvln = dv_scr[...]
        vhat = f["vhat"]
        dlng_ref[...] += _colsum(dvln * vhat)
        dlnb_ref[...] += _colsum(dvln)
        dvhat = dvln * lng
        dvg = f["rstd"] * (dvhat - _rowmean(dvhat) - vhat * _rowmean(dvhat * vhat))
        da_ref[:, D:2 * D] = (dvg * _gelu_grad(av, f["tv"])).astype(CDT)

        @pl.when(pl.program_id(0) == nt - 1)
        def _():
            ones = jnp.ones((8, SGU_BLOCK), F32)
            for g in range(SGU_GROUPS):
                red = lax.dot_general(ones, bsacc[g], NT_DIMS, preferred_element_type=F32,
                                      precision=lax.Precision.HIGHEST)
                dbs_ref[g:g + 1, :] = red[0:1, :]

    gshape = (SGU_GROUPS, SGU_BLOCK, SGU_BLOCK)
    return pl.pallas_call(
        body,
        name="bwd_a",
        grid=(nt,),
        in_specs=[
            _chunk_spec(ts, C_AU), _chunk_spec(ts, C_AV), _chunk_spec(ts, C_AZ), _chunk_spec(ts, 0),
            _const_spec((1, D)), _const_spec((1, D)), _const_spec(gshape), _const_spec(gshape),
            _const_spec((SGU_BLOCK, D)), _const_spec((D, D)), HBM_SPEC,
        ],
        out_specs=[
            _dproj_spec(ts, "a"), _const_spec((D, D)), _const_spec(gshape),
            _const_spec((SGU_GROUPS, SGU_BLOCK)), _const_spec((1, D)), _const_spec((1, D)),
        ],
        out_shape=[
            jax.ShapeDtypeStruct((s, N_IN), CDT), jax.ShapeDtypeStruct((D, D), F32),
            jax.ShapeDtypeStruct(gshape, F32), jax.ShapeDtypeStruct((SGU_GROUPS, SGU_BLOCK), F32),
            jax.ShapeDtypeStruct((1, D), F32), jax.ShapeDtypeStruct((1, D), F32),
        ],
        scratch_shapes=[
            pltpu.VMEM((ts, D), CDT), pltpu.VMEM((ts, D), F32), pltpu.VMEM((ts, D), CDT),
            pltpu.VMEM((ts, D), F32), pltpu.VMEM((ts, D), F32), pltpu.VMEM(gshape, F32),
        ],
        input_output_aliases={10: 0},
        compiler_params=_cparams(("arbitrary",)),
    )(proj, proj, proj, dp, lng, lnb, wm, wm_t, bias_full, wa_t, dproj)


def _bwd_b(proj, dp, pool_w, pool_w_t, pool_b, pool_scale, wb_t, dproj):
    s = proj.shape[0]
    ts = min(TS_MIX, s)
    nt = s // ts
    g = POOL_GDIM

    def body(bp_ref, bph_ref, bz_ref, dpb_ref, pw_ref, pwt_ref, pb_ref, ps_ref, wbt_ref, _dproj_in,
             db_ref, dwb_ref, dpw_ref, dpb_out_ref, dps_ref,
             pe, sa, sb, d_scr, lin_scr, ee, dl_scr, carry):
        _zero_first([dwb_ref, dpw_ref, dpb_out_ref, dps_ref, carry])
        tile = nt - 1 - pl.program_id(0)
        bz = bz_ref[...].astype(F32)
        pscale = ps_ref[...]
        f = _b_forward(bp_ref, bph_ref, bz, pw_ref, pb_ref[...], pscale, tile, pe, sa, sb, d_scr, lin_scr)
        dpb = dpb_ref[...]
        dwb_ref[...] += lax.dot_general(f["yb"].astype(CDT), dpb, TN_DIMS, preferred_element_type=F32)
        dyb = _dot(dpb, wbt_ref[...])
        t1 = dyb * f["lin"]
        dps_ref[...] += _colsum(t1 * f["sz"])
        db_ref[:, D:2 * D] = (t1 * pscale * _silu_grad(bz, f["sg"])).astype(CDT)
        dlin = dyb * pscale * f["sz"]
        dpb_out_ref[...] += _colsum(dlin)
        dl_scr[...] = dlin.astype(CDT)
        r = HALO + ts
        ee[ts:r, :] = carry[...]
        for k in range(POOL_GROUPS):
            cols = slice(k * g, (k + 1) * g)
            dlk = dl_scr[:, cols]
            dpw_ref[k] += lax.dot_general(d_scr[:, cols], dlk, TN_DIMS, preferred_element_type=F32)
            dd = _dot(dlk, pwt_ref[k])
            lin_scr[:, cols] = dd
            ee[0:ts, cols] = dd * f["invs"][k]
        carry[...] = ee[0:HALO, :]
        sa[0:ts + 24, :] = ee[0:ts + 24, :] + ee[1:ts + 25, :]
        sb[0:ts + 16, g:] = sa[0:ts + 16, g:] + sa[2:ts + 18, g:]
        sa[0:ts + 8, 2 * g:] = sb[0:ts + 8, 2 * g:] + sb[4:ts + 12, 2 * g:]
        sb[0:ts, 3 * g:] = sa[0:ts, 3 * g:] + sa[8:ts + 8, 3 * g:]
        sums = [sa[0:ts, 0:g], sb[0:ts, g:2 * g], sa[0:ts, 2 * g:3 * g], sb[0:ts, 3 * g:]]
        for k in range(POOL_GROUPS):
            cols = slice(k * g, (k + 1) * g)
            db_ref[:, cols] = (sums[k] - lin_scr[:, cols]).astype(CDT)

    ext = pltpu.VMEM((HALO + ts, D), F32)
    wshape = (POOL_GROUPS, g, g)
    return pl.pallas_call(
        body,
        name="bwd_b",
        grid=(nt,),
        in_specs=[
            _chunk_spec(ts, C_BP, rev_nt=nt), _halo_spec(ts, C_BP, rev_nt=nt), _chunk_spec(ts, C_BZ, rev_nt=nt),
            _chunk_spec(ts, 1, rev_nt=nt),
            _const_spec(wshape), _const_spec(wshape), _const_spec((1, D)), _const_spec((1, D)), _const_spec((D, D)),
            HBM_SPEC,
        ],
        out_specs=[
            _dproj_spec(ts, "b", rev_nt=nt), _const_spec((D, D)), _const_spec(wshape),
            _const_spec((1, D)), _const_spec((1, D)),
        ],
        out_shape=[
            jax.ShapeDtypeStruct((s, N_IN), CDT), jax.ShapeDtypeStruct((D, D), F32),
            jax.ShapeDtypeStruct(wshape, F32), jax.ShapeDtypeStruct((1, D), F32), jax.ShapeDtypeStruct((1, D), F32),
        ],
        scratch_shapes=[ext, ext, ext, pltpu.VMEM((ts, D), CDT), pltpu.VMEM((ts, D), F32), ext,
                        pltpu.VMEM((ts, D), CDT), pltpu.VMEM((HALO, D), F32)],
        input_output_aliases={9: 0},
        compiler_params=_cparams(("arbitrary",)),
    )(proj, proj, proj, dp, pool_w, pool_w_t, pool_b, pool_scale, wb_t, dproj)


def _bwd_c(proj, dp, conv_w, conv_b, wc_t, dproj):
    s = proj.shape[0]
    ts = min(TS_MIX, s)
    nt = s // ts

    def body(ch_ref, chh_ref, cb_ref, cc_ref, cch_ref, cz_ref, dpc_ref, cw_ref, cbias_ref, wct_ref, _dproj_in,
             dc_ref, dwc_ref, dcw_ref, dcb_ref, qe, de, carry):
        _zero_first([dwc_ref, dcw_ref, dcb_ref, carry])
        tile = nt - 1 - pl.program_id(0)
        cb = cb_ref[...].astype(F32)
        cz = cz_ref[...].astype(F32)
        f = _c_forward(ch_ref, chh_ref, cb, cc_ref, cch_ref, cz, cw_ref, cbias_ref[...], tile, qe)
        dpc = dpc_ref[...]
        dwc_ref[...] += lax.dot_general(f["yc"].astype(CDT), dpc, TN_DIMS, preferred_element_type=F32)
        dyc = _dot(dpc, wct_ref[...])
        t1 = dyc * f["conv"]
        dc_ref[:, D:2 * D] = (t1 * f["sz"]).astype(CDT)
        dc_ref[:, 3 * D:4 * D] = (t1 * cb * _silu_grad(cz, f["sg"])).astype(CDT)
        dconv = dyc * cb * f["sz"]
        r = HALO + ts
        dcb_ref[...] += _colsum(dconv)
        dcw_ref[0:1, :] += _colsum(dconv * qe[HALO - 2:r - 2, :])
        dcw_ref[1:2, :] += _colsum(dconv * qe[HALO - 1:r - 1, :])
        dcw_ref[2:3, :] += _colsum(dconv * qe[HALO:r, :])
        de[0:ts, :] = dconv
        de[ts:ts + 8, :] = carry[...]
        carry[...] = de[0:8, :]
        dq = cw_ref[2:3, :] * dconv + cw_ref[1:2, :] * de[1:ts + 1, :] + cw_ref[0:1, :] * de[2:ts + 2, :]
        dc_ref[:, 0:D] = (dq * cc_ref[...].astype(F32)).astype(CDT)
        dc_ref[:, 2 * D:3 * D] = (dq * ch_ref[...].astype(F32)).astype(CDT)

    return pl.pallas_call(
        body,
        name="bwd_c",
        grid=(nt,),
        in_specs=[
            _chunk_spec(ts, C_CH, rev_nt=nt), _halo_spec(ts, C_CH, rev_nt=nt), _chunk_spec(ts, C_CB, rev_nt=nt),
            _chunk_spec(ts, C_CC, rev_nt=nt), _halo_spec(ts, C_CC, rev_nt=nt), _chunk_spec(ts, C_CZ, rev_nt=nt),
            _chunk_spec(ts, 2, rev_nt=nt),
            _const_spec((3, D)), _const_spec((1, D)), _const_spec((D, D)), HBM_SPEC,
        ],
        out_specs=[
            _dproj_spec(ts, "c", rev_nt=nt), _const_spec((D, D)), _const_spec((8, D)),
            _const_spec((1, D)),
        ],
        out_shape=[
            jax.ShapeDtypeStruct((s, N_IN), CDT), jax.ShapeDtypeStruct((D, D), F32),
            jax.ShapeDtypeStruct((8, D), F32), jax.ShapeDtypeStruct((1, D), F32),
        ],
        scratch_shapes=[pltpu.VMEM((HALO + ts, D), F32), pltpu.VMEM((ts + 8, D), F32), pltpu.VMEM((8, D), F32)],
        input_output_aliases={10: 0},
        compiler_params=_cparams(("arbitrary",)),
    )(proj, proj, proj, proj, proj, proj, dp, conv_w, conv_b, wc_t, dproj)


TK_DX = 2 * D


def _bwd_proj_dx(dproj, w_in_t, x, g_row, dx_out, scat, gath):
    s = x.shape[0]
    ts = min(TS_PROJ, s)
    nk = N_IN // TK_DX
    nt = s // ts
    ns, nx = len(scat), len(scat) + len(gath)

    def body(*refs):
        d_ref, wt_ref, x_ref, g_ref, dxo_ref = refs[:5]
        ex_ins = refs[5:5 + nx]
        dxi_ref, dng_ref = refs[5 + nx:7 + nx]
        ex_outs = refs[7 + nx:7 + 2 * nx]
        acc = refs[7 + 2 * nx]
        sems = refs[8 + 2 * nx:]
        i, k = pl.program_id(0), pl.program_id(1)

        def exchange():
            return _exchange_copies(ex_ins[:ns], ex_outs[:ns], ex_ins[ns:], ex_outs[ns:], *sems)

        @pl.when((i == 0) & (k == 0))
        def _():
            dng_ref[...] = jnp.zeros_like(dng_ref)
            for cp in exchange():
                cp.start()

        @pl.when(k == 0)
        def _():
            acc[...] = jnp.zeros_like(acc)

        acc[...] += _dot(d_ref[...], wt_ref[...])

        @pl.when(k == nk - 1)
        def _():
            xf = x_ref[...]
            g = g_ref[...]
            dh = acc[...]
            r = lax.rsqrt(_rowmean(xf * xf) + RMS_EPS)
            xhat = xf * r
            dng_ref[...] += _colsum(dh * xhat)
            dxhat = dh * g
            dxi_ref[...] = dxo_ref[...] + r * (dxhat - xhat * _rowmean(dxhat * xhat))

        @pl.when((i == nt - 1) & (k == nk - 1))
        def _():
            for cp in exchange():
                cp.wait()

    tile = pl.BlockSpec((ts, D), lambda i, k: (i, 0))
    row = pl.BlockSpec((1, D), lambda i, k: (0, 0))
    return pl.pallas_call(
        body,
        name="bwd_proj_dx_exchange",
        grid=(nt, nk),
        in_specs=[
            pl.BlockSpec((ts, TK_DX), lambda i, k: (i, k)), pl.BlockSpec((TK_DX, D), lambda i, k: (k, 0)),
            tile, row, tile,
        ] + [HBM_SPEC] * nx,
        out_specs=[tile, row] + [HBM_SPEC] * nx,
        out_shape=[jax.ShapeDtypeStruct((s, D), F32), jax.ShapeDtypeStruct((1, D), F32)]
        + _exchange_shapes(scat, gath),
        scratch_shapes=[pltpu.VMEM((ts, D), F32)] + _exchange_sems(nx),
        compiler_params=_cparams(("arbitrary", "arbitrary")),
    )(dproj, w_in_t, x, g_row, dx_out, *scat, *gath)


TN_DW = 512


def _bwd_proj_dw(h_t, dproj):
    s = h_t.shape[1]
    tk = min(8192, s)
    per_shard = SHARD_N // TN_DW
    per_chunk = D // TN_DW
    nk = s // tk

    def body(ht_ref, d_ref, out_ref, acc):
        k = pl.program_id(1)

        @pl.when(k == 0)
        def _():
            acc[...] = jnp.zeros_like(acc)

        acc[...] += _dot(ht_ref[...], d_ref[...])

        @pl.when(k == nk - 1)
        def _():
            out_ref[...] = acc[...].astype(CDT)

    def out_index(n, k):
        chunk = n // per_chunk
        proj_chunk = sum(jnp.where(chunk == c, p, 0) for c, p in enumerate(DPROJ_PERM))
        col = proj_chunk * per_chunk + n % per_chunk
        return (col // per_shard, 0, col % per_shard)

    return pl.pallas_call(
        body,
        name="bwd_proj_dw",
        grid=(N_IN // TN_DW, nk),
        in_specs=[pl.BlockSpec((D, tk), lambda n, k: (0, k)), pl.BlockSpec((tk, TN_DW), lambda n, k: (k, n))],
        out_specs=pl.BlockSpec((None, D, TN_DW), out_index),
        out_shape=jax.ShapeDtypeStruct((NDEV, D, SHARD_N), CDT),
        scratch_shapes=[pltpu.VMEM((D, TN_DW), F32)],
        compiler_params=_cparams(("arbitrary", "arbitrary")),
    )(h_t, dproj)


def _adamw(recvs, w, m, v, tr, name):
    nq = len(recvs)
    _, rows, c = recvs[0].shape
    assert rows % tr == 0 and w.shape == (nq * rows, c)
    steps = rows // tr

    def body(*refs):
        rrefs = refs[:nq]
        w_ref, m_ref, v_ref, g_out, d_out, m_out, v_out = refs[nq:]
        q = pl.program_id(0)
        for qi in range(nq):
            @pl.when(q == qi)
            def _(r=rrefs[qi]):
                g = r[0].astype(F32)
                for j in range(1, NDEV):
                    g = g + r[j].astype(F32)
                wv = w_ref[...]
                m2 = ADAM_B1 * m_ref[...] + (1.0 - ADAM_B1) * g
                v2 = ADAM_B2 * v_ref[...] + (1.0 - ADAM_B2) * (g * g)
                m_hat = m2 / (1.0 - ADAM_B1 ** ADAM_STEP)
                v_hat = v2 / (1.0 - ADAM_B2 ** ADAM_STEP)
                g_out[...] = g
                d_out[...] = -ADAM_LR * (m_hat / (jnp.sqrt(v_hat) + ADAM_EPS) + ADAM_WD * wv)
                m_out[...] = m2
                v_out[...] = v2

    def rspec(qi):
        return pl.BlockSpec((NDEV, tr, c), lambda q, t: (0, jnp.where(q == qi, t, 0), 0))

    tile = pl.BlockSpec((tr, c), lambda q, t: (q * steps + t, 0))
    shp = jax.ShapeDtypeStruct(w.shape, F32)
    return pl.pallas_call(
        body,
        name=name,
        grid=(nq, steps),
        in_specs=[rspec(qi) for qi in range(nq)] + [tile, tile, tile],
        out_specs=[tile] * 4,
        out_shape=[shp] * 4,
        compiler_params=_cparams(("arbitrary", "arbitrary")),
    )(*recvs, w, m, v)


SMALL_ROWS = 264
REP_NAMES = ("sgu_ln_g", "sgu_ln_b", "sgu_b", "pool_b", "pool_scale", "conv_b")


def _pack_small(pool_part, conv_part):
    n = pool_part.shape[0]
    flat = jnp.concatenate([pool_part.reshape(n, -1), conv_part.reshape(n, -1)], axis=1)
    return jnp.pad(flat, ((0, 0), (0, SMALL_ROWS * 128 - flat.shape[1]))).reshape(n, SMALL_ROWS, 128)


def _unpack_small(packed):
    flat = packed.reshape(DEPTH, SMALL_ROWS * 128)
    n_pool = POOL_GROUPS * (POOL_GDIM // NDEV) * POOL_GDIM
    pool = flat[:, :n_pool].reshape(DEPTH, POOL_GROUPS, POOL_GDIM // NDEV, POOL_GDIM)
    conv = flat[:, n_pool:n_pool + 3 * (D // NDEV)].reshape(DEPTH, 3, D // NDEV)
    return pool, conv


REP_LAYER_ROWS = len(REP_NAMES) * 8 + SGU_GROUPS * SGU_BLOCK
NORM_ROWS = 8 * (DEPTH + 1)


def _pack_rep_layer(vals, sgu_w_l):
    parts = [vals[nme].reshape(8, 128) for nme in REP_NAMES]
    parts.append(sgu_w_l.reshape(SGU_GROUPS * SGU_BLOCK, SGU_BLOCK))
    return jnp.concatenate(parts, axis=0)


def _pack_rep(per_layer, sgu_w):
    vecs = jnp.stack([per_layer[nme].reshape(DEPTH, 8, 128) for nme in REP_NAMES], axis=1)
    vecs = vecs.reshape(DEPTH, 8 * len(REP_NAMES), 128)
    mats = sgu_w.reshape(DEPTH, SGU_GROUPS * SGU_BLOCK, SGU_BLOCK)
    return jnp.concatenate([vecs, mats], axis=1).reshape(DEPTH * REP_LAYER_ROWS, 128)


def _unpack_rep(packed):
    p = packed.reshape(DEPTH, REP_LAYER_ROWS, 128)
    nv = 8 * len(REP_NAMES)
    vecs = p[:, :nv].reshape(DEPTH, len(REP_NAMES), D)
    out = {nme: vecs[:, q] for q, nme in enumerate(REP_NAMES)}
    out["sgu_b"] = out["sgu_b"].reshape(DEPTH, SGU_GROUPS, SGU_BLOCK)
    return out, p[:, nv:].reshape(DEPTH, SGU_GROUPS, SGU_BLOCK, SGU_BLOCK)


def _pack_norm(norm_g_like, final_g_like):
    return jnp.concatenate([norm_g_like.reshape(DEPTH * 8, 128), final_g_like.reshape(8, 128)], axis=0)


def _unpack_norm(packed):
    return packed[:DEPTH * 8].reshape(DEPTH, D), packed[DEPTH * 8:].reshape(D)


def kernel(x, norm_g, w_in, sgu_ln_g, sgu_ln_b, sgu_w, sgu_b, pool_w, pool_b, pool_scale, conv_w, conv_b, w_branch_a, w_branch_b, w_branch_c, w_out, final_g, loss_target, m_norm_g, m_w_in, m_sgu_ln_g, m_sgu_ln_b, m_sgu_w, m_sgu_b, m_pool_w, m_pool_b, m_pool_scale, m_conv_w, m_conv_b, m_w_branch_a, m_w_branch_b, m_w_branch_c, m_w_out, m_final_g, v_norm_g, v_w_in, v_sgu_ln_g, v_sgu_ln_b, v_sgu_w, v_sgu_b, v_pool_w, v_pool_b, v_pool_scale, v_conv_w, v_conv_b, v_w_branch_a, v_w_branch_b, v_w_branch_c, v_w_out, v_final_g):
    assert x.shape[0] == 1 and x.shape[2] == D
    s = x.shape[1]
    x0 = x.reshape(s, D)
    target = loss_target.reshape(s, D)

    w_br = jnp.stack([w_branch_a, w_branch_b, w_branch_c, w_out], axis=1)
    shards = [[w_in[l].astype(CDT), w_br[l].astype(CDT), pool_w[l].astype(CDT), conv_w[l]] for l in range(DEPTH)]

    def layout(gathered):
        g_in, g_br, g_pool, g_conv = gathered
        mats = jnp.transpose(g_br, (1, 0, 2, 3)).reshape(4, D, D)
        pool_full = jnp.transpose(g_pool, (1, 0, 2, 3)).reshape(POOL_GROUPS, POOL_GDIM, POOL_GDIM)
        w_in_t = jnp.transpose(g_in, (0, 2, 1)).reshape(N_IN // D, D, D)
        return dict(
            g_in=g_in,
            w_in_t=jnp.concatenate([w_in_t[p] for p in DPROJ_PERM], axis=0),
            mats=mats, mats_t=jnp.transpose(mats, (0, 2, 1)),
            pool=pool_full, pool_t=jnp.transpose(pool_full, (0, 2, 1)),
            conv=jnp.transpose(g_conv, (1, 0, 2)).reshape(3, D),
        )

    pos = np.arange(SGU_BLOCK) // CHUNK
    mask = jnp.asarray(pos[None, :] <= pos[:, None])
    wm_all = jnp.where(mask[None, None], sgu_w, 0.0).astype(CDT)
    wm_all_t = jnp.transpose(wm_all, (0, 1, 3, 2))
    bias_all = jnp.repeat(jnp.transpose(sgu_b, (0, 2, 1)), SGU_BLOCK, axis=2)

    def row(a, l):
        return a[l].reshape(1, D)

    g_in0 = _gather_two_level(shards[0][0], "gather_w_in0")
    weights = [None] * DEPTH
    saved = []
    xl = x0
    for l in range(DEPTH):
        g_in = g_in0 if l == 0 else weights[l]["g_in"]
        behind = (shards[0][1:] if l == 0 else []) + (shards[l + 1] if l + 1 < DEPTH else [])
        proj, h_t, *got = _fwd_proj(xl, row(norm_g, l), g_in, behind)
        if l == 0:
            weights[0] = layout([g_in0] + got[:3])
            got = got[3:]
        if l + 1 < DEPTH:
            weights[l + 1] = layout(got)
        w = weights[l]
        pa = _fwd_a(proj, row(sgu_ln_g, l), row(sgu_ln_b, l), wm_all[l], bias_all[l], w["mats"][0])
        pb = _fwd_b(proj, w["pool"], row(pool_b, l), row(pool_scale, l), w["mats"][1])
        pc, x_next = _fwd_c_o(proj, w["conv"], row(conv_b, l), w["mats"][2], xl, pa, pb, w["mats"][3])
        saved.append((xl, proj, h_t, pa, pb, pc))
        xl = x_next

    dx, loss_cols, dfinal_g = _loss_head(xl, target, final_g.reshape(1, D))
    loss = lax.psum(0.5 / D * jnp.sum(loss_cols), ("x", "y", "c"))

    norm_grads = [None] * DEPTH
    recv_in, recv_br, recv_small, recv_rep = ([None] * DEPTH for _ in range(4))
    for l in reversed(range(DEPTH)):
        w = weights[l]
        xl, proj, h_t, pa, pb, pc = saved[l]
        dp, dproj, dwo = _bwd_o(dx, pa, pb, pc, proj, w["mats_t"][3])
        dproj, dwa, dwm, dbs, dlng, dlnb = _bwd_a(
            proj, dp, row(sgu_ln_g, l), row(sgu_ln_b, l), wm_all[l], wm_all_t[l], bias_all[l], w["mats_t"][0], dproj)
        dproj, dwb, dpw, dpb, dps = _bwd_b(
            proj, dp, w["pool"], w["pool_t"], row(pool_b, l), row(pool_scale, l), w["mats_t"][1], dproj)
        dproj, dwc, dcw, dcb = _bwd_c(proj, dp, w["conv"], row(conv_b, l), w["mats_t"][2], dproj)
        dw_in = _bwd_proj_dw(h_t, dproj)

        dw_br = jnp.stack([dwa, dwb, dwc, dwo]).astype(CDT).reshape(4, NDEV, D // NDEV, D).transpose(1, 0, 2, 3)
        dpw_j = dpw.reshape(POOL_GROUPS, NDEV, POOL_GDIM // NDEV, POOL_GDIM).transpose(1, 0, 2, 3)
        dcw_j = dcw[0:3].reshape(3, NDEV, D // NDEV).transpose(1, 0, 2)
        small = _pack_small(dpw_j, dcw_j)
        rep = _pack_rep_layer(
            dict(sgu_ln_g=dlng, sgu_ln_b=dlnb, sgu_b=dbs, pool_b=dpb, pool_scale=dps, conv_b=dcb),
            jnp.where(mask[None], dwm, 0.0))
        dx, norm_grads[l], recv_in[l], recv_br[l], recv_small[l], recv_rep[l] = _bwd_proj_dx(
            dproj, w["w_in_t"], xl, row(norm_g, l), dx, [dw_in, dw_br, small], [rep])

    (recv_norm,) = _exchange([], [_pack_norm(jnp.stack(norm_grads), dfinal_g)], "gather_norm_grads")

    rep_w = dict(sgu_ln_g=sgu_ln_g, sgu_ln_b=sgu_ln_b, sgu_b=sgu_b, pool_b=pool_b, pool_scale=pool_scale, conv_b=conv_b)
    rep_m = dict(sgu_ln_g=m_sgu_ln_g, sgu_ln_b=m_sgu_ln_b, sgu_b=m_sgu_b, pool_b=m_pool_b, pool_scale=m_pool_scale,
                 conv_b=m_conv_b)
    rep_v = dict(sgu_ln_g=v_sgu_ln_g, sgu_ln_b=v_sgu_ln_b, sgu_b=v_sgu_b, pool_b=v_pool_b, pool_scale=v_pool_scale,
                 conv_b=v_conv_b)
    rep_out = _adamw(recv_rep, _pack_rep(rep_w, sgu_w), _pack_rep(rep_m, m_sgu_w), _pack_rep(rep_v, v_sgu_w),
                     REP_LAYER_ROWS // 2, "adamw_replicated")
    rep_res = [_unpack_rep(o) for o in rep_out]
    norm_out = _adamw([recv_norm], _pack_norm(norm_g, final_g), _pack_norm(m_norm_g, m_final_g),
                      _pack_norm(v_norm_g, v_final_g), NORM_ROWS, "adamw_norm")
    norm_res = [_unpack_norm(o) for o in norm_out]

    in_out = _adamw(recv_in, w_in.reshape(DEPTH * D, SHARD_N), m_w_in.reshape(DEPTH * D, SHARD_N),
                    v_w_in.reshape(DEPTH * D, SHARD_N), 128, "adamw_w_in")
    in_res = [o.reshape(DEPTH, D, SHARD_N) for o in in_out]

    def stack_br(a, b, c, o):
        return jnp.stack([a, b, c, o], axis=1).reshape(DEPTH * 4 * (D // NDEV), D)

    br_out = _adamw(
        [r.reshape(NDEV, 4 * (D // NDEV), D) for r in recv_br],
        stack_br(w_branch_a, w_branch_b, w_branch_c, w_out),
        stack_br(m_w_branch_a, m_w_branch_b, m_w_branch_c, m_w_out),
        stack_br(v_w_branch_a, v_w_branch_b, v_w_branch_c, v_w_out), 128, "adamw_w_branch")
    br_res = [o.reshape(DEPTH, 4, D // NDEV, D) for o in br_out]

    def stack_small(pw, cw):
        return _pack_small(pw, cw).reshape(DEPTH * SMALL_ROWS, 128)

    small_out = _adamw(recv_small, stack_small(pool_w, conv_w), stack_small(m_pool_w, m_conv_w),
                       stack_small(v_pool_w, v_conv_w), SMALL_ROWS, "adamw_small")
    small_res = [_unpack_small(o) for o in small_out]

    def leaves(kind):
        rep, sgu_w_k = rep_res[kind]
        norm_k, final_k = norm_res[kind]
        br = br_res[kind]
        pool_k, conv_k = small_res[kind]
        return [norm_k, in_res[kind], rep["sgu_ln_g"], rep["sgu_ln_b"], sgu_w_k, rep["sgu_b"], pool_k,
                rep["pool_b"], rep["pool_scale"], conv_k, rep["conv_b"], br[:, 0], br[:, 1], br[:, 2], br[:, 3],
                final_k]

    return (loss, dx.reshape(1, s, D), *leaves(0), *leaves(1), *leaves(2), *leaves(3))
```
